```python
import jax, jax.numpy as jnp
from jax import lax
import numpy as np

D_MODEL = 1024
BATCH = 4
SEQ = 8192
DEPTH = 4
DEC_BATCH = 8
DEC_SEQ = 64
PAST_LEN = 4096

CHUNK = 64
W_RG = 512
RG_BLOCKS = 8
RG_BW = W_RG // RG_BLOCKS
CONV_W = 4
RG_C = 8.0
HG_HEADS = 4
HG_DK = 128
HG_DV = 128
W_HG = HG_HEADS * HG_DK
SPLITS = [W_RG, 2 * W_RG, 2 * W_RG + W_HG, 2 * W_RG + 2 * W_HG, 2 * W_RG + 3 * W_HG]
IN_COLS = 2 * W_RG + 4 * W_HG
N_EXPERTS = 32
TOP_K = 4
D_FF = 1024
SWIGLU_LIMIT = 7.0
SWIGLU_ALPHA = 1.702
MOE_BLOCK = 128
EPS = 1e-6

kernel_name = 'hybrid_rglru_hgrn2_moe_stream_step'


def rms_norm(x, g):
    xf = x.astype(jnp.float32)
    y = xf * lax.rsqrt(jnp.mean(xf * xf, axis=-1, keepdims=True) + EPS)
    return (y * g.astype(jnp.float32)).astype(x.dtype)


def causal_conv(x, buf, w, b):
    t = x.shape[1]
    xc = jnp.concatenate([buf.astype(x.dtype), x], axis=1)
    y = b
    for j in range(CONV_W):
        y = y + xc[:, j:j + t] * w[j]
    return y, xc[:, -(CONV_W - 1):]


def rg_lru(x, h0, w_r, b_r, w_i, b_i, lam):
    bsz, t, _ = x.shape
    xf = x.astype(jnp.float32)
    xb = xf.reshape(bsz, t, RG_BLOCKS, RG_BW)
    gate_r = jax.nn.sigmoid(jnp.einsum('btnc,ncd->btnd', xb, w_r.astype(jnp.float32)).reshape(bsz, t, W_RG) + b_r)
    gate_i = jax.nn.sigmoid(jnp.einsum('btnc,ncd->btnd', xb, w_i.astype(jnp.float32)).reshape(bsz, t, W_RG) + b_i)
    log_a = -RG_C * gate_r * jax.nn.softplus(-lam.astype(jnp.float32))
    a = jnp.exp(log_a)
    u = jnp.sqrt(-jnp.expm1(2.0 * log_a)) * gate_i * xf
    u = u.at[:, 0].add(a[:, 0] * h0.astype(jnp.float32))

    def combine(left, right):
        a1, b1 = left
        a2, b2 = right
        return a1 * a2, a2 * b1 + b2

    _, h = lax.associative_scan(combine, (a, u), axis=1)
    return h, h[:, -1]


def hgrn2_chunked(q, k, v, log_f, s0):
    bsz, t, h, _ = q.shape
    blk = min(CHUNK, t)
    nb = t // blk

    def to_blocks(a):
        return a.reshape(bsz, nb, blk, h, a.shape[-1]).transpose(1, 0, 3, 2, 4)

    causal = jnp.tril(jnp.ones((blk, blk), dtype=bool))[:, :, None]

    def step(s, inp):
        qb, kb, vb, gb = inp
        g_cum = jnp.cumsum(gb, axis=2)
        diff = g_cum[:, :, :, None, :] - g_cum[:, :, None, :, :]
        decay = jnp.where(causal, jnp.exp(jnp.where(causal, diff, 0.0)), 0.0)
        scores = jnp.einsum('bhtd,bhsd,bhtsd->bhts', qb, kb, decay)
        o_intra = jnp.einsum('bhts,bhsv->bhtv', scores, vb)
        o_inter = jnp.einsum('bhtd,bhdv->bhtv', qb * jnp.exp(g_cum), s)
        k_dec = kb * jnp.exp(g_cum[:, :, -1:, :] - g_cum)
        s_new = jnp.exp(g_cum[:, :, -1, :])[..., None] * s + jnp.einsum('bhsd,bhsv->bhdv', k_dec, vb)
        return s_new, o_intra + o_inter

    s_last, o = lax.scan(step, s0.astype(jnp.float32), (to_blocks(q), to_blocks(k), to_blocks(v), to_blocks(log_f)))
    o = o.transpose(1, 0, 3, 2, 4).reshape(bsz, t, h, v.shape[-1])
    return o, s_last


def moe_ffn(x2, w_router, b_router, w_gate, b_gate, w_up, b_up, w_down, b_down):
    n, d = x2.shape
    logits = x2.astype(jnp.float32) @ w_router.astype(jnp.float32) + b_router
    top_val, top_idx = lax.top_k(logits, TOP_K)
    probs = jax.nn.softmax(top_val, axis=-1)
    nk = n * TOP_K
    flat_e = top_idx.reshape(nk)
    flat_tok = jnp.arange(nk, dtype=jnp.int32) // TOP_K
    order = jnp.argsort(flat_e)
    sorted_e = flat_e[order]
    counts = jnp.bincount(flat_e, length=N_EXPERTS)
    padded = (counts + MOE_BLOCK - 1) // MOE_BLOCK * MOE_BLOCK
    pad_end = jnp.cumsum(padded)
    pad_start = pad_end - padded
    start = jnp.cumsum(counts) - counts
    dest = pad_start[sorted_e] + jnp.arange(nk, dtype=jnp.int32) - start[sorted_e]
    n_rows = -(-(nk + N_EXPERTS * (MOE_BLOCK - 1)) // MOE_BLOCK) * MOE_BLOCK
    n_blocks = n_rows // MOE_BLOCK
    row_tok = jnp.full((n_rows,), n, jnp.int32).at[dest].set(flat_tok[order])
    x_rows = jnp.concatenate([x2, jnp.zeros((1, d), x2.dtype)], axis=0)[row_tok]
    x_rows = x_rows.reshape(n_blocks, MOE_BLOCK, d)
    blk_e = jnp.minimum(jnp.searchsorted(pad_end, jnp.arange(n_blocks) * MOE_BLOCK, side='right'), N_EXPERTS - 1)

    def expert_block(args):
        xb, e = args
        gate = jnp.minimum(xb @ w_gate[e] + b_gate[e], SWIGLU_LIMIT)
        up = jnp.clip(xb @ w_up[e] + b_up[e], -SWIGLU_LIMIT, SWIGLU_LIMIT)
        glu = gate * jax.nn.sigmoid(SWIGLU_ALPHA * gate)
        return ((up + 1.0) * glu) @ w_down[e] + b_down[e]

    y_rows = lax.map(expert_block, (x_rows, blk_e)).reshape(n_rows, d)
    y_pairs = jnp.zeros((nk, d), y_rows.dtype).at[order].set(y_rows[dest])
    return jnp.einsum('nkd,nk->nd', y_pairs.reshape(n, TOP_K, d), probs.astype(y_pairs.dtype))


def hybrid_layer(x, c, conv_buf, h0, s0, lb, w_ada, b_ada, g_norm1, g_norm2, w_in, conv_w, conv_b,
                 w_rgate, b_rgate, w_igate, b_igate, lru_lambda, hgrn_norm_g, w_out,
                 w_router, b_router, w_gate, b_gate, w_up, b_up, w_down, b_down):
    bsz, t, d = x.shape
    mod = (jax.nn.silu(c) @ w_ada + b_ada)[:, None, :]
    sh1, sc1, gt1, sh2, sc2, gt2 = jnp.split(mod, 6, axis=-1)
    hn = rms_norm(x, g_norm1) * (1.0 + sc1) + sh1
    proj = hn @ w_in
    x_rg, y_rg, q, f, i, g = jnp.split(proj, SPLITS, axis=-1)
    xc, conv_new = causal_conv(x_rg, conv_buf, conv_w, conv_b)
    hs, h_last = rg_lru(xc, h0, w_rgate, b_rgate, w_igate, b_igate, lru_lambda)
    o_rg = jax.nn.gelu(y_rg) * hs.astype(x.dtype)
    heads = lambda a: a.reshape(bsz, t, HG_HEADS, HG_DK).astype(jnp.float32)
    ff = heads(f)
    lbh = lb.reshape(HG_HEADS, HG_DK)
    log_f = jax.nn.log_sigmoid(ff) + jnp.log1p(lbh * jnp.exp(-ff))
    k = (1.0 - lbh) * jax.nn.sigmoid(-ff)
    o_hg, s_last = hgrn2_chunked(heads(q), k, heads(i), log_f, s0)
    o_hg = (rms_norm(o_hg, hgrn_norm_g) * jax.nn.silu(heads(g))).astype(x.dtype).reshape(bsz, t, W_HG)
    mix = jnp.concatenate([o_rg, o_hg], axis=-1) @ w_out
    x = x + gt1 * mix
    hn2 = rms_norm(x, g_norm2) * (1.0 + sc2) + sh2
    ffn = moe_ffn(hn2.reshape(bsz * t, d), w_router, b_router, w_gate, b_gate, w_up, b_up, w_down, b_down)
    x = x + gt2 * ffn.reshape(bsz, t, d).astype(x.dtype)
    return x, conv_new, h_last, s_last


def setup_inputs(seed: int = 0) -> dict:
    key = jax.random.key(seed)
    ks = iter(jax.random.split(key, 40))
    nrm = lambda shape, scale: jax.random.normal(next(ks), shape, jnp.float32) * scale
    u = jax.random.uniform(next(ks), (DEPTH, W_RG), jnp.float32, 0.9, 0.999)
    a0 = u ** (1.0 / RG_C)
    return {
        'x_prompt': nrm((BATCH, SEQ, D_MODEL), 1.0),
        'x_sample': nrm((DEC_BATCH, DEC_SEQ, D_MODEL), 1.0),
        'state_conv': nrm((DEPTH, DEC_BATCH, CONV_W - 1, W_RG), 0.5),
        'state_rglru': nrm((DEPTH, DEC_BATCH, W_RG), 0.5),
        'state_hgrn': nrm((DEPTH, DEC_BATCH, HG_HEADS, HG_DK, HG_DV), 0.3),
        'c_prompt': nrm((BATCH, D_MODEL), 1.0),
        'c_sample': nrm((DEC_BATCH, D_MODEL), 1.0),
        'w_ada': nrm((DEPTH, D_MODEL, 6 * D_MODEL), 0.5 * D_MODEL ** -0.5),
        'b_ada': nrm((DEPTH, 6 * D_MODEL), 0.01),
        'g_norm1': 1.0 + nrm((DEPTH, D_MODEL), 0.02),
        'g_norm2': 1.0 + nrm((DEPTH, D_MODEL), 0.02),
        'w_in': nrm((DEPTH, D_MODEL, IN_COLS), D_MODEL ** -0.5),
        'conv_w': nrm((DEPTH, CONV_W, W_RG), CONV_W ** -0.5),
        'conv_b': nrm((DEPTH, W_RG), 0.01),
        'w_rgate': nrm((DEPTH, RG_BLOCKS, RG_BW, RG_BW), RG_BW ** -0.5),
        'b_rgate': nrm((DEPTH, W_RG), 0.01),
        'w_igate': nrm((DEPTH, RG_BLOCKS, RG_BW, RG_BW), RG_BW ** -0.5),
        'b_igate': nrm((DEPTH, W_RG), 0.01),
        'lru_lambda': jnp.log(a0) - jnp.log1p(-a0),
        'hgrn_lb': nrm((DEPTH, W_HG), 0.5),
        'hgrn_norm_g': 1.0 + nrm((DEPTH, HG_DV), 0.02),
        'w_out': nrm((DEPTH, D_MODEL, D_MODEL), D_MODEL ** -0.5),
        'w_router': nrm((DEPTH, D_MODEL, N_EXPERTS), D_MODEL ** -0.5),
        'b_router': nrm((DEPTH, N_EXPERTS), 0.01),
        'w_gate': nrm((DEPTH, N_EXPERTS, D_MODEL, D_FF), D_MODEL ** -0.5),
        'b_gate': nrm((DEPTH, N_EXPERTS, D_FF), 0.01),
        'w_up': nrm((DEPTH, N_EXPERTS, D_MODEL, D_FF), D_MODEL ** -0.5),
        'b_up': nrm((DEPTH, N_EXPERTS, D_FF), 0.01),
        'w_down': nrm((DEPTH, N_EXPERTS, D_FF, D_MODEL), D_FF ** -0.5),
        'b_down': nrm((DEPTH, N_EXPERTS, D_MODEL), 0.01),
        'g_final': 1.0 + nrm((D_MODEL,), 0.02),
    }


def reference(x_prompt, x_sample, state_conv, state_rglru, state_hgrn, c_prompt, c_sample,
              w_ada, b_ada, g_norm1, g_norm2, w_in, conv_w, conv_b, w_rgate, b_rgate,
              w_igate, b_igate, lru_lambda, hgrn_lb, hgrn_norm_g, w_out, w_router, b_router,
              w_gate, b_gate, w_up, b_up, w_down, b_down, g_final):
    p = jax.nn.softmax(hgrn_lb.astype(jnp.float32), axis=0)
    lb_all = jnp.cumsum(p, axis=0) - p[0]

    def run_trunk(x, c, conv_st, h_st, s_st):
        convs, hs, ss = [], [], []
        for l in range(DEPTH):
            x, cb, hl, sl = hybrid_layer(
                x, c, conv_st[l], h_st[l], s_st[l], lb_all[l], w_ada[l], b_ada[l], g_norm1[l], g_norm2[l],
                w_in[l], conv_w[l], conv_b[l], w_rgate[l], b_rgate[l], w_igate[l], b_igate[l],
                lru_lambda[l], hgrn_norm_g[l], w_out[l], w_router[l], b_router[l],
                w_gate[l], b_gate[l], w_up[l], b_up[l], w_down[l], b_down[l])
            convs.append(cb)
            hs.append(hl)
            ss.append(sl)
        return rms_norm(x, g_final), jnp.stack(convs), jnp.stack(hs), jnp.stack(ss)

    bp = x_prompt.shape[0]
    zero_conv = jnp.zeros((DEPTH, bp, CONV_W - 1, W_RG), x_prompt.dtype)
    zero_h = jnp.zeros((DEPTH, bp, W_RG), jnp.float32)
    zero_s = jnp.zeros((DEPTH, bp, HG_HEADS, HG_DK, HG_DV), jnp.float32)
    y_prompt, conv_p, rglru_p, hgrn_p = run_trunk(x_prompt, c_prompt, zero_conv, zero_h, zero_s)
    y_sample, conv_s, rglru_s, hgrn_s = run_trunk(x_sample, c_sample, state_conv, state_rglru, state_hgrn)
    return (y_prompt, y_sample, conv_p, rglru_p, hgrn_p, conv_s, rglru_s, hgrn_s)
```

```python
import functools
import math

import jax
import jax.numpy as jnp
from jax import lax
from jax.experimental import pallas as pl
from jax.experimental.pallas import tpu as pltpu

F32 = jnp.float32
BF16 = jnp.bfloat16
I32 = jnp.int32
HIGHEST = lax.Precision.HIGHEST

EPS = 1e-6
RG_C = 8.0
CONV_W = 4
HG_HEADS = 4
HG_D = 128
TOP_K = 4
SWIGLU_LIMIT = 7.0
SWIGLU_ALPHA = 1.702

V7X_LANES = 128
V7X_SUBLANES = 8
V7X_VMEM_LIMIT_BYTES = 56 * 1024 * 1024

ROW_TILE = 256
HGRN_CHUNK = 256
FFN_TILE = 256
ADA_COLS = 1536

NT_DIMS = (((1,), (1,)), ((), ()))
TN_DIMS = (((0,), (0,)), ((), ()))


def _params(sem, vmem=V7X_VMEM_LIMIT_BYTES):
    return pltpu.CompilerParams(dimension_semantics=sem, vmem_limit_bytes=vmem)


def _rms(x, g):
    ms = jnp.mean(x * x, axis=-1, keepdims=True)
    return x * lax.rsqrt(ms + EPS) * g


def _mod_row(mod_ref, k, rows):
    nb, _, d = mod_ref.shape
    if nb == 1:
        return mod_ref[0, k:k + 1, :]
    per = rows // nb
    return jnp.concatenate([jnp.broadcast_to(mod_ref[b, k:k + 1, :], (per, d)) for b in range(nb)], axis=0)


def _mod_spec(t, boff, d):
    if t >= ROW_TILE:
        assert t % ROW_TILE == 0
        per = t // ROW_TILE
        return pl.BlockSpec((1, 6, d), lambda i, *_: (boff + i // per, 0, 0))
    nb = ROW_TILE // t
    assert ROW_TILE % t == 0 and boff % nb == 0
    return pl.BlockSpec((nb, 6, d), lambda i, *_: (boff // nb + i, 0, 0))


def _ada_kernel(c_ref, w_ref, b_ref, o_ref):
    c = c_ref[...]
    s = c * jax.nn.sigmoid(c)
    o_ref[0] = jnp.dot(s, w_ref[0], precision=HIGHEST, preferred_element_type=F32) + b_ref[0]


def _ada(c_all, w_ada, b_ada):
    depth, d, n6 = w_ada.shape
    rows = c_all.shape[0]
    return pl.pallas_call(
        _ada_kernel,
        grid=(depth, n6 // ADA_COLS),
        in_specs=[
            pl.BlockSpec((rows, d), lambda l, j: (0, 0)),
            pl.BlockSpec((1, d, ADA_COLS), lambda l, j: (l, 0, j)),
            pl.BlockSpec((1, 1, ADA_COLS), lambda l, j: (l, 0, j)),
        ],
        out_specs=pl.BlockSpec((1, rows, ADA_COLS), lambda l, j: (l, 0, j)),
        out_shape=jax.ShapeDtypeStruct((depth, rows, n6), F32),
        compiler_params=_params(("arbitrary", "arbitrary")),
        name="ada",
    )(c_all, w_ada, b_ada.reshape(depth, 1, n6))


def _inproj_kernel(x_ref, mod_ref, g_ref, w_ref, o_ref):
    rows = x_ref.shape[0]
    hn = _rms(x_ref[...], g_ref[...]) * (1.0 + _mod_row(mod_ref, 1, rows)) + _mod_row(mod_ref, 0, rows)
    o_ref[...] = jnp.dot(hn.astype(BF16), w_ref[...], preferred_element_type=F32)


def _inproj(x, mod, boff, g1, w_in, t):
    n, d = x.shape
    cols = w_in.shape[1]
    tm = ROW_TILE
    assert n % tm == 0
    return pl.pallas_call(
        _inproj_kernel,
        grid=(n // tm,),
        in_specs=[
            pl.BlockSpec((tm, d), lambda i: (i, 0)),
            _mod_spec(t, boff, d),
            pl.BlockSpec((1, d), lambda i: (0, 0)),
            pl.BlockSpec((d, cols), lambda i: (0, 0)),
        ],
        out_specs=pl.BlockSpec((tm, cols), lambda i: (i, 0)),
        out_shape=jax.ShapeDtypeStruct((n, cols), F32),
        compiler_params=_params(("arbitrary",)),
        name="inproj",
    )(x, mod, g1, w_in)


def _rglru_kernel(xr_ref, yr_ref, cst_ref, h0_ref, cw_ref, cb_ref, wr_ref, br_ref, wi_ref, bi_ref,
                  lam_ref, o_ref, cnew_ref, hlast_ref, tail_s, h_s, *, tt):
    j = pl.program_id(1)
    w = xr_ref.shape[1]

    @pl.when(j == 0)
    def _():
        tail_s[...] = jnp.zeros((V7X_SUBLANES, w), F32)
        tail_s[V7X_SUBLANES - (CONV_W - 1):, :] = cst_ref[0]
        h_s[...] = h0_ref[0]

    x = xr_ref[...]
    ext = jnp.concatenate([tail_s[...], x], axis=0)
    cw = cw_ref[...]
    xc = cb_ref[...] + x * cw[CONV_W - 1:CONV_W, :]
    for d in range(1, CONV_W):
        xs = pltpu.roll(ext, d, axis=0)[V7X_SUBLANES:, :]
        xc = xc + xs * cw[CONV_W - 1 - d:CONV_W - d, :]
    tail_s[...] = x[tt - V7X_SUBLANES:, :]
    cnew_ref[0] = x[tt - (CONV_W - 1):, :]

    xb = xc.astype(BF16)
    gr = jax.nn.sigmoid(jnp.dot(xb, wr_ref[...], preferred_element_type=F32) + br_ref[...])
    gi = jax.nn.sigmoid(jnp.dot(xb, wi_ref[...], preferred_element_type=F32) + bi_ref[...])
    nl = -lam_ref[...]
    sp = jnp.maximum(nl, 0.0) + jnp.log1p(jnp.exp(-jnp.abs(nl)))
    a = jnp.exp((-RG_C) * gr * sp)
    u = jnp.sqrt(1.0 - a * a) * gi * xc

    rows = lax.broadcasted_iota(I32, (tt, w), 0)
    d = 1
    while d < tt:
        keep = rows >= d
        u = jnp.where(keep, u + a * pltpu.roll(u, d, axis=0), u)
        a = jnp.where(keep, a * pltpu.roll(a, d, axis=0), a)
        d *= 2
    h = u + a * h_s[...]
    h_s[...] = h[tt - 1:, :]
    hlast_ref[0] = h[tt - 1:, :]
    o_ref[...] = (jax.nn.gelu(yr_ref[...], approximate=True) * h).astype(BF16)


def _rglru(proj, conv_st, h0, cw, cb, wr, br, wi, bi, lam, bsz, t):
    n = proj.shape[0]
    w = cw.shape[1]
    tt = min(t, ROW_TILE)
    nt = t // tt
    assert tt >= V7X_SUBLANES and t % tt == 0
    full = lambda shape: pl.BlockSpec(shape, lambda b, j: (0,) * len(shape))
    return pl.pallas_call(
        functools.partial(_rglru_kernel, tt=tt),
        grid=(bsz, nt),
        in_specs=[
            pl.BlockSpec((tt, w), lambda b, j: (b * nt + j, 0)),
            pl.BlockSpec((tt, w), lambda b, j: (b * nt + j, 1)),
            pl.BlockSpec((1, CONV_W - 1, w), lambda b, j: (b, 0, 0)),
            pl.BlockSpec((1, 1, w), lambda b, j: (b, 0, 0)),
            full((CONV_W, w)), full((1, w)), full((w, w)), full((1, w)), full((w, w)), full((1, w)),
            full((1, w)),
        ],
        out_specs=[
            pl.BlockSpec((tt, w), lambda b, j: (b * nt + j, 0)),
            pl.BlockSpec((1, CONV_W - 1, w), lambda b, j: (b, 0, 0)),
            pl.BlockSpec((1, 1, w), lambda b, j: (b, 0, 0)),
        ],
        out_shape=[
            jax.ShapeDtypeStruct((n, w), BF16),
            jax.ShapeDtypeStruct((bsz, CONV_W - 1, w), F32),
            jax.ShapeDtypeStruct((bsz, 1, w), F32),
        ],
        scratch_shapes=[pltpu.VMEM((V7X_SUBLANES, w), F32), pltpu.VMEM((1, w), F32)],
        compiler_params=_params(("arbitrary", "arbitrary")),
        name="rglru",
    )(proj, proj, conv_st, h0.reshape(bsz, 1, w), cw, cb, wr, br, wi, bi, lam)


def _hgrn_tables(c):
    t = jnp.arange(c)[:, None]
    r = jnp.arange(c)[None, :]
    mats, masks = [], [(t == r)]
    h = 1
    while h < c:
        m_t = (t // (2 * h)) * (2 * h) + h
        upper = t >= m_t
        a = jnp.where(upper, ((r >= m_t) & (r <= t)).astype(F32),
                      -((r > t) & (r < m_t)).astype(F32))
        mats.append(a)
        same = (t // (2 * h)) == (r // (2 * h))
        masks.append(same & upper & (r < m_t))
        h *= 2
    mats.append((r <= t).astype(F32))
    return (jnp.concatenate(mats, axis=0).astype(BF16),
            jnp.stack(masks).astype(F32))


def _hgrn_kernel(q_ref, f_ref, v_ref, g_ref, lb_ref, ng_ref, s0_ref, amat_ref, mask_ref,
                 o_ref, slast_ref, s_s, acc_s, *, c, nlev):
    j = pl.program_id(1)

    @pl.when(j == 0)
    def _():
        s_s[...] = s0_ref[0]

    ff = f_ref[...]
    lb = lb_ref[...]
    log_sig = -(jnp.maximum(-ff, 0.0) + jnp.log1p(jnp.exp(-jnp.abs(ff))))
    log_f = log_sig + jnp.log1p(lb * jnp.exp(-ff))
    k = (1.0 - lb) * jax.nn.sigmoid(-ff)
    q = q_ref[...]

    hi = log_f.astype(BF16)
    r1 = log_f - hi.astype(F32)
    mid = r1.astype(BF16)
    lo = (r1 - mid.astype(F32)).astype(BF16)

    def table(l):
        a = amat_ref[l * c:(l + 1) * c, :]
        return (jnp.dot(a, hi, preferred_element_type=F32)
                + jnp.dot(a, mid, preferred_element_type=F32)
                + jnp.dot(a, lo, preferred_element_type=F32))

    def pair_scores(l, qw, kw):
        keep = mask_ref[l] > 0.0
        for hd in range(HG_HEADS):
            sl = slice(hd * HG_D, (hd + 1) * HG_D)
            p = lax.dot_general(qw[:, sl], kw[:, sl], NT_DIMS, preferred_element_type=F32)
            p = jnp.where(keep, p, 0.0)
            if l == 0:
                acc_s[hd] = p
            else:
                acc_s[hd] += p

    pair_scores(0, q.astype(BF16), k.astype(BF16))
    for l in range(1, nlev + 1):
        wgt = jnp.exp(-jnp.abs(table(l - 1)))
        pair_scores(l, (q * wgt).astype(BF16), (k * wgt).astype(BF16))

    gcum = table(nlev)
    glast = gcum[c - 1:, :]
    qg = (q * jnp.exp(gcum)).astype(BF16)
    kdec = (k * jnp.exp(glast - gcum)).astype(BF16)
    dlast = jnp.exp(glast)
    vb = v_ref[...].astype(BF16)
    for hd in range(HG_HEADS):
        sl = slice(hd * HG_D, (hd + 1) * HG_D)
        s_old = s_s[hd]
        o = (jnp.dot(acc_s[hd].astype(BF16), vb[:, sl], preferred_element_type=F32)
             + jnp.dot(qg[:, sl], s_old.astype(BF16), preferred_element_type=F32))
        kv = lax.dot_general(kdec[:, sl], vb[:, sl], TN_DIMS, preferred_element_type=F32)
        dcol = jnp.broadcast_to(dlast[:, sl], (HG_D, HG_D)).T
        s_s[hd] = dcol * s_old + kv
        gate = g_ref[:, sl]
        o_ref[:, sl] = (_rms(o, ng_ref[...]) * (gate * jax.nn.sigmoid(gate))).astype(BF16)
    slast_ref[0] = s_s[...]


def _hgrn(proj, lb, ng, s0, bsz, t):
    n = proj.shape[0]
    w = HG_HEADS * HG_D
    c = min(t, HGRN_CHUNK)
    nt = t // c
    nlev = int(math.log2(c))
    assert 2 ** nlev == c and t % c == 0
    amat, masks = _hgrn_tables(c)
    col = lambda k: pl.BlockSpec((c, w), lambda b, j: (b * nt + j, k))
    full = lambda shape: pl.BlockSpec(shape, lambda b, j: (0,) * len(shape))
    return pl.pallas_call(
        functools.partial(_hgrn_kernel, c=c, nlev=nlev),
        grid=(bsz, nt),
        in_specs=[
            col(2), col(3), col(4), col(5),
            full((1, w)), full((1, HG_D)),
            pl.BlockSpec((1, HG_HEADS, HG_D, HG_D), lambda b, j: (b, 0, 0, 0)),
            full(amat.shape), full(masks.shape),
        ],
        out_specs=[
            pl.BlockSpec((c, w), lambda b, j: (b * nt + j, 0)),
            pl.BlockSpec((1, HG_HEADS, HG_D, HG_D), lambda b, j: (b, 0, 0, 0)),
        ],
        out_shape=[
            jax.ShapeDtypeStruct((n, w), BF16),
            jax.ShapeDtypeStruct((bsz, HG_HEADS, HG_D, HG_D), F32),
        ],
        scratch_shapes=[pltpu.VMEM((HG_HEADS, HG_D, HG_D), F32),
                        pltpu.VMEM((HG_HEADS, c, c), F32)],
        compiler_params=_params(("arbitrary", "arbitrary")),
        name="hgrn",
    )(proj, proj, proj, proj, lb, ng, s0, amat, masks)


def _outproj_kernel(org_ref, ohg_ref, x_ref, mod_ref, wo_ref, g2_ref, wrt_ref, brt_ref, tri_ref,
                    cin_ref, x1_ref, xp_ref, idx_ref, rank_ref, prob_ref, cnt_ref, carry_s, *, tm):
    @pl.when(pl.program_id(0) == 0)
    def _():
        carry_s[...] = cin_ref[...]

    half = org_ref.shape[1]
    mix = (jnp.dot(org_ref[...], wo_ref[:half, :], preferred_element_type=F32)
           + jnp.dot(ohg_ref[...], wo_ref[half:, :], preferred_element_type=F32))
    x1 = x_ref[...] + _mod_row(mod_ref, 2, tm) * mix
    x1_ref[...] = x1
    hn = _rms(x1, g2_ref[...]) * (1.0 + _mod_row(mod_ref, 4, tm)) + _mod_row(mod_ref, 3, tm)

    bits = lax.bitcast_convert_type(hn.astype(BF16).astype(F32), jnp.uint32)
    dh = hn.shape[1] // 2
    xp_ref[...] = (bits[:, :dh] & jnp.uint32(0xFFFF0000)) | (bits[:, dh:] >> 16)

    ne = wrt_ref.shape[0]
    logits = lax.dot_general(wrt_ref[...], hn, NT_DIMS, precision=HIGHEST,
                             preferred_element_type=F32) + brt_ref[...]
    eio = lax.broadcasted_iota(I32, (ne, tm), 0)
    cur = logits
    vals, idxs, sels = [], [], []
    for _ in range(TOP_K):
        mx = jnp.max(cur, axis=0, keepdims=True)
        ix = jnp.min(jnp.where(cur == mx, eio, ne), axis=0, keepdims=True)
        sel = eio == ix
        vals.append(mx)
        idxs.append(ix)
        sels.append(sel)
        cur = jnp.where(sel, -jnp.inf, cur)
    ex = [jnp.exp(v - vals[0]) for v in vals]
    den = ex[0] + ex[1] + ex[2] + ex[3]
    probs = [e / den for e in ex]

    onehot = sels[0].astype(F32)
    for sel in sels[1:]:
        onehot = onehot + sel.astype(F32)
    incl = jnp.dot(onehot.astype(BF16), tri_ref[...], preferred_element_type=F32)
    before = incl - onehot + carry_s[...]
    ranks = [jnp.sum(jnp.where(sel, before, 0.0), axis=0, keepdims=True) for sel in sels]
    carry_s[...] = carry_s[...] + incl[:, tm - 1:]
    cnt_ref[...] = carry_s[...]

    idx_ref[...] = jnp.concatenate(idxs, axis=0)
    rank_ref[...] = jnp.concatenate(ranks, axis=0).astype(I32)
    sub = lax.broadcasted_iota(I32, (V7X_LANES, tm), 0)
    pt = jnp.zeros((V7X_LANES, tm), F32)
    for kk in range(TOP_K):
        pt = jnp.where(sub == kk, probs[kk], pt)
    prob_ref[...] = pt.T


def _outproj(o_rg, o_hg, x, mod, boff, w_out, g2, wrt, brt, cnt_in, t):
    n, d = x.shape
    half = o_rg.shape[1]
    ne = wrt.shape[0]
    tm = ROW_TILE
    assert n % tm == 0
    tri = (jnp.arange(tm)[:, None] <= jnp.arange(tm)[None, :]).astype(BF16)
    rowblk = lambda cols: pl.BlockSpec((tm, cols), lambda i: (i, 0))
    lanes = lambda rows: pl.BlockSpec((rows, tm), lambda i: (0, i))
    full = lambda shape: pl.BlockSpec(shape, lambda i: (0,) * len(shape))
    return pl.pallas_call(
        functools.partial(_outproj_kernel, tm=tm),
        grid=(n // tm,),
        in_specs=[
            rowblk(half), rowblk(half), rowblk(d),
            _mod_spec(t, boff, d),
            full((d, d)), full((1, d)), full((ne, d)), full((ne, 1)), full((tm, tm)), full((ne, 1)),
        ],
        out_specs=[
            rowblk(d), rowblk(d // 2), lanes(TOP_K), lanes(TOP_K), rowblk(V7X_LANES), full((ne, 1)),
        ],
        out_shape=[
            jax.ShapeDtypeStruct((n, d), F32),
            jax.ShapeDtypeStruct((n, d // 2), jnp.uint32),
            jax.ShapeDtypeStruct((TOP_K, n), I32),
            jax.ShapeDtypeStruct((TOP_K, n), I32),
            jax.ShapeDtypeStruct((n, V7X_LANES), F32),
            jax.ShapeDtypeStruct((ne, 1), F32),
        ],
        scratch_shapes=[pltpu.VMEM((ne, 1), F32)],
        compiler_params=_params(("arbitrary",)),
        name="outproj",
    )(o_rg, o_hg, x, mod, w_out, g2, wrt, brt, tri, cnt_in)


def _dispatch_kernel(off_ref, cnt_ref, idx_ref, rank_ref, xp_ref, out_ref, zero_s, sem, zsem, *, tt):
    row = lambda ref, r: ref.at[pl.ds(r, 1), :]

    def start(r, carry):
        for kk in range(TOP_K):
            dst = off_ref[idx_ref[kk, r]] + rank_ref[kk, r]
            pltpu.make_async_copy(row(xp_ref, r), row(out_ref, dst), sem).start()
        return carry

    def wait(r, carry):
        for kk in range(TOP_K):
            pltpu.make_async_copy(row(xp_ref, 0), row(out_ref, 0), sem).wait()
        return carry

    lax.fori_loop(0, tt, start, 0)
    lax.fori_loop(0, tt, wait, 0)

    @pl.when(pl.program_id(0) == pl.num_programs(0) - 1)
    def _():
        zero_s[...] = jnp.zeros(zero_s.shape, zero_s.dtype)

        def per_expert(e, carry):
            lo = off_ref[e] + cnt_ref[e]
            hi = off_ref[e + 1]

            def zstart(r, c2):
                pltpu.make_async_copy(row(zero_s, 0), row(out_ref, r), zsem).start()
                return c2

            def zwait(r, c2):
                pltpu.make_async_copy(row(zero_s, 0), row(out_ref, 0), zsem).wait()
                return c2

            lax.fori_loop(lo, hi, zstart, 0)
            lax.fori_loop(lo, hi, zwait, 0)
            return carry

        lax.fori_loop(0, cnt_ref.shape[0], per_expert, 0)


def _dispatch(off, cnt, idx_t, rank_t, xp, n_rows):
    n, dw = xp.shape
    tt = ROW_TILE
    assert n % tt == 0
    smem = lambda: pl.BlockSpec((TOP_K, tt), lambda i, *_: (0, i), memory_space=pltpu.SMEM)
    return pl.pallas_call(
        functools.partial(_dispatch_kernel, tt=tt),
        grid_spec=pltpu.PrefetchScalarGridSpec(
            num_scalar_prefetch=2,
            grid=(n // tt,),
            in_specs=[smem(), smem(), pl.BlockSpec((tt, dw), lambda i, *_: (i, 0))],
            out_specs=pl.BlockSpec(memory_space=pl.ANY),
            scratch_shapes=[pltpu.VMEM((V7X_SUBLANES, dw), jnp.uint32),
                            pltpu.SemaphoreType.DMA(()), pltpu.SemaphoreType.DMA(())],
        ),
        out_shape=jax.ShapeDtypeStruct((n_rows, dw), jnp.uint32),
        compiler_params=_params(("arbitrary",)),
        name="dispatch",
    )(off, cnt, idx_t, rank_t, xp)


def _ffn_kernel(te_ref, nu_ref, xs_ref, wg_ref, bg_ref, wu_ref, bu_ref, wd_ref, bd_ref, y_ref,
                wg_s, wu_s, wd_s):
    i = pl.program_id(0)
    valid = i < nu_ref[0]
    changed = (i == 0) | (te_ref[i] != te_ref[jnp.maximum(i - 1, 0)])

    @pl.when(valid & changed)
    def _():
        wg_s[...] = wg_ref[...].astype(BF16)
        wu_s[...] = wu_ref[...].astype(BF16)
        wd_s[...] = wd_ref[...].astype(BF16)

    @pl.when(valid)
    def _():
        words = xs_ref[...]
        dh = words.shape[1]
        xa = lax.bitcast_convert_type(words & jnp.uint32(0xFFFF0000), F32).astype(BF16)
        xb = lax.bitcast_convert_type(words << 16, F32).astype(BF16)
        gate = (jnp.dot(xa, wg_s[:dh, :], preferred_element_type=F32)
                + jnp.dot(xb, wg_s[dh:, :], preferred_element_type=F32) + bg_ref[...])
        up = (jnp.dot(xa, wu_s[:dh, :], preferred_element_type=F32)
              + jnp.dot(xb, wu_s[dh:, :], preferred_element_type=F32) + bu_ref[...])
        gate = jnp.minimum(gate, SWIGLU_LIMIT)
        up = jnp.clip(up, -SWIGLU_LIMIT, SWIGLU_LIMIT)
        glu = gate * jax.nn.sigmoid(SWIGLU_ALPHA * gate)
        act = ((up + 1.0) * glu).astype(BF16)
        y_ref[...] = jnp.dot(act, wd_s[...], preferred_element_type=F32) + bd_ref[...]

    @pl.when(jnp.logical_not(valid))
    def _():
        y_ref[...] = jnp.zeros(y_ref.shape, y_ref.dtype)


def _ffn(layer, tile_expert, n_used, xs, w_gate, b_gate, w_up, b_up, w_down, b_down):
    n_rows, dw = xs.shape
    d, dff = w_gate.shape[2], w_gate.shape[3]
    tm = FFN_TILE
    wspec = lambda r, c: pl.BlockSpec((None, None, r, c), lambda i, te, nu: (layer, te[i], 0, 0))
    return pl.pallas_call(
        _ffn_kernel,
        grid_spec=pltpu.PrefetchScalarGridSpec(
            num_scalar_prefetch=2,
            grid=(n_rows // tm,),
            in_specs=[
                pl.BlockSpec((tm, dw), lambda i, te, nu: (i, 0)),
                wspec(d, dff), wspec(1, dff), wspec(d, dff), wspec(1, dff), wspec(dff, d), wspec(1, d),
            ],
            out_specs=pl.BlockSpec((tm, d), lambda i, te, nu: (i, 0)),
            scratch_shapes=[pltpu.VMEM((d, dff), BF16), pltpu.VMEM((d, dff), BF16),
                            pltpu.VMEM((dff, d), BF16)],
        ),
        out_shape=jax.ShapeDtypeStruct((n_rows, d), F32),
        compiler_params=_params(("arbitrary",)),
        name="ffn",
    )(tile_expert, n_used, xs, w_gate, b_gate, w_up, b_up, w_down, b_down)


def _combine_kernel(off_ref, idx_ref, rank_ref, y_ref, prob_ref, x1_ref, mod_ref, gf_ref, o_ref,
                    buf_s, sem, *, tt, final):
    row = lambda ref, r: ref.at[pl.ds(r, 1), :]

    def start(r, carry):
        for kk in range(TOP_K):
            src = off_ref[idx_ref[kk, r]] + rank_ref[kk, r]
            pltpu.make_async_copy(row(y_ref, src), row(buf_s.at[kk], r), sem).start()
        return carry

    def wait(r, carry):
        for kk in range(TOP_K):
            pltpu.make_async_copy(row(y_ref, 0), row(buf_s.at[kk], 0), sem).wait()
        return carry

    lax.fori_loop(0, tt, start, 0)
    lax.fori_loop(0, tt, wait, 0)

    p = prob_ref[...]
    y = p[:, 0:1] * buf_s[0]
    for kk in range(1, TOP_K):
        y = y + p[:, kk:kk + 1] * buf_s[kk]
    x2 = x1_ref[...] + _mod_row(mod_ref, 5, tt) * y
    if final:
        x2 = _rms(x2, gf_ref[...])
    o_ref[...] = x2


def _combine(off, idx_t, rank_t, y_rows, prob, x1, mod, boff, g_final, roff, t, final):
    n, d = x1.shape
    tt = ROW_TILE
    r0 = roff // tt
    assert roff % tt == 0 and n % tt == 0
    smem = lambda: pl.BlockSpec((TOP_K, tt), lambda i, *_: (0, r0 + i), memory_space=pltpu.SMEM)
    return pl.pallas_call(
        functools.partial(_combine_kernel, tt=tt, final=final),
        grid_spec=pltpu.PrefetchScalarGridSpec(
            num_scalar_prefetch=1,
            grid=(n // tt,),
            in_specs=[
                smem(), smem(),
                pl.BlockSpec(memory_space=pl.ANY),
                pl.BlockSpec((tt, V7X_LANES), lambda i, *_: (r0 + i, 0)),
                pl.BlockSpec((tt, d), lambda i, *_: (i, 0)),
                _mod_spec(t, boff, d),
                pl.BlockSpec((1, d), lambda i, *_: (0, 0)),
            ],
            out_specs=pl.BlockSpec((tt, d), lambda i, *_: (i, 0)),
            scratch_shapes=[pltpu.VMEM((TOP_K, tt, d), F32), pltpu.SemaphoreType.DMA(())],
        ),
        out_shape=jax.ShapeDtypeStruct((n, d), F32),
        compiler_params=_params(("arbitrary",)),
        name="combine",
    )(off, idx_t, rank_t, y_rows, prob, x1, mod, g_final)


def _block_diag(w):
    nb, c, _ = w.shape
    eye = jnp.eye(nb, dtype=w.dtype)
    return (eye[:, None, :, None] * w[:, :, None, :]).reshape(nb * c, nb * c)


def kernel(x_prompt, x_sample, state_conv, state_rglru, state_hgrn, c_prompt, c_sample, w_ada, b_ada, g_norm1, g_norm2, w_in, conv_w, conv_b, w_rgate, b_rgate, w_igate, b_igate, lru_lambda, hgrn_lb, hgrn_norm_g, w_out, w_router, b_router, w_gate, b_gate, w_up, b_up, w_down, b_down, g_final):
    depth = w_ada.shape[0]
    bp, tp, d = x_prompt.shape
    bs, ts, _ = x_sample.shape
    w_rg = conv_w.shape[2]
    ne = w_router.shape[2]
    n_p, n_s = bp * tp, bs * ts
    n_tot = n_p + n_s
    groups = ((bp, tp, 0, 0), (bs, ts, bp, n_p))

    n_c = bp + bs
    n_c_pad = -(-n_c // V7X_SUBLANES) * V7X_SUBLANES
    c_all = jnp.concatenate([c_prompt, c_sample, jnp.zeros((n_c_pad - n_c, d), F32)], axis=0)
    mod_all = _ada(c_all, w_ada, b_ada).reshape(depth, n_c_pad, 6, d)

    p_lb = jax.nn.softmax(hgrn_lb.astype(F32), axis=0)
    lb_all = jnp.cumsum(p_lb, axis=0) - p_lb[0]

    w_in_b = w_in.astype(BF16)
    w_out_b = w_out.astype(BF16)
    wrt = jnp.swapaxes(w_router, 1, 2)

    n_tiles = -(-(TOP_K * n_tot + ne * (FFN_TILE - 1)) // FFN_TILE)
    n_rows = n_tiles * FFN_TILE

    xs = [x_prompt.reshape(n_p, d), x_sample.reshape(n_s, d)]
    conv_in = [jnp.zeros((depth, bp, CONV_W - 1, w_rg), F32), state_conv]
    h_in = [jnp.zeros((depth, bp, w_rg), F32), state_rglru]
    s_in = [jnp.zeros((depth, bp, HG_HEADS, HG_D, HG_D), F32), state_hgrn]
    conv_out, h_out, s_out = ([], []), ([], []), ([], [])

    for l in range(depth):
        mod = mod_all[l]
        wr_d = _block_diag(w_rgate[l]).astype(BF16)
        wi_d = _block_diag(w_igate[l]).astype(BF16)
        cnt = jnp.zeros((ne, 1), F32)
        x1s, xps, idxs, ranks, probs = [], [], [], [], []
        for gi, (bsz, t, boff, _) in enumerate(groups):
            proj = _inproj(xs[gi], mod, boff, g_norm1[l][None], w_in_b[l], t)
            o_rg, c_new, h_last = _rglru(
                proj, conv_in[gi][l], h_in[gi][l], conv_w[l], conv_b[l][None], wr_d, b_rgate[l][None],
                wi_d, b_igate[l][None], lru_lambda[l][None], bsz, t)
            o_hg, s_last = _hgrn(proj, lb_all[l][None], hgrn_norm_g[l][None], s_in[gi][l], bsz, t)
            x1, xp, idx_t, rank_t, prob, cnt = _outproj(
                o_rg, o_hg, xs[gi], mod, boff, w_out_b[l], g_norm2[l][None], wrt[l],
                b_router[l][:, None], cnt, t)
            conv_out[gi].append(c_new)
            h_out[gi].append(h_last[:, 0])
            s_out[gi].append(s_last)
            x1s.append(x1)
            xps.append(xp)
            idxs.append(idx_t)
            ranks.append(rank_t)
            probs.append(prob)

        counts = cnt[:, 0].astype(I32)
        padded = (counts + FFN_TILE - 1) // FFN_TILE * FFN_TILE
        pad_end = jnp.cumsum(padded)
        off = jnp.concatenate([jnp.zeros((1,), I32), pad_end]).astype(I32)
        n_used = (pad_end[-1] // FFN_TILE).astype(I32)
        tile_start = jnp.minimum(jnp.arange(n_tiles, dtype=I32), n_used - 1) * FFN_TILE
        tile_expert = jnp.minimum(jnp.searchsorted(pad_end, tile_start, side='right'), ne - 1).astype(I32)

        idx_all = jnp.concatenate(idxs, axis=1)
        rank_all = jnp.concatenate(ranks, axis=1)
        prob_all = jnp.concatenate(probs, axis=0)
        seg_off = jnp.concatenate([off, jnp.full((1,), n_rows, I32)])
        seg_cnt = jnp.concatenate([counts, jnp.zeros((1,), I32)])
        x_sorted = _dispatch(seg_off, seg_cnt, idx_all, rank_all, jnp.concatenate(xps, axis=0), n_rows)
        y_rows = _ffn(l, tile_expert, n_used.reshape(1), x_sorted, w_gate, b_gate[:, :, None, :],
                      w_up, b_up[:, :, None, :], w_down, b_down[:, :, None, :])
        final = l == depth - 1
        xs = [_combine(off, idx_all, rank_all, y_rows, prob_all, x1s[gi], mod, boff, g_final[None],
                       roff, t, final)
              for gi, (bsz, t, boff, roff) in enumerate(groups)]

    return (xs[0].reshape(bp, tp, d), xs[1].reshape(bs, ts, d),
            jnp.stack(conv_out[0]), jnp.stack(h_out[0]), jnp.stack(s_out[0]),
            jnp.stack(conv_out[1]), jnp.stack(h_out[1]), jnp.stack(s_out[1]))
```

```python
import functools
import math

import jax
import jax.numpy as jnp
from jax import lax
from jax.experimental import pallas as pl
from jax.experimental.pallas import tpu as pltpu

F32 = jnp.float32
BF16 = jnp.bfloat16
I32 = jnp.int32
HIGHEST = lax.Precision.HIGHEST

EPS = 1e-6
RG_C = 8.0
CONV_W = 4
HG_HEADS = 4
HG_D = 128
TOP_K = 4
SWIGLU_LIMIT = 7.0
SWIGLU_ALPHA = 1.702

V7X_LANES = 128
V7X_SUBLANES = 8
V7X_VMEM_LIMIT_BYTES = 56 * 1024 * 1024

ROW_TILE = 256
HGRN_CHUNK = 256
OUTPROJ_PARTS = 2
FFN_TILE = 256
ADA_COLS = 1536

NT_DIMS = (((1,), (1,)), ((), ()))
TN_DIMS = (((0,), (0,)), ((), ()))


def _params(sem, vmem=V7X_VMEM_LIMIT_BYTES):
    return pltpu.CompilerParams(dimension_semantics=sem, vmem_limit_bytes=vmem)


def _rms(x, g):
    ms = jnp.mean(x * x, axis=-1, keepdims=True)
    return x * lax.rsqrt(ms + EPS) * g


def _mod_row(mod_ref, k, rows, part=0, parts=1):
    nb, _, d = mod_ref.shape
    if nb == 1:
        return mod_ref[0, k:k + 1, :]
    nbp = nb // parts
    per = rows // nbp
    return jnp.concatenate([jnp.broadcast_to(mod_ref[b, k:k + 1, :], (per, d))
                            for b in range(part * nbp, (part + 1) * nbp)], axis=0)


def _mod_spec(t, boff, d, tile=None):
    tile = tile or ROW_TILE
    if t >= tile:
        assert t % tile == 0
        per = t // tile
        return pl.BlockSpec((1, 6, d), lambda i, *_: (boff + i // per, 0, 0))
    nb = tile // t
    assert tile % t == 0 and boff % nb == 0
    return pl.BlockSpec((nb, 6, d), lambda i, *_: (boff // nb + i, 0, 0))


def _ada_kernel(c_ref, w_ref, b_ref, o_ref):
    c = c_ref[...]
    s = c * jax.nn.sigmoid(c)
    o_ref[0] = jnp.dot(s, w_ref[0], precision=HIGHEST, preferred_element_type=F32) + b_ref[0]


def _ada(c_all, w_ada, b_ada):
    depth, d, n6 = w_ada.shape
    rows = c_all.shape[0]
    return pl.pallas_call(
        _ada_kernel,
        grid=(depth, n6 // ADA_COLS),
        in_specs=[
            pl.BlockSpec((rows, d), lambda l, j: (0, 0)),
            pl.BlockSpec((1, d, ADA_COLS), lambda l, j: (l, 0, j)),
            pl.BlockSpec((1, 1, ADA_COLS), lambda l, j: (l, 0, j)),
        ],
        out_specs=pl.BlockSpec((1, rows, ADA_COLS), lambda l, j: (l, 0, j)),
        out_shape=jax.ShapeDtypeStruct((depth, rows, n6), F32),
        compiler_params=_params(("arbitrary", "arbitrary")),
        name="ada",
    )(c_all, w_ada, b_ada.reshape(depth, 1, n6))


def _inproj_kernel(x_ref, mod_ref, g_ref, w_ref, o_ref):
    rows = x_ref.shape[0]
    hn = _rms(x_ref[...], g_ref[...]) * (1.0 + _mod_row(mod_ref, 1, rows)) + _mod_row(mod_ref, 0, rows)
    o_ref[...] = jnp.dot(hn.astype(BF16), w_ref[...], preferred_element_type=F32)


def _inproj(x, mod, boff, g1, w_in, t):
    n, d = x.shape
    cols = w_in.shape[1]
    tm = ROW_TILE
    assert n % tm == 0
    return pl.pallas_call(
        _inproj_kernel,
        grid=(n // tm,),
        in_specs=[
            pl.BlockSpec((tm, d), lambda i: (i, 0)),
            _mod_spec(t, boff, d),
            pl.BlockSpec((1, d), lambda i: (0, 0)),
            pl.BlockSpec((d, cols), lambda i: (0, 0)),
        ],
        out_specs=pl.BlockSpec((tm, cols), lambda i: (i, 0)),
        out_shape=jax.ShapeDtypeStruct((n, cols), F32),
        compiler_params=_params(("arbitrary",)),
        name="inproj",
    )(x, mod, g1, w_in)


def _rglru_kernel(xr_ref, yr_ref, cst_ref, h0_ref, cw_ref, cb_ref, wr_ref, br_ref, wi_ref, bi_ref,
                  lam_ref, o_ref, cnew_ref, hlast_ref, tail_s, h_s, *, tt):
    j = pl.program_id(1)
    w = xr_ref.shape[1]

    @pl.when(j == 0)
    def _():
        tail_s[...] = jnp.zeros((V7X_SUBLANES, w), F32)
        tail_s[V7X_SUBLANES - (CONV_W - 1):, :] = cst_ref[0]
        h_s[...] = h0_ref[0]

    x = xr_ref[...]
    ext = jnp.concatenate([tail_s[...], x], axis=0)
    cw = cw_ref[...]
    xc = cb_ref[...] + x * cw[CONV_W - 1:CONV_W, :]
    for d in range(1, CONV_W):
        xs = pltpu.roll(ext, d, axis=0)[V7X_SUBLANES:, :]
        xc = xc + xs * cw[CONV_W - 1 - d:CONV_W - d, :]
    tail_s[...] = x[tt - V7X_SUBLANES:, :]
    cnew_ref[0] = x[tt - (CONV_W - 1):, :]

    xb = xc.astype(BF16)
    gr = jax.nn.sigmoid(jnp.dot(xb, wr_ref[...], preferred_element_type=F32) + br_ref[...])
    gi = jax.nn.sigmoid(jnp.dot(xb, wi_ref[...], preferred_element_type=F32) + bi_ref[...])
    nl = -lam_ref[...]
    sp = jnp.maximum(nl, 0.0) + jnp.log1p(jnp.exp(-jnp.abs(nl)))
    a = jnp.exp((-RG_C) * gr * sp)
    u = jnp.sqrt(1.0 - a * a) * gi * xc

    rows = lax.broadcasted_iota(I32, (tt, w), 0)
    d = 1
    while d < tt:
        keep = rows >= d
        u = jnp.where(keep, u + a * pltpu.roll(u, d, axis=0), u)
        a = jnp.where(keep, a * pltpu.roll(a, d, axis=0), a)
        d *= 2
    h = u + a * h_s[...]
    h_s[...] = h[tt - 1:, :]
    hlast_ref[0] = h[tt - 1:, :]
    o_ref[...] = (jax.nn.gelu(yr_ref[...], approximate=True) * h).astype(BF16)


def _rglru(proj, conv_st, h0, cw, cb, wr, br, wi, bi, lam, bsz, t):
    n = proj.shape[0]
    w = cw.shape[1]
    tt = min(t, ROW_TILE)
    nt = t // tt
    assert tt >= V7X_SUBLANES and t % tt == 0
    full = lambda shape: pl.BlockSpec(shape, lambda b, j: (0,) * len(shape))
    return pl.pallas_call(
        functools.partial(_rglru_kernel, tt=tt),
        grid=(bsz, nt),
        in_specs=[
            pl.BlockSpec((tt, w), lambda b, j: (b * nt + j, 0)),
            pl.BlockSpec((tt, w), lambda b, j: (b * nt + j, 1)),
            pl.BlockSpec((1, CONV_W - 1, w), lambda b, j: (b, 0, 0)),
            pl.BlockSpec((1, 1, w), lambda b, j: (b, 0, 0)),
            full((CONV_W, w)), full((1, w)), full((w, w)), full((1, w)), full((w, w)), full((1, w)),
            full((1, w)),
        ],
        out_specs=[
            pl.BlockSpec((tt, w), lambda b, j: (b * nt + j, 0)),
            pl.BlockSpec((1, CONV_W - 1, w), lambda b, j: (b, 0, 0)),
            pl.BlockSpec((1, 1, w), lambda b, j: (b, 0, 0)),
        ],
        out_shape=[
            jax.ShapeDtypeStruct((n, w), BF16),
            jax.ShapeDtypeStruct((bsz, CONV_W - 1, w), F32),
            jax.ShapeDtypeStruct((bsz, 1, w), F32),
        ],
        scratch_shapes=[pltpu.VMEM((V7X_SUBLANES, w), F32), pltpu.VMEM((1, w), F32)],
        compiler_params=_params(("arbitrary", "arbitrary")),
        name="rglru",
    )(proj, proj, conv_st, h0.reshape(bsz, 1, w), cw, cb, wr, br, wi, bi, lam)


def _hgrn_tables(c):
    t = jnp.arange(c)[:, None]
    r = jnp.arange(c)[None, :]
    tri = (r <= t).astype(BF16)
    bd = min(c, V7X_LANES)
    t, r = t[:bd, :bd], r[:bd, :bd]
    masks = [(t == r)]
    h = 1
    while 2 * h <= bd:
        m_t = (t // (2 * h)) * (2 * h) + h
        same = (t // (2 * h)) == (r // (2 * h))
        masks.append(same & (t >= m_t) & (r < m_t))
        h *= 2
    return tri, jnp.stack(masks).astype(F32)


def _hgrn_kernel(q_ref, f_ref, v_ref, g_ref, lb_ref, ng_ref, s0_ref, tri_ref, mask_ref,
                 o_ref, slast_ref, s_s, acc_s, g_s, *, c, nlev):
    j = pl.program_id(1)
    w = f_ref.shape[1]

    @pl.when(j == 0)
    def _():
        s_s[...] = s0_ref[0]

    ff = f_ref[...]
    lb = lb_ref[...]
    log_sig = -(jnp.maximum(-ff, 0.0) + jnp.log1p(jnp.exp(-jnp.abs(ff))))
    log_f = log_sig + jnp.log1p(lb * jnp.exp(-ff))
    k = (1.0 - lb) * jax.nn.sigmoid(-ff)
    q = q_ref[...]

    hi = log_f.astype(BF16)
    r1 = log_f - hi.astype(F32)
    mid = r1.astype(BF16)
    lo = (r1 - mid.astype(F32)).astype(BF16)
    tri = tri_ref[...]
    gcum = (jnp.dot(tri, hi, preferred_element_type=F32) + jnp.dot(tri, mid, preferred_element_type=F32)
            + jnp.dot(tri, lo, preferred_element_type=F32))
    g_s[...] = gcum

    row = lax.broadcasted_iota(I32, (c, w), 0)
    shifted = {0: gcum}

    def midpoint(h):
        if h >= V7X_SUBLANES:
            blocks = [jnp.broadcast_to(g_s[b * 2 * h + h - 1:b * 2 * h + h, :], (2 * h, w))
                      for b in range(c // (2 * h))]
            return blocks[0] if len(blocks) == 1 else jnp.concatenate(blocks, axis=0)
        phase = row & (2 * h - 1)
        out = gcum
        for p in range(2 * h):
            off = (h - 1) - p
            if off not in shifted:
                shifted[off] = pltpu.roll(gcum, (-off) % c, axis=0)
            if off != 0:
                out = jnp.where(phase == p, shifted[off], out)
        return out

    bd = mask_ref.shape[1]

    def pair_scores(l, qw, kw):
        h = 0 if l == 0 else 2 ** (l - 1)
        for hd in range(HG_HEADS):
            sl = slice(hd * HG_D, (hd + 1) * HG_D)
            if 2 * h <= bd:
                keep = mask_ref[l] > 0.0
                for b in range(c // bd):
                    rs = slice(b * bd, (b + 1) * bd)
                    p = lax.dot_general(qw[rs, sl], kw[rs, sl], NT_DIMS, preferred_element_type=F32)
                    acc_s[hd, rs, rs] = jnp.where(keep, p, 0.0 if l == 0 else acc_s[hd, rs, rs])
            else:
                for b in range(c // (2 * h)):
                    rows = slice(b * 2 * h + h, (b + 1) * 2 * h)
                    cols = slice(b * 2 * h, b * 2 * h + h)
                    acc_s[hd, rows, cols] = lax.dot_general(qw[rows, sl], kw[cols, sl], NT_DIMS,
                                                            preferred_element_type=F32)

    if c > bd:
        acc_s[...] = jnp.zeros(acc_s.shape, F32)
    qb = q.astype(BF16)
    kb = k.astype(BF16)
    pair_scores(0, qb, kb)
    for l in range(1, nlev + 1):
        wgt = jnp.exp(-jnp.abs(gcum - midpoint(2 ** (l - 1)))).astype(BF16)
        pair_scores(l, qb * wgt, kb * wgt)

    glast = gcum[c - 1:, :]
    qg = (q * jnp.exp(gcum)).astype(BF16)
    kdec = (k * jnp.exp(glast - gcum)).astype(BF16)
    dlast = jnp.exp(glast)
    vb = v_ref[...].astype(BF16)
    for hd in range(HG_HEADS):
        sl = slice(hd * HG_D, (hd + 1) * HG_D)
        s_old = s_s[hd]
        o = (jnp.dot(acc_s[hd].astype(BF16), vb[:, sl], preferred_element_type=F32)
             + jnp.dot(qg[:, sl], s_old.astype(BF16), preferred_element_type=F32))
        kv = lax.dot_general(kdec[:, sl], vb[:, sl], TN_DIMS, preferred_element_type=F32)
        dcol = jnp.broadcast_to(dlast[:, sl], (HG_D, HG_D)).T
        s_s[hd] = dcol * s_old + kv
        gate = g_ref[:, sl]
        o_ref[:, sl] = (_rms(o, ng_ref[...]) * (gate * jax.nn.sigmoid(gate))).astype(BF16)
    slast_ref[0] = s_s[...]


def _hgrn(proj, lb, ng, s0, bsz, t):
    n = proj.shape[0]
    w = HG_HEADS * HG_D
    c = min(t, HGRN_CHUNK)
    nt = t // c
    nlev = int(math.log2(c))
    assert 2 ** nlev == c and t % c == 0
    tri, masks = _hgrn_tables(c)
    col = lambda k: pl.BlockSpec((c, w), lambda b, j: (b * nt + j, k))
    full = lambda shape: pl.BlockSpec(shape, lambda b, j: (0,) * len(shape))
    return pl.pallas_call(
        functools.partial(_hgrn_kernel, c=c, nlev=nlev),
        grid=(bsz, nt),
        in_specs=[
            col(2), col(3), col(4), col(5),
            full((1, w)), full((1, HG_D)),
            pl.BlockSpec((1, HG_HEADS, HG_D, HG_D), lambda b, j: (b, 0, 0, 0)),
            full(tri.shape), full(masks.shape),
        ],
        out_specs=[
            pl.BlockSpec((c, w), lambda b, j: (b * nt + j, 0)),
            pl.BlockSpec((1, HG_HEADS, HG_D, HG_D), lambda b, j: (b, 0, 0, 0)),
        ],
        out_shape=[
            jax.ShapeDtypeStruct((n, w), BF16),
            jax.ShapeDtypeStruct((bsz, HG_HEADS, HG_D, HG_D), F32),
        ],
        scratch_shapes=[pltpu.VMEM((HG_HEADS, HG_D, HG_D), F32),
                        pltpu.VMEM((HG_HEADS, c, c), F32),
                        pltpu.VMEM((c, w), F32)],
        compiler_params=_params(("arbitrary", "arbitrary")),
        name="hgrn",
    )(proj, proj, proj, proj, lb, ng, s0, tri, masks)


def _outproj_kernel(org_ref, ohg_ref, x_ref, mod_ref, wo_ref, g2_ref, wr_ref, brt_ref, tri_ref,
                    low_ref, cin_ref, x1_ref, hb_ref, pos_ref, prob_ref, segb_ref, segl_ref, cnt_ref,
                    carry_s, *, tm, parts):
    @pl.when(pl.program_id(0) == 0)
    def _():
        carry_s[...] = cin_ref[...]

    carry = carry_s[...]
    for part in range(parts):
        carry = _outproj_part(part, parts, tm, carry, org_ref, ohg_ref, x_ref, mod_ref, wo_ref,
                              g2_ref, wr_ref, brt_ref, tri_ref, low_ref, x1_ref, hb_ref, pos_ref,
                              prob_ref, segb_ref, segl_ref)
    carry_s[...] = carry
    cnt_ref[...] = carry


def _outproj_part(part, parts, tm, carry, org_ref, ohg_ref, x_ref, mod_ref, wo_ref, g2_ref, wr_ref,
                  brt_ref, tri_ref, low_ref, x1_ref, hb_ref, pos_ref, prob_ref, segb_ref, segl_ref):
    rs = slice(part * tm, (part + 1) * tm)
    mrow = lambda k: _mod_row(mod_ref, k, tm, part, parts)
    half = org_ref.shape[1]
    mix = (jnp.dot(org_ref[rs, :], wo_ref[:half, :], preferred_element_type=F32)
           + jnp.dot(ohg_ref[rs, :], wo_ref[half:, :], preferred_element_type=F32))
    x1 = x_ref[rs, :] + mrow(2) * mix
    x1_ref[rs, :] = x1
    hn = _rms(x1, g2_ref[...]) * (1.0 + mrow(4)) + mrow(3)
    hn_hi = hn.astype(BF16)
    hb_ref[rs, :] = hn_hi

    ne = brt_ref.shape[0]
    hn_lo = (hn - hn_hi.astype(F32)).astype(BF16)
    both = jnp.dot(hn_hi, wr_ref[...], preferred_element_type=F32)
    logits_tm = (both[:, :V7X_LANES] + both[:, V7X_LANES:]
                 + jnp.dot(hn_lo, wr_ref[:, :V7X_LANES], preferred_element_type=F32))
    logits = logits_tm.T[:ne, :] + brt_ref[...]
    eio = lax.broadcasted_iota(I32, (ne, tm), 0)
    cur = logits
    vals, sels = [], []
    for _ in range(TOP_K):
        mx = jnp.max(cur, axis=0, keepdims=True)
        ix = jnp.min(jnp.where(cur == mx, eio, ne), axis=0, keepdims=True)
        sel = eio == ix
        vals.append(mx)
        sels.append(sel)
        cur = jnp.where(sel, -jnp.inf, cur)
    ex = [jnp.exp(v - vals[0]) for v in vals]
    den = ex[0] + ex[1] + ex[2] + ex[3]
    probs = [e / den for e in ex]

    onehot = sels[0].astype(F32)
    for sel in sels[1:]:
        onehot = onehot + sel.astype(F32)
    incl = jnp.dot(onehot.astype(BF16), tri_ref[...], preferred_element_type=F32)
    run_len = jnp.ceil(incl[:, tm - 1:] * (1.0 / V7X_SUBLANES)) * V7X_SUBLANES
    run0 = jnp.dot(low_ref[...], jnp.broadcast_to(run_len, (ne, tm)).astype(BF16),
                   preferred_element_type=F32)
    local = run0 + incl - onehot
    slots = [jnp.sum(jnp.where(sel, local, 0.0), axis=0, keepdims=True) for sel in sels]
    segb_ref[part] = carry
    segl_ref[part] = run_len

    pos_ref[:, rs] = jnp.concatenate(slots, axis=0).astype(I32)
    sub = lax.broadcasted_iota(I32, (V7X_LANES, tm), 0)
    pt = jnp.zeros((V7X_LANES, tm), F32)
    for kk in range(TOP_K):
        pt = jnp.where(sub == kk, probs[kk], pt)
        pt = jnp.where(sub == TOP_K + kk, slots[kk], pt)
    prob_ref[rs, :] = pt.T
    return carry + run_len


def _router_weights(w_router):
    d, ne = w_router.shape
    assert ne <= V7X_LANES
    hi = w_router.astype(BF16)
    lo = (w_router - hi.astype(F32)).astype(BF16)
    pad = jnp.zeros((d, V7X_LANES - ne), BF16)
    return jnp.concatenate([hi, pad, lo, pad], axis=1)


def _outproj(o_rg, o_hg, x, mod, boff, w_out, g2, wr, brt, cnt_in, t):
    n, d = x.shape
    half = o_rg.shape[1]
    ne = brt.shape[0]
    tm = ROW_TILE
    parts = OUTPROJ_PARTS
    tile = tm * parts
    assert n % tile == 0 and tm <= 256
    nt = n // tm
    tri = (jnp.arange(tm)[:, None] <= jnp.arange(tm)[None, :]).astype(BF16)
    low = (jnp.arange(ne)[:, None] > jnp.arange(ne)[None, :]).astype(BF16)
    rowblk = lambda cols: pl.BlockSpec((tile, cols), lambda i: (i, 0))
    full = lambda shape: pl.BlockSpec(shape, lambda i: (0,) * len(shape))
    pertile = lambda: pl.BlockSpec((parts, ne, 1), lambda i: (i, 0, 0))
    return pl.pallas_call(
        functools.partial(_outproj_kernel, tm=tm, parts=parts),
        grid=(n // tile,),
        in_specs=[
            rowblk(half), rowblk(half), rowblk(d),
            _mod_spec(t, boff, d, tile),
            full((d, d)), full((1, d)), full(wr.shape), full((ne, 1)), full((tm, tm)), full((ne, ne)),
            full((ne, 1)),
        ],
        out_specs=[
            rowblk(d), rowblk(d), pl.BlockSpec((TOP_K, tile), lambda i: (0, i)), rowblk(V7X_LANES),
            pertile(), pertile(), full((ne, 1)),
        ],
        out_shape=[
            jax.ShapeDtypeStruct((n, d), F32),
            jax.ShapeDtypeStruct((n, d), BF16),
            jax.ShapeDtypeStruct((TOP_K, n), I32),
            jax.ShapeDtypeStruct((n, V7X_LANES), F32),
            jax.ShapeDtypeStruct((nt, ne, 1), F32),
            jax.ShapeDtypeStruct((nt, ne, 1), F32),
            jax.ShapeDtypeStruct((ne, 1), F32),
        ],
        scratch_shapes=[pltpu.VMEM((ne, 1), F32)],
        compiler_params=_params(("arbitrary",)),
        name="outproj",
    )(o_rg, o_hg, x, mod, w_out, g2, wr, brt, tri, low, cnt_in)


def _pow2_pieces(hi, lo=1):
    return tuple(1 << b for b in range(int(math.log2(hi)), int(math.log2(lo)) - 1, -1))


def _tile_slots(ne):
    worst = TOP_K * ROW_TILE + ne * (V7X_SUBLANES - 1)
    return -(-worst // V7X_LANES) * V7X_LANES


def _copy_run(n, make_copy, wait, pieces):
    for rows in pieces:
        done = n & (-2 * rows)

        @pl.when((n & rows) != 0)
        def _():
            cp = make_copy(done, rows)
            cp.wait() if wait else cp.start()


def _tile_runs(start_ref, len_ref, tile, ne, make_copy, wait):
    pieces = _pow2_pieces(ROW_TILE, V7X_SUBLANES)

    def per_expert(e, buf_row):
        n = len_ref[tile * ne + e]
        g = pl.multiple_of(start_ref[tile * ne + e], V7X_SUBLANES)
        buf_row = pl.multiple_of(buf_row, V7X_SUBLANES)
        _copy_run(n, lambda o, rows: make_copy(pl.multiple_of(g + o, V7X_SUBLANES),
                                               pl.multiple_of(buf_row + o, V7X_SUBLANES), rows),
                  wait, pieces)
        return buf_row + n

    lax.fori_loop(0, ne, per_expert, 0)


def _pack_pairs(lo_cols, hi_cols):
    a = lax.bitcast_convert_type(lo_cols, jnp.uint32)
    b = lax.bitcast_convert_type(hi_cols, jnp.uint32)
    return (a & jnp.uint32(0xFFFF0000)) | (b >> 16)


def _unpack_pairs(words):
    a = lax.bitcast_convert_type(words & jnp.uint32(0xFFFF0000), F32).astype(BF16)
    b = lax.bitcast_convert_type(words << 16, F32).astype(BF16)
    return a, b


def _dispatch_kernel(start_ref, len_ref, zoff_ref, zcnt_ref, pos_ref, hb_ref, out_ref,
                     buf_s, zero_s, sem, zsem, *, ne):
    i = pl.program_id(0)
    last = pl.num_programs(0) - 1
    slot = i % 2
    tt, d = hb_ref.shape
    dh = d // 2

    def runs(tile, s, wait):
        _tile_runs(start_ref, len_ref, tile, ne,
                   lambda g, b, rows: pltpu.make_async_copy(
                       buf_s.at[s, pl.ds(b, rows), :], out_ref.at[pl.ds(g, rows), :], sem.at[s]),
                   wait)

    @pl.when(i >= 2)
    def _():
        runs(i - 2, slot, True)

    pos = pos_ref[...]
    slot_row = lax.broadcasted_iota(I32, (buf_s.shape[1], tt), 0)
    hit = slot_row == pos[0:1, :]
    for kk in range(1, TOP_K):
        hit = hit | (slot_row == pos[kk:kk + 1, :])
    perm = jnp.where(hit, 1.0, 0.0).astype(BF16)
    hb = hb_ref[...]
    buf_s[slot] = _pack_pairs(jnp.dot(perm, hb[:, :dh], preferred_element_type=F32),
                              jnp.dot(perm, hb[:, dh:], preferred_element_type=F32))
    runs(i, slot, False)

    @pl.when(i == last)
    def _():
        @pl.when(i >= 1)
        def _():
            runs(i - 1, 1 - slot, True)
        runs(i, slot, True)

        zero_s[...] = jnp.zeros(zero_s.shape, zero_s.dtype)
        zrows = zero_s.shape[0]
        zcopy = lambda g, rows: pltpu.make_async_copy(
            zero_s.at[pl.ds(0, rows), :],
            out_ref.at[pl.ds(pl.multiple_of(g, V7X_SUBLANES), rows), :], zsem)

        def zero_run(e, carry):
            lo = zoff_ref[e] + zcnt_ref[e]
            n = zoff_ref[e + 1] - lo
            nfull = n // zrows
            rem = n - nfull * zrows
            for wait in (False, True):
                def full(c, c2):
                    cp = zcopy(lo + c * zrows, zrows)
                    cp.wait() if wait else cp.start()
                    return c2
                lax.fori_loop(0, nfull, full, 0)
                _copy_run(rem, lambda o, rows: zcopy(lo + nfull * zrows + o, rows), wait,
                          _pow2_pieces(zrows // 2, V7X_SUBLANES))
            return carry

        lax.fori_loop(0, zcnt_ref.shape[0], zero_run, 0)


def _dispatch(seg_start, seg_len, zoff, zcnt, pos_t, hb, n_rows, ne):
    n, d = hb.shape
    tt = ROW_TILE
    assert n % tt == 0
    return pl.pallas_call(
        functools.partial(_dispatch_kernel, ne=ne),
        grid_spec=pltpu.PrefetchScalarGridSpec(
            num_scalar_prefetch=4,
            grid=(n // tt,),
            in_specs=[pl.BlockSpec((TOP_K, tt), lambda i, *_: (0, i)),
                      pl.BlockSpec((tt, d), lambda i, *_: (i, 0))],
            out_specs=pl.BlockSpec(memory_space=pl.ANY),
            scratch_shapes=[pltpu.VMEM((2, _tile_slots(ne), d // 2), jnp.uint32),
                            pltpu.VMEM((tt, d // 2), jnp.uint32),
                            pltpu.SemaphoreType.DMA((2,)), pltpu.SemaphoreType.DMA(())],
        ),
        out_shape=jax.ShapeDtypeStruct((n_rows, d // 2), jnp.uint32),
        compiler_params=_params(("arbitrary",)),
        name="dispatch",
    )(seg_start, seg_len, zoff, zcnt, pos_t, hb)


def _ffn_kernel(te_ref, nu_ref, xs_ref, wg_ref, bg_ref, wu_ref, bu_ref, wd_ref, bd_ref, y_ref,
                wg_s, wu_s, wd_s):
    i = pl.program_id(0)
    valid = i < nu_ref[0]
    changed = (i == 0) | (te_ref[i] != te_ref[jnp.maximum(i - 1, 0)])

    @pl.when(valid & changed)
    def _():
        wg_s[...] = wg_ref[...].astype(BF16)
        wu_s[...] = wu_ref[...].astype(BF16)
        wd_s[...] = wd_ref[...].astype(BF16)

    @pl.when(valid)
    def _():
        dh = xs_ref.shape[1]
        xa, xb = _unpack_pairs(xs_ref[...])
        gate = (jnp.dot(xa, wg_s[:dh, :], preferred_element_type=F32)
                + jnp.dot(xb, wg_s[dh:, :], preferred_element_type=F32) + bg_ref[...])
        up = (jnp.dot(xa, wu_s[:dh, :], preferred_element_type=F32)
              + jnp.dot(xb, wu_s[dh:, :], preferred_element_type=F32) + bu_ref[...])
        gate = jnp.minimum(gate, SWIGLU_LIMIT)
        up = jnp.clip(up, -SWIGLU_LIMIT, SWIGLU_LIMIT)
        glu = gate * jax.nn.sigmoid(SWIGLU_ALPHA * gate)
        act = ((up + 1.0) * glu).astype(BF16)
        y = (jnp.dot(act, wd_s[...], preferred_element_type=F32) + bd_ref[...]).astype(BF16).astype(F32)
        yh = y.shape[1] // 2
        y_ref[...] = _pack_pairs(y[:, :yh], y[:, yh:])

    @pl.when(jnp.logical_not(valid))
    def _():
        y_ref[...] = jnp.zeros(y_ref.shape, y_ref.dtype)


def _ffn(layer, tile_expert, n_used, xs, w_gate, b_gate, w_up, b_up, w_down, b_down):
    n_rows, dw = xs.shape
    d, dff = w_gate.shape[2], w_gate.shape[3]
    tm = FFN_TILE
    wspec = lambda r, c: pl.BlockSpec((None, None, r, c), lambda i, te, nu: (layer, te[i], 0, 0))
    return pl.pallas_call(
        _ffn_kernel,
        grid_spec=pltpu.PrefetchScalarGridSpec(
            num_scalar_prefetch=2,
            grid=(n_rows // tm,),
            in_specs=[
                pl.BlockSpec((tm, dw), lambda i, te, nu: (i, 0)),
                wspec(d, dff), wspec(1, dff), wspec(d, dff), wspec(1, dff), wspec(dff, d), wspec(1, d),
            ],
            out_specs=pl.BlockSpec((tm, d // 2), lambda i, te, nu: (i, 0)),
            scratch_shapes=[pltpu.VMEM((d, dff), BF16), pltpu.VMEM((d, dff), BF16),
                            pltpu.VMEM((dff, d), BF16)],
        ),
        out_shape=jax.ShapeDtypeStruct((n_rows, d // 2), jnp.uint32),
        compiler_params=_params(("arbitrary",)),
        name="ffn",
    )(tile_expert, n_used, xs, w_gate, b_gate, w_up, b_up, w_down, b_down)


def _combine_kernel(start_ref, len_ref, y_ref, prob_ref, x1_ref, mod_ref, gf_ref, o_ref,
                    buf_s, sem, *, tile0, ne, final):
    i = pl.program_id(0)
    slot = i % 2
    tile = tile0 + i
    tt = x1_ref.shape[0]

    def runs(tl, s, wait):
        _tile_runs(start_ref, len_ref, tl, ne,
                   lambda g, b, rows: pltpu.make_async_copy(
                       y_ref.at[pl.ds(g, rows), :], buf_s.at[s, pl.ds(b, rows), :], sem.at[s]),
                   wait)

    @pl.when(i == 0)
    def _():
        buf_s[...] = jnp.zeros(buf_s.shape, buf_s.dtype)
        runs(tile, slot, False)

    @pl.when(i + 1 < pl.num_programs(0))
    def _():
        runs(tile + 1, 1 - slot, False)

    runs(tile, slot, True)

    ya, yb = _unpack_pairs(buf_s[slot])
    pp = prob_ref[...]
    slot_col = lax.broadcasted_iota(I32, (tt, buf_s.shape[1]), 1)
    wgt = jnp.zeros((tt, buf_s.shape[1]), F32)
    for kk in range(TOP_K):
        pos = pp[:, TOP_K + kk:TOP_K + kk + 1].astype(I32)
        wgt = jnp.where(slot_col == pos, pp[:, kk:kk + 1], wgt)
    wgt = wgt.astype(BF16)
    y = jnp.concatenate([jnp.dot(wgt, ya, preferred_element_type=F32),
                         jnp.dot(wgt, yb, preferred_element_type=F32)], axis=1)
    x2 = x1_ref[...] + _mod_row(mod_ref, 5, tt) * y
    if final:
        x2 = _rms(x2, gf_ref[...])
    o_ref[...] = x2


def _combine(seg_start, seg_len, y_rows, prob, x1, mod, boff, g_final, roff, t, ne, final):
    n, d = x1.shape
    tt = ROW_TILE
    r0 = roff // tt
    assert roff % tt == 0 and n % tt == 0
    return pl.pallas_call(
        functools.partial(_combine_kernel, tile0=r0, ne=ne, final=final),
        grid_spec=pltpu.PrefetchScalarGridSpec(
            num_scalar_prefetch=2,
            grid=(n // tt,),
            in_specs=[
                pl.BlockSpec(memory_space=pl.ANY),
                pl.BlockSpec((tt, V7X_LANES), lambda i, *_: (r0 + i, 0)),
                pl.BlockSpec((tt, d), lambda i, *_: (i, 0)),
                _mod_spec(t, boff, d),
                pl.BlockSpec((1, d), lambda i, *_: (0, 0)),
            ],
            out_specs=pl.BlockSpec((tt, d), lambda i, *_: (i, 0)),
            scratch_shapes=[pltpu.VMEM((2, _tile_slots(ne), d // 2), jnp.uint32),
                            pltpu.SemaphoreType.DMA((2,))],
        ),
        out_shape=jax.ShapeDtypeStruct((n, d), F32),
        compiler_params=_params(("arbitrary",)),
        name="combine",
    )(seg_start, seg_len, y_rows, prob, x1, mod, g_final)


def _block_diag(w):
    nb, c, _ = w.shape
    eye = jnp.eye(nb, dtype=w.dtype)
    return (eye[:, None, :, None] * w[:, :, None, :]).reshape(nb * c, nb * c)


def kernel(x_prompt, x_sample, state_conv, state_rglru, state_hgrn, c_prompt, c_sample, w_ada, b_ada, g_norm1, g_norm2, w_in, conv_w, conv_b, w_rgate, b_rgate, w_igate, b_igate, lru_lambda, hgrn_lb, hgrn_norm_g, w_out, w_router, b_router, w_gate, b_gate, w_up, b_up, w_down, b_down, g_final):
    depth = w_ada.shape[0]
    bp, tp, d = x_prompt.shape
    bs, ts, _ = x_sample.shape
    w_rg = conv_w.shape[2]
    ne = w_router.shape[2]
    n_p, n_s = bp * tp, bs * ts
    n_tot = n_p + n_s
    span = max(ROW_TILE * OUTPROJ_PARTS // ts, 1)
    boff_s = -(-bp // span) * span
    groups = ((bp, tp, 0, 0), (bs, ts, boff_s, n_p))
    n_c_pad = -(-(boff_s + bs) // V7X_SUBLANES) * V7X_SUBLANES
    c_all = jnp.concatenate([c_prompt, jnp.zeros((boff_s - bp, d), F32), c_sample,
                             jnp.zeros((n_c_pad - boff_s - bs, d), F32)], axis=0)
    mod_all = _ada(c_all, w_ada, b_ada).reshape(depth, n_c_pad, 6, d)

    p_lb = jax.nn.softmax(hgrn_lb.astype(F32), axis=0)
    lb_all = jnp.cumsum(p_lb, axis=0) - p_lb[0]

    w_in_b = w_in.astype(BF16)
    w_out_b = w_out.astype(BF16)

    run_pad = (n_tot // ROW_TILE) * ne * (V7X_SUBLANES - 1)
    n_tiles = -(-(TOP_K * n_tot + run_pad + ne * (FFN_TILE - 1)) // FFN_TILE)
    n_rows = n_tiles * FFN_TILE

    xs = [x_prompt.reshape(n_p, d), x_sample.reshape(n_s, d)]
    conv_in = [jnp.zeros((depth, bp, CONV_W - 1, w_rg), F32), state_conv]
    h_in = [jnp.zeros((depth, bp, w_rg), F32), state_rglru]
    s_in = [jnp.zeros((depth, bp, HG_HEADS, HG_D, HG_D), F32), state_hgrn]
    conv_out, h_out, s_out = ([], []), ([], []), ([], [])

    for l in range(depth):
        mod = mod_all[l]
        wr_d = _block_diag(w_rgate[l]).astype(BF16)
        wi_d = _block_diag(w_igate[l]).astype(BF16)
        cnt = jnp.zeros((ne, 1), F32)
        x1s, hbs, poss, probs, segbs, segls = [], [], [], [], [], []
        for gi, (bsz, t, boff, _) in enumerate(groups):
            proj = _inproj(xs[gi], mod, boff, g_norm1[l][None], w_in_b[l], t)
            o_rg, c_new, h_last = _rglru(
                proj, conv_in[gi][l], h_in[gi][l], conv_w[l], conv_b[l][None], wr_d, b_rgate[l][None],
                wi_d, b_igate[l][None], lru_lambda[l][None], bsz, t)
            o_hg, s_last = _hgrn(proj, lb_all[l][None], hgrn_norm_g[l][None], s_in[gi][l], bsz, t)
            x1, hb, pos_t, prob, segb, segl, cnt = _outproj(
                o_rg, o_hg, xs[gi], mod, boff, w_out_b[l], g_norm2[l][None], _router_weights(w_router[l]),
                b_router[l][:, None], cnt, t)
            conv_out[gi].append(c_new)
            h_out[gi].append(h_last[:, 0])
            s_out[gi].append(s_last)
            x1s.append(x1)
            hbs.append(hb)
            poss.append(pos_t)
            probs.append(prob)
            segbs.append(segb[:, :, 0])
            segls.append(segl[:, :, 0])

        counts = cnt[:, 0].astype(I32)
        padded = (counts + FFN_TILE - 1) // FFN_TILE * FFN_TILE
        pad_end = jnp.cumsum(padded)
        off = jnp.concatenate([jnp.zeros((1,), I32), pad_end]).astype(I32)
        n_used = (pad_end[-1] // FFN_TILE).astype(I32)
        tile_start = jnp.minimum(jnp.arange(n_tiles, dtype=I32), n_used - 1) * FFN_TILE
        tile_expert = jnp.minimum(jnp.sum(tile_start[:, None] >= pad_end[None, :], axis=1), ne - 1).astype(I32)
        seg_start = (off[None, :ne] + jnp.concatenate(segbs, axis=0).astype(I32)).reshape(-1)
        seg_len = jnp.concatenate(segls, axis=0).astype(I32).reshape(-1)
        zoff = jnp.concatenate([off, jnp.full((1,), n_rows, I32)])
        zcnt = jnp.concatenate([counts, jnp.zeros((1,), I32)])

        prob_all = jnp.concatenate(probs, axis=0)
        x_sorted = _dispatch(seg_start, seg_len, zoff, zcnt, jnp.concatenate(poss, axis=1),
                             jnp.concatenate(hbs, axis=0), n_rows, ne)
        y_rows = _ffn(l, tile_expert, n_used.reshape(1), x_sorted, w_gate, b_gate[:, :, None, :],
                      w_up, b_up[:, :, None, :], w_down, b_down[:, :, None, :])
        final = l == depth - 1
        xs = [_combine(seg_start, seg_len, y_rows, prob_all, x1s[gi], mod, boff, g_final[None],
                       roff, t, ne, final)
              for gi, (bsz, t, boff, roff) in enumerate(groups)]

    return (xs[0].reshape(bp, tp, d), xs[1].reshape(bs, ts, d),
            jnp.stack(conv_out[0]), jnp.stack(h_out[0]), jnp.stack(s_out[0]),
            jnp.stack(conv_out[1]), jnp.stack(h_out[1]), jnp.stack(s_out[1]))
```

```python
import functools
import math

import jax
import jax.numpy as jnp
from jax import lax
from jax.experimental import pallas as pl
from jax.experimental.pallas import tpu as pltpu

F32 = jnp.float32
BF16 = jnp.bfloat16
I32 = jnp.int32
HIGHEST = lax.Precision.HIGHEST

EPS = 1e-6
RG_C = 8.0
CONV_W = 4
HG_HEADS = 4
HG_D = 128
TOP_K = 4
SWIGLU_LIMIT = 7.0
SWIGLU_ALPHA = 1.702

V7X_LANES = 128
V7X_SUBLANES = 8
V7X_VMEM_LIMIT_BYTES = 56 * 1024 * 1024

ROW_TILE = 256
INPROJ_TILE = 512
HGRN_CHUNK = 256
OUTPROJ_PARTS = 2
FFN_TILE = 512
ADA_COLS = 1536

NT_DIMS = (((1,), (1,)), ((), ()))
TN_DIMS = (((0,), (0,)), ((), ()))


def _params(sem, vmem=V7X_VMEM_LIMIT_BYTES):
    return pltpu.CompilerParams(dimension_semantics=sem, vmem_limit_bytes=vmem)


def _rms(x, g):
    ms = jnp.mean(x * x, axis=-1, keepdims=True)
    return x * lax.rsqrt(ms + EPS) * g


def _mod_row(mod_ref, k, rows, part=0, parts=1):
    nb, _, d = mod_ref.shape
    if nb == 1:
        return mod_ref[0, k:k + 1, :]
    nbp = nb // parts
    per = rows // nbp
    return jnp.concatenate([jnp.broadcast_to(mod_ref[b, k:k + 1, :], (per, d))
                            for b in range(part * nbp, (part + 1) * nbp)], axis=0)


def _mod_spec(t, boff, d, tile=None):
    tile = tile or ROW_TILE
    if t >= tile:
        assert t % tile == 0
        per = t // tile
        return pl.BlockSpec((1, 6, d), lambda i, *_: (boff + i // per, 0, 0))
    nb = tile // t
    assert tile % t == 0 and boff % nb == 0
    return pl.BlockSpec((nb, 6, d), lambda i, *_: (boff // nb + i, 0, 0))


def _ada_kernel(c_ref, w_ref, b_ref, o_ref):
    c = c_ref[...]
    s = c * jax.nn.sigmoid(c)
    o_ref[0] = jnp.dot(s, w_ref[0], precision=HIGHEST, preferred_element_type=F32) + b_ref[0]


def _ada(c_all, w_ada, b_ada):
    depth, d, n6 = w_ada.shape
    rows = c_all.shape[0]
    return pl.pallas_call(
        _ada_kernel,
        grid=(depth, n6 // ADA_COLS),
        in_specs=[
            pl.BlockSpec((rows, d), lambda l, j: (0, 0)),
            pl.BlockSpec((1, d, ADA_COLS), lambda l, j: (l, 0, j)),
            pl.BlockSpec((1, 1, ADA_COLS), lambda l, j: (l, 0, j)),
        ],
        out_specs=pl.BlockSpec((1, rows, ADA_COLS), lambda l, j: (l, 0, j)),
        out_shape=jax.ShapeDtypeStruct((depth, rows, n6), F32),
        compiler_params=_params(("arbitrary", "arbitrary")),
        name="ada",
    )(c_all, w_ada, b_ada.reshape(depth, 1, n6))


def _inproj_kernel(x_ref, mod_ref, g_ref, w_ref, o_ref):
    rows = x_ref.shape[0]
    hn = _rms(x_ref[...], g_ref[...]) * (1.0 + _mod_row(mod_ref, 1, rows)) + _mod_row(mod_ref, 0, rows)
    o_ref[...] = jnp.dot(hn.astype(BF16), w_ref[...], preferred_element_type=F32)


def _inproj(x, mod, boff, g1, w_in, t):
    n, d = x.shape
    cols = w_in.shape[1]
    tm = INPROJ_TILE
    assert n % tm == 0
    return pl.pallas_call(
        _inproj_kernel,
        grid=(n // tm,),
        in_specs=[
            pl.BlockSpec((tm, d), lambda i: (i, 0)),
            _mod_spec(t, boff, d, tm),
            pl.BlockSpec((1, d), lambda i: (0, 0)),
            pl.BlockSpec((d, cols), lambda i: (0, 0)),
        ],
        out_specs=pl.BlockSpec((tm, cols), lambda i: (i, 0)),
        out_shape=jax.ShapeDtypeStruct((n, cols), F32),
        compiler_params=_params(("arbitrary",)),
        name="inproj",
    )(x, mod, g1, w_in)


def _rglru_kernel(xr_ref, yr_ref, cst_ref, h0_ref, cw_ref, cb_ref, wr_ref, br_ref, wi_ref, bi_ref,
                  lam_ref, o_ref, cnew_ref, hlast_ref, tail_s, h_s, *, tt):
    j = pl.program_id(1)
    w = xr_ref.shape[1]

    @pl.when(j == 0)
    def _():
        tail_s[...] = jnp.zeros((V7X_SUBLANES, w), F32)
        tail_s[V7X_SUBLANES - (CONV_W - 1):, :] = cst_ref[0]
        h_s[...] = h0_ref[0]

    x = xr_ref[...]
    ext = jnp.concatenate([tail_s[...], x], axis=0)
    cw = cw_ref[...]
    xc = cb_ref[...] + x * cw[CONV_W - 1:CONV_W, :]
    for d in range(1, CONV_W):
        xs = pltpu.roll(ext, d, axis=0)[V7X_SUBLANES:, :]
        xc = xc + xs * cw[CONV_W - 1 - d:CONV_W - d, :]
    tail_s[...] = x[tt - V7X_SUBLANES:, :]
    cnew_ref[0] = x[tt - (CONV_W - 1):, :]

    xb = xc.astype(BF16)
    gr = jax.nn.sigmoid(jnp.dot(xb, wr_ref[...], preferred_element_type=F32) + br_ref[...])
    gi = jax.nn.sigmoid(jnp.dot(xb, wi_ref[...], preferred_element_type=F32) + bi_ref[...])
    nl = -lam_ref[...]
    sp = jnp.maximum(nl, 0.0) + jnp.log1p(jnp.exp(-jnp.abs(nl)))
    a = jnp.exp((-RG_C) * gr * sp)
    u = jnp.sqrt(1.0 - a * a) * gi * xc

    rows = lax.broadcasted_iota(I32, (tt, w), 0)
    d = 1
    while d < tt:
        keep = rows >= d
        u = jnp.where(keep, u + a * pltpu.roll(u, d, axis=0), u)
        a = jnp.where(keep, a * pltpu.roll(a, d, axis=0), a)
        d *= 2
    h = u + a * h_s[...]
    h_s[...] = h[tt - 1:, :]
    hlast_ref[0] = h[tt - 1:, :]
    o_ref[...] = (jax.nn.gelu(yr_ref[...], approximate=True) * h).astype(BF16)


def _rglru(proj, conv_st, h0, cw, cb, wr, br, wi, bi, lam, bsz, t):
    n = proj.shape[0]
    w = cw.shape[1]
    tt = min(t, ROW_TILE)
    nt = t // tt
    assert tt >= V7X_SUBLANES and t % tt == 0
    full = lambda shape: pl.BlockSpec(shape, lambda b, j: (0,) * len(shape))
    return pl.pallas_call(
        functools.partial(_rglru_kernel, tt=tt),
        grid=(bsz, nt),
        in_specs=[
            pl.BlockSpec((tt, w), lambda b, j: (b * nt + j, 0)),
            pl.BlockSpec((tt, w), lambda b, j: (b * nt + j, 1)),
            pl.BlockSpec((1, CONV_W - 1, w), lambda b, j: (b, 0, 0)),
            pl.BlockSpec((1, 1, w), lambda b, j: (b, 0, 0)),
            full((CONV_W, w)), full((1, w)), full((w, w)), full((1, w)), full((w, w)), full((1, w)),
            full((1, w)),
        ],
        out_specs=[
            pl.BlockSpec((tt, w), lambda b, j: (b * nt + j, 0)),
            pl.BlockSpec((1, CONV_W - 1, w), lambda b, j: (b, 0, 0)),
            pl.BlockSpec((1, 1, w), lambda b, j: (b, 0, 0)),
        ],
        out_shape=[
            jax.ShapeDtypeStruct((n, w), BF16),
            jax.ShapeDtypeStruct((bsz, CONV_W - 1, w), F32),
            jax.ShapeDtypeStruct((bsz, 1, w), F32),
        ],
        scratch_shapes=[pltpu.VMEM((V7X_SUBLANES, w), F32), pltpu.VMEM((1, w), F32)],
        compiler_params=_params(("arbitrary", "arbitrary")),
        name="rglru",
    )(proj, proj, conv_st, h0.reshape(bsz, 1, w), cw, cb, wr, br, wi, bi, lam)


def _hgrn_tables(c):
    t = jnp.arange(c)[:, None]
    r = jnp.arange(c)[None, :]
    tri = (r <= t).astype(BF16)
    bd = min(c, V7X_LANES)
    t, r = t[:bd, :bd], r[:bd, :bd]
    masks = [(t == r)]
    h = 1
    while 2 * h <= bd:
        m_t = (t // (2 * h)) * (2 * h) + h
        same = (t // (2 * h)) == (r // (2 * h))
        masks.append(same & (t >= m_t) & (r < m_t))
        h *= 2
    return tri, jnp.stack(masks).astype(F32)


def _hgrn_kernel(q_ref, f_ref, v_ref, g_ref, lb_ref, ng_ref, s0_ref, tri_ref, mask_ref,
                 o_ref, slast_ref, s_s, acc_s, g_s, *, c, nlev):
    j = pl.program_id(1)
    w = f_ref.shape[1]

    @pl.when(j == 0)
    def _():
        s_s[...] = s0_ref[0]

    ff = f_ref[...]
    lb = lb_ref[...]
    log_sig = -(jnp.maximum(-ff, 0.0) + jnp.log1p(jnp.exp(-jnp.abs(ff))))
    log_f = log_sig + jnp.log1p(lb * jnp.exp(-ff))
    k = (1.0 - lb) * jax.nn.sigmoid(-ff)
    q = q_ref[...]

    hi = log_f.astype(BF16)
    r1 = log_f - hi.astype(F32)
    mid = r1.astype(BF16)
    lo = (r1 - mid.astype(F32)).astype(BF16)
    tri = tri_ref[...]
    gcum = (jnp.dot(tri, hi, preferred_element_type=F32) + jnp.dot(tri, mid, preferred_element_type=F32)
            + jnp.dot(tri, lo, preferred_element_type=F32))
    g_s[...] = gcum

    row = lax.broadcasted_iota(I32, (c, w), 0)
    shifted = {0: gcum}

    def midpoint(h):
        if h >= V7X_SUBLANES:
            blocks = [jnp.broadcast_to(g_s[b * 2 * h + h - 1:b * 2 * h + h, :], (2 * h, w))
                      for b in range(c // (2 * h))]
            return blocks[0] if len(blocks) == 1 else jnp.concatenate(blocks, axis=0)
        phase = row & (2 * h - 1)
        out = gcum
        for p in range(2 * h):
            off = (h - 1) - p
            if off not in shifted:
                shifted[off] = pltpu.roll(gcum, (-off) % c, axis=0)
            if off != 0:
                out = jnp.where(phase == p, shifted[off], out)
        return out

    bd = mask_ref.shape[1]

    def pair_scores(l, qw, kw):
        h = 0 if l == 0 else 2 ** (l - 1)
        for hd in range(HG_HEADS):
            sl = slice(hd * HG_D, (hd + 1) * HG_D)
            if 2 * h <= bd:
                keep = mask_ref[l] > 0.0
                for b in range(c // bd):
                    rs = slice(b * bd, (b + 1) * bd)
                    p = lax.dot_general(qw[rs, sl], kw[rs, sl], NT_DIMS, preferred_element_type=F32)
                    acc_s[hd, rs, rs] = jnp.where(keep, p, 0.0 if l == 0 else acc_s[hd, rs, rs])
            else:
                for b in range(c // (2 * h)):
                    rows = slice(b * 2 * h + h, (b + 1) * 2 * h)
                    cols = slice(b * 2 * h, b * 2 * h + h)
                    acc_s[hd, rows, cols] = lax.dot_general(qw[rows, sl], kw[cols, sl], NT_DIMS,
                                                            preferred_element_type=F32)

    if c > bd:
        acc_s[...] = jnp.zeros(acc_s.shape, F32)
    qb = q.astype(BF16)
    kb = k.astype(BF16)
    pair_scores(0, qb, kb)
    for l in range(1, nlev + 1):
        wgt = jnp.exp(-jnp.abs(gcum - midpoint(2 ** (l - 1)))).astype(BF16)
        pair_scores(l, qb * wgt, kb * wgt)

    glast = gcum[c - 1:, :]
    qg = (q * jnp.exp(gcum)).astype(BF16)
    kdec = (k * jnp.exp(glast - gcum)).astype(BF16)
    dlast = jnp.exp(glast)
    vb = v_ref[...].astype(BF16)
    for hd in range(HG_HEADS):
        sl = slice(hd * HG_D, (hd + 1) * HG_D)
        s_old = s_s[hd]
        o = (jnp.dot(acc_s[hd].astype(BF16), vb[:, sl], preferred_element_type=F32)
             + jnp.dot(qg[:, sl], s_old.astype(BF16), preferred_element_type=F32))
        kv = lax.dot_general(kdec[:, sl], vb[:, sl], TN_DIMS, preferred_element_type=F32)
        dcol = jnp.broadcast_to(dlast[:, sl], (HG_D, HG_D)).T
        s_s[hd] = dcol * s_old + kv
        gate = g_ref[:, sl]
        o_ref[:, sl] = (_rms(o, ng_ref[...]) * (gate * jax.nn.sigmoid(gate))).astype(BF16)
    slast_ref[0] = s_s[...]


def _hgrn(proj, lb, ng, s0, bsz, t):
    n = proj.shape[0]
    w = HG_HEADS * HG_D
    c = min(t, HGRN_CHUNK)
    nt = t // c
    nlev = int(math.log2(c))
    assert 2 ** nlev == c and t % c == 0
    tri, masks = _hgrn_tables(c)
    col = lambda k: pl.BlockSpec((c, w), lambda b, j: (b * nt + j, k))
    full = lambda shape: pl.BlockSpec(shape, lambda b, j: (0,) * len(shape))
    return pl.pallas_call(
        functools.partial(_hgrn_kernel, c=c, nlev=nlev),
        grid=(bsz, nt),
        in_specs=[
            col(2), col(3), col(4), col(5),
            full((1, w)), full((1, HG_D)),
            pl.BlockSpec((1, HG_HEADS, HG_D, HG_D), lambda b, j: (b, 0, 0, 0)),
            full(tri.shape), full(masks.shape),
        ],
        out_specs=[
            pl.BlockSpec((c, w), lambda b, j: (b * nt + j, 0)),
            pl.BlockSpec((1, HG_HEADS, HG_D, HG_D), lambda b, j: (b, 0, 0, 0)),
        ],
        out_shape=[
            jax.ShapeDtypeStruct((n, w), BF16),
            jax.ShapeDtypeStruct((bsz, HG_HEADS, HG_D, HG_D), F32),
        ],
        scratch_shapes=[pltpu.VMEM((HG_HEADS, HG_D, HG_D), F32),
                        pltpu.VMEM((HG_HEADS, c, c), F32),
                        pltpu.VMEM((c, w), F32)],
        compiler_params=_params(("arbitrary", "arbitrary")),
        name="hgrn",
    )(proj, proj, proj, proj, lb, ng, s0, tri, masks)


def _outproj_kernel(org_ref, ohg_ref, x_ref, mod_ref, wo_ref, g2_ref, wr_ref, brt_ref, tri_ref,
                    low_ref, cin_ref, x1_ref, hb_ref, pos_ref, prob_ref, segb_ref, segl_ref, cnt_ref,
                    carry_s, *, tm, parts):
    @pl.when(pl.program_id(0) == 0)
    def _():
        carry_s[...] = cin_ref[...]

    carry = carry_s[...]
    for part in range(parts):
        carry = _outproj_part(part, parts, tm, carry, org_ref, ohg_ref, x_ref, mod_ref, wo_ref,
                              g2_ref, wr_ref, brt_ref, tri_ref, low_ref, x1_ref, hb_ref, pos_ref,
                              prob_ref, segb_ref, segl_ref)
    carry_s[...] = carry
    cnt_ref[...] = carry


def _outproj_part(part, parts, tm, carry, org_ref, ohg_ref, x_ref, mod_ref, wo_ref, g2_ref, wr_ref,
                  brt_ref, tri_ref, low_ref, x1_ref, hb_ref, pos_ref, prob_ref, segb_ref, segl_ref):
    rs = slice(part * tm, (part + 1) * tm)
    mrow = lambda k: _mod_row(mod_ref, k, tm, part, parts)
    half = org_ref.shape[1]
    mix = (jnp.dot(org_ref[rs, :], wo_ref[:half, :], preferred_element_type=F32)
           + jnp.dot(ohg_ref[rs, :], wo_ref[half:, :], preferred_element_type=F32))
    x1 = x_ref[rs, :] + mrow(2) * mix
    x1_ref[rs, :] = x1
    hn = _rms(x1, g2_ref[...]) * (1.0 + mrow(4)) + mrow(3)
    hn_hi = hn.astype(BF16)
    hb_ref[rs, :] = hn_hi

    ne = brt_ref.shape[0]
    hn_lo = (hn - hn_hi.astype(F32)).astype(BF16)
    both = jnp.dot(hn_hi, wr_ref[...], preferred_element_type=F32)
    logits_tm = (both[:, :V7X_LANES] + both[:, V7X_LANES:]
                 + jnp.dot(hn_lo, wr_ref[:, :V7X_LANES], preferred_element_type=F32))
    logits = logits_tm.T[:ne, :] + brt_ref[...]
    eio = lax.broadcasted_iota(I32, (ne, tm), 0)
    cur = logits
    vals, sels = [], []
    for _ in range(TOP_K):
        mx = jnp.max(cur, axis=0, keepdims=True)
        ix = jnp.min(jnp.where(cur == mx, eio, ne), axis=0, keepdims=True)
        sel = eio == ix
        vals.append(mx)
        sels.append(sel)
        cur = jnp.where(sel, -jnp.inf, cur)
    ex = [jnp.exp(v - vals[0]) for v in vals]
    den = ex[0] + ex[1] + ex[2] + ex[3]
    probs = [e / den for e in ex]

    onehot = sels[0].astype(F32)
    for sel in sels[1:]:
        onehot = onehot + sel.astype(F32)
    incl = jnp.dot(onehot.astype(BF16), tri_ref[...], preferred_element_type=F32)
    run_len = jnp.ceil(incl[:, tm - 1:] * (1.0 / V7X_SUBLANES)) * V7X_SUBLANES
    run0 = jnp.dot(low_ref[...], jnp.broadcast_to(run_len, (ne, tm)).astype(BF16),
                   preferred_element_type=F32)
    local = run0 + incl - onehot
    slots = [jnp.sum(jnp.where(sel, local, 0.0), axis=0, keepdims=True) for sel in sels]
    segb_ref[part] = carry
    segl_ref[part] = run_len

    pos_ref[:, rs] = jnp.concatenate(slots, axis=0).astype(I32)
    sub = lax.broadcasted_iota(I32, (V7X_LANES, tm), 0)
    pt = jnp.zeros((V7X_LANES, tm), F32)
    for kk in range(TOP_K):
        pt = jnp.where(sub == kk, probs[kk], pt)
        pt = jnp.where(sub == TOP_K + kk, slots[kk], pt)
    prob_ref[rs, :] = pt.T
    return carry + run_len


def _router_weights(w_router):
    d, ne = w_router.shape
    assert ne <= V7X_LANES
    hi = w_router.astype(BF16)
    lo = (w_router - hi.astype(F32)).astype(BF16)
    pad = jnp.zeros((d, V7X_LANES - ne), BF16)
    return jnp.concatenate([hi, pad, lo, pad], axis=1)


def _outproj(o_rg, o_hg, x, mod, boff, w_out, g2, wr, brt, cnt_in, t):
    n, d = x.shape
    half = o_rg.shape[1]
    ne = brt.shape[0]
    tm = ROW_TILE
    parts = OUTPROJ_PARTS
    tile = tm * parts
    assert n % tile == 0 and tm <= 256
    nt = n // tm
    tri = (jnp.arange(tm)[:, None] <= jnp.arange(tm)[None, :]).astype(BF16)
    low = (jnp.arange(ne)[:, None] > jnp.arange(ne)[None, :]).astype(BF16)
    rowblk = lambda cols: pl.BlockSpec((tile, cols), lambda i: (i, 0))
    full = lambda shape: pl.BlockSpec(shape, lambda i: (0,) * len(shape))
    pertile = lambda: pl.BlockSpec((parts, ne, 1), lambda i: (i, 0, 0))
    return pl.pallas_call(
        functools.partial(_outproj_kernel, tm=tm, parts=parts),
        grid=(n // tile,),
        in_specs=[
            rowblk(half), rowblk(half), rowblk(d),
            _mod_spec(t, boff, d, tile),
            full((d, d)), full((1, d)), full(wr.shape), full((ne, 1)), full((tm, tm)), full((ne, ne)),
            full((ne, 1)),
        ],
        out_specs=[
            rowblk(d), rowblk(d), pl.BlockSpec((TOP_K, tile), lambda i: (0, i)), rowblk(V7X_LANES),
            pertile(), pertile(), full((ne, 1)),
        ],
        out_shape=[
            jax.ShapeDtypeStruct((n, d), F32),
            jax.ShapeDtypeStruct((n, d), BF16),
            jax.ShapeDtypeStruct((TOP_K, n), I32),
            jax.ShapeDtypeStruct((n, V7X_LANES), F32),
            jax.ShapeDtypeStruct((nt, ne, 1), F32),
            jax.ShapeDtypeStruct((nt, ne, 1), F32),
            jax.ShapeDtypeStruct((ne, 1), F32),
        ],
        scratch_shapes=[pltpu.VMEM((ne, 1), F32)],
        compiler_params=_params(("arbitrary",)),
        name="outproj",
    )(o_rg, o_hg, x, mod, w_out, g2, wr, brt, tri, low, cnt_in)


def _pow2_pieces(hi, lo=1):
    return tuple(1 << b for b in range(int(math.log2(hi)), int(math.log2(lo)) - 1, -1))


def _tile_slots(ne):
    worst = TOP_K * ROW_TILE + ne * (V7X_SUBLANES - 1)
    return -(-worst // V7X_LANES) * V7X_LANES


def _copy_run(n, make_copy, wait, pieces):
    for rows in pieces:
        done = n & (-2 * rows)

        @pl.when((n & rows) != 0)
        def _():
            cp = make_copy(done, rows)
            cp.wait() if wait else cp.start()


RUN_PIECES = _pow2_pieces(ROW_TILE, V7X_SUBLANES)
BUF_ROW_BITS = 11
LIST_HEAD = V7X_SUBLANES


def _run_lists(seg_start, seg_len, ne):
    assert _tile_slots(ne) <= 1 << BUF_ROW_BITS
    tiles = seg_len.shape[0]
    buf0 = jnp.cumsum(seg_len, axis=1) - seg_len
    j = jnp.arange(ne, dtype=I32)
    counts, entries = [], []
    for rows in RUN_PIECES:
        has = (seg_len & rows) != 0
        done = seg_len & (-2 * rows)
        packed = ((seg_start + done) << BUF_ROW_BITS) | (buf0 + done)
        rank = jnp.cumsum(has.astype(I32), axis=1) - 1
        place = has[:, :, None] & (rank[:, :, None] == j[None, None, :])
        entries.append(jnp.sum(jnp.where(place, packed[:, :, None], 0), axis=1))
        counts.append(jnp.sum(has.astype(I32), axis=1))
    head = jnp.stack(counts + [jnp.zeros((tiles,), I32)] * (LIST_HEAD - len(RUN_PIECES)), axis=1)
    flat = jnp.concatenate([head] + entries, axis=1)
    width = -(-flat.shape[1] // V7X_LANES) * V7X_LANES
    flat = jnp.pad(flat, ((0, 0), (0, width - flat.shape[1])))
    return flat.reshape(tiles, 1, width)


def _list_spec(ne, index_map):
    width = -(-(LIST_HEAD + len(RUN_PIECES) * ne) // V7X_LANES) * V7X_LANES
    return pl.BlockSpec((1, 1, width), index_map, memory_space=pltpu.SMEM)


def _tile_runs(list_ref, ne, make_copy, wait):
    for pi, rows in enumerate(RUN_PIECES):
        def piece(jj, carry, pi=pi, rows=rows):
            v = list_ref[0, 0, LIST_HEAD + pi * ne + jj]
            cp = make_copy(pl.multiple_of(v >> BUF_ROW_BITS, V7X_SUBLANES),
                           pl.multiple_of(v & ((1 << BUF_ROW_BITS) - 1), V7X_SUBLANES), rows)
            cp.wait() if wait else cp.start()
            return carry

        lax.fori_loop(0, list_ref[0, 0, pi], piece, 0)


def _pack_pairs(lo_cols, hi_cols):
    a = lax.bitcast_convert_type(lo_cols, jnp.uint32)
    b = lax.bitcast_convert_type(hi_cols, jnp.uint32)
    return (a & jnp.uint32(0xFFFF0000)) | (b >> 16)


def _unpack_pairs(words):
    a = lax.bitcast_convert_type(words & jnp.uint32(0xFFFF0000), F32).astype(BF16)
    b = lax.bitcast_convert_type(words << 16, F32).astype(BF16)
    return a, b


def _dispatch_kernel(zoff_ref, zcnt_ref, lst_ref, lst1_ref, lst2_ref, pos_ref, hba_ref, hbb_ref,
                     out_ref, buf_s, zero_s, sem, zsem, *, ne, tiles_a):
    i = pl.program_id(0)
    last = pl.num_programs(0) - 1
    slot = i % 2
    tt, d = hba_ref.shape
    dh = d // 2

    def runs(list_ref, s, wait):
        _tile_runs(list_ref, ne,
                   lambda g, b, rows: pltpu.make_async_copy(
                       buf_s.at[s, pl.ds(b, rows), :], out_ref.at[pl.ds(g, rows), :], sem.at[s]),
                   wait)

    @pl.when(i >= 2)
    def _():
        runs(lst2_ref, slot, True)

    pos = pos_ref[...]
    slot_row = lax.broadcasted_iota(I32, (buf_s.shape[1], tt), 0)
    hit = slot_row == pos[0:1, :]
    for kk in range(1, TOP_K):
        hit = hit | (slot_row == pos[kk:kk + 1, :])
    perm = jnp.where(hit, 1.0, 0.0).astype(BF16)
    hb = jnp.where(i < tiles_a, hba_ref[...], hbb_ref[...])
    buf_s[slot] = _pack_pairs(jnp.dot(perm, hb[:, :dh], preferred_element_type=F32),
                              jnp.dot(perm, hb[:, dh:], preferred_element_type=F32))
    runs(lst_ref, slot, False)

    @pl.when(i == last)
    def _():
        @pl.when(i >= 1)
        def _():
            runs(lst1_ref, 1 - slot, True)
        runs(lst_ref, slot, True)

        zero_s[...] = jnp.zeros(zero_s.shape, zero_s.dtype)
        zrows = zero_s.shape[0]
        zcopy = lambda g, rows: pltpu.make_async_copy(
            zero_s.at[pl.ds(0, rows), :],
            out_ref.at[pl.ds(pl.multiple_of(g, V7X_SUBLANES), rows), :], zsem)

        def zero_run(e, carry):
            lo = zoff_ref[e] + zcnt_ref[e]
            n = zoff_ref[e + 1] - lo
            nfull = n // zrows
            rem = n - nfull * zrows
            for wait in (False, True):
                def full(c, c2):
                    cp = zcopy(lo + c * zrows, zrows)
                    cp.wait() if wait else cp.start()
                    return c2
                lax.fori_loop(0, nfull, full, 0)
                _copy_run(rem, lambda o, rows: zcopy(lo + nfull * zrows + o, rows), wait,
                          _pow2_pieces(zrows // 2, V7X_SUBLANES))
            return carry

        lax.fori_loop(0, zcnt_ref.shape[0], zero_run, 0)


def _dispatch(lists, zoff, zcnt, pos_t, hb_a, hb_b, n_rows, ne):
    d = hb_a.shape[1]
    tt = ROW_TILE
    ta, tb = hb_a.shape[0] // tt, hb_b.shape[0] // tt
    assert hb_a.shape[0] % tt == 0 and hb_b.shape[0] % tt == 0
    back = lambda k: _list_spec(ne, lambda i, *_: (jnp.maximum(i - k, 0), 0, 0))
    return pl.pallas_call(
        functools.partial(_dispatch_kernel, ne=ne, tiles_a=ta),
        grid_spec=pltpu.PrefetchScalarGridSpec(
            num_scalar_prefetch=2,
            grid=(ta + tb,),
            in_specs=[back(0), back(1), back(2),
                      pl.BlockSpec((TOP_K, tt), lambda i, *_: (0, i)),
                      pl.BlockSpec((tt, d), lambda i, *_: (jnp.minimum(i, ta - 1), 0)),
                      pl.BlockSpec((tt, d), lambda i, *_: (jnp.maximum(i - ta, 0), 0))],
            out_specs=pl.BlockSpec(memory_space=pl.ANY),
            scratch_shapes=[pltpu.VMEM((2, _tile_slots(ne), d // 2), jnp.uint32),
                            pltpu.VMEM((tt, d // 2), jnp.uint32),
                            pltpu.SemaphoreType.DMA((2,)), pltpu.SemaphoreType.DMA(())],
        ),
        out_shape=jax.ShapeDtypeStruct((n_rows, d // 2), jnp.uint32),
        compiler_params=_params(("arbitrary",)),
        name="dispatch",
    )(zoff, zcnt, lists, lists, lists, pos_t, hb_a, hb_b)


def _ffn_kernel(te_ref, nu_ref, xs_ref, wg_ref, bg_ref, wu_ref, bu_ref, wd_ref, bd_ref, y_ref,
                wg_s, wu_s, wd_s):
    i = pl.program_id(0)
    valid = i < nu_ref[0]
    changed = (i == 0) | (te_ref[i] != te_ref[jnp.maximum(i - 1, 0)])

    @pl.when(valid & changed)
    def _():
        wg_s[...] = wg_ref[...].astype(BF16)
        wu_s[...] = wu_ref[...].astype(BF16)
        wd_s[...] = wd_ref[...].astype(BF16)

    @pl.when(valid)
    def _():
        dh = xs_ref.shape[1]
        xa, xb = _unpack_pairs(xs_ref[...])
        gate = (jnp.dot(xa, wg_s[:dh, :], preferred_element_type=F32)
                + jnp.dot(xb, wg_s[dh:, :], preferred_element_type=F32) + bg_ref[...])
        up = (jnp.dot(xa, wu_s[:dh, :], preferred_element_type=F32)
              + jnp.dot(xb, wu_s[dh:, :], preferred_element_type=F32) + bu_ref[...])
        gate = jnp.minimum(gate, SWIGLU_LIMIT)
        up = jnp.clip(up, -SWIGLU_LIMIT, SWIGLU_LIMIT)
        glu = gate * jax.nn.sigmoid(SWIGLU_ALPHA * gate)
        act = ((up + 1.0) * glu).astype(BF16)
        y = (jnp.dot(act, wd_s[...], preferred_element_type=F32) + bd_ref[...]).astype(BF16).astype(F32)
        yh = y.shape[1] // 2
        y_ref[...] = _pack_pairs(y[:, :yh], y[:, yh:])

    @pl.when(jnp.logical_not(valid))
    def _():
        y_ref[...] = jnp.zeros(y_ref.shape, y_ref.dtype)


def _ffn(layer, tile_expert, n_used, xs, w_gate, b_gate, w_up, b_up, w_down, b_down):
    n_rows, dw = xs.shape
    d, dff = w_gate.shape[2], w_gate.shape[3]
    tm = FFN_TILE
    wspec = lambda r, c: pl.BlockSpec((None, None, r, c), lambda i, te, nu: (layer, te[i], 0, 0))
    return pl.pallas_call(
        _ffn_kernel,
        grid_spec=pltpu.PrefetchScalarGridSpec(
            num_scalar_prefetch=2,
            grid=(n_rows // tm,),
            in_specs=[
                pl.BlockSpec((tm, dw), lambda i, te, nu: (i, 0)),
                wspec(d, dff), wspec(1, dff), wspec(d, dff), wspec(1, dff), wspec(dff, d), wspec(1, d),
            ],
            out_specs=pl.BlockSpec((tm, d // 2), lambda i, te, nu: (i, 0)),
            scratch_shapes=[pltpu.VMEM((d, dff), BF16), pltpu.VMEM((d, dff), BF16),
                            pltpu.VMEM((dff, d), BF16)],
        ),
        out_shape=jax.ShapeDtypeStruct((n_rows, d // 2), jnp.uint32),
        compiler_params=_params(("arbitrary",)),
        name="ffn",
    )(tile_expert, n_used, xs, w_gate, b_gate, w_up, b_up, w_down, b_down)


def _combine_kernel(lst_ref, nxt_ref, y_ref, prob_ref, x1_ref, mod_ref, gf_ref, o_ref,
                    buf_s, sem, *, ne, final):
    i = pl.program_id(0)
    slot = i % 2
    tt = x1_ref.shape[0]

    def runs(list_ref, s, wait):
        _tile_runs(list_ref, ne,
                   lambda g, b, rows: pltpu.make_async_copy(
                       y_ref.at[pl.ds(g, rows), :], buf_s.at[s, pl.ds(b, rows), :], sem.at[s]),
                   wait)

    @pl.when(i == 0)
    def _():
        buf_s[...] = jnp.zeros(buf_s.shape, buf_s.dtype)
        runs(lst_ref, slot, False)

    @pl.when(i + 1 < pl.num_programs(0))
    def _():
        runs(nxt_ref, 1 - slot, False)

    runs(lst_ref, slot, True)

    ya, yb = _unpack_pairs(buf_s[slot])
    pp = prob_ref[...]
    slot_col = lax.broadcasted_iota(I32, (tt, buf_s.shape[1]), 1)
    wgt = jnp.zeros((tt, buf_s.shape[1]), F32)
    for kk in range(TOP_K):
        pos = pp[:, TOP_K + kk:TOP_K + kk + 1].astype(I32)
        wgt = jnp.where(slot_col == pos, pp[:, kk:kk + 1], wgt)
    wgt = wgt.astype(BF16)
    y = jnp.concatenate([jnp.dot(wgt, ya, preferred_element_type=F32),
                         jnp.dot(wgt, yb, preferred_element_type=F32)], axis=1)
    x2 = x1_ref[...] + _mod_row(mod_ref, 5, tt) * y
    if final:
        x2 = _rms(x2, gf_ref[...])
    o_ref[...] = x2


def _combine(lists, y_rows, prob, x1, mod, boff, g_final, roff, t, ne, final):
    n, d = x1.shape
    tt = ROW_TILE
    r0 = roff // tt
    assert roff % tt == 0 and n % tt == 0
    last = r0 + n // tt - 1
    return pl.pallas_call(
        functools.partial(_combine_kernel, ne=ne, final=final),
        grid=(n // tt,),
        in_specs=[
            _list_spec(ne, lambda i: (r0 + i, 0, 0)),
            _list_spec(ne, lambda i: (jnp.minimum(r0 + i + 1, last), 0, 0)),
            pl.BlockSpec(memory_space=pl.ANY),
            pl.BlockSpec((tt, V7X_LANES), lambda i: (i, 0)),
            pl.BlockSpec((tt, d), lambda i: (i, 0)),
            _mod_spec(t, boff, d),
            pl.BlockSpec((1, d), lambda i: (0, 0)),
        ],
        out_specs=pl.BlockSpec((tt, d), lambda i: (i, 0)),
        scratch_shapes=[pltpu.VMEM((2, _tile_slots(ne), d // 2), jnp.uint32),
                        pltpu.SemaphoreType.DMA((2,))],
        out_shape=jax.ShapeDtypeStruct((n, d), F32),
        compiler_params=_params(("arbitrary",)),
        name="combine",
    )(lists, lists, y_rows, prob, x1, mod, g_final)


def _block_diag(w):
    depth, nb, c, _ = w.shape
    eye = jnp.eye(nb, dtype=w.dtype)
    return (eye[None, :, None, :, None] * w[:, :, :, None, :]).reshape(depth, nb * c, nb * c)


def kernel(x_prompt, x_sample, state_conv, state_rglru, state_hgrn, c_prompt, c_sample, w_ada, b_ada, g_norm1, g_norm2, w_in, conv_w, conv_b, w_rgate, b_rgate, w_igate, b_igate, lru_lambda, hgrn_lb, hgrn_norm_g, w_out, w_router, b_router, w_gate, b_gate, w_up, b_up, w_down, b_down, g_final):
    depth = w_ada.shape[0]
    bp, tp, d = x_prompt.shape
    bs, ts, _ = x_sample.shape
    w_rg = conv_w.shape[2]
    ne = w_router.shape[2]
    n_p, n_s = bp * tp, bs * ts
    n_tot = n_p + n_s
    span = max(max(ROW_TILE * OUTPROJ_PARTS, INPROJ_TILE) // ts, 1)
    boff_s = -(-bp // span) * span
    groups = ((bp, tp, 0, 0), (bs, ts, boff_s, n_p))
    n_c_pad = -(-(boff_s + bs) // V7X_SUBLANES) * V7X_SUBLANES
    c_all = jnp.concatenate([c_prompt, jnp.zeros((boff_s - bp, d), F32), c_sample,
                             jnp.zeros((n_c_pad - boff_s - bs, d), F32)], axis=0)
    mod_all = _ada(c_all, w_ada, b_ada).reshape(depth, n_c_pad, 6, d)

    p_lb = jax.nn.softmax(hgrn_lb.astype(F32), axis=0)
    lb_all = jnp.cumsum(p_lb, axis=0) - p_lb[0]

    w_in_b = w_in.astype(BF16)
    w_out_b = w_out.astype(BF16)
    wr_all = _block_diag(w_rgate).astype(BF16)
    wi_all = _block_diag(w_igate).astype(BF16)

    run_pad = (n_tot // ROW_TILE) * ne * (V7X_SUBLANES - 1)
    n_tiles = -(-(TOP_K * n_tot + run_pad + ne * (FFN_TILE - 1)) // FFN_TILE)
    n_rows = n_tiles * FFN_TILE

    xs = [x_prompt.reshape(n_p, d), x_sample.reshape(n_s, d)]
    conv_in = [jnp.zeros((depth, bp, CONV_W - 1, w_rg), F32), state_conv]
    h_in = [jnp.zeros((depth, bp, w_rg), F32), state_rglru]
    s_in = [jnp.zeros((depth, bp, HG_HEADS, HG_D, HG_D), F32), state_hgrn]
    conv_out, h_out, s_out = ([], []), ([], []), ([], [])

    for l in range(depth):
        mod = mod_all[l]
        wr_d, wi_d = wr_all[l], wi_all[l]
        cnt = jnp.zeros((ne, 1), F32)
        x1s, hbs, poss, probs, segbs, segls = [], [], [], [], [], []
        for gi, (bsz, t, boff, _) in enumerate(groups):
            proj = _inproj(xs[gi], mod, boff, g_norm1[l][None], w_in_b[l], t)
            o_rg, c_new, h_last = _rglru(
                proj, conv_in[gi][l], h_in[gi][l], conv_w[l], conv_b[l][None], wr_d, b_rgate[l][None],
                wi_d, b_igate[l][None], lru_lambda[l][None], bsz, t)
            o_hg, s_last = _hgrn(proj, lb_all[l][None], hgrn_norm_g[l][None], s_in[gi][l], bsz, t)
            x1, hb, pos_t, prob, segb, segl, cnt = _outproj(
                o_rg, o_hg, xs[gi], mod, boff, w_out_b[l], g_norm2[l][None], _router_weights(w_router[l]),
                b_router[l][:, None], cnt, t)
            conv_out[gi].append(c_new)
            h_out[gi].append(h_last[:, 0])
            s_out[gi].append(s_last)
            x1s.append(x1)
            hbs.append(hb)
            poss.append(pos_t)
            probs.append(prob)
            segbs.append(segb[:, :, 0])
            segls.append(segl[:, :, 0])

        counts = cnt[:, 0].astype(I32)
        padded = (counts + FFN_TILE - 1) // FFN_TILE * FFN_TILE
        pad_end = jnp.cumsum(padded)
        off = jnp.concatenate([jnp.zeros((1,), I32), pad_end]).astype(I32)
        n_used = (pad_end[-1] // FFN_TILE).astype(I32)
        tile_start = jnp.minimum(jnp.arange(n_tiles, dtype=I32), n_used - 1) * FFN_TILE
        tile_expert = jnp.minimum(jnp.sum(tile_start[:, None] >= pad_end[None, :], axis=1), ne - 1).astype(I32)
        lists = _run_lists(off[None, :ne] + jnp.concatenate(segbs, axis=0).astype(I32),
                           jnp.concatenate(segls, axis=0).astype(I32), ne)
        zoff = jnp.concatenate([off, jnp.full((1,), n_rows, I32)])
        zcnt = jnp.concatenate([counts, jnp.zeros((1,), I32)])

        x_sorted = _dispatch(lists, zoff, zcnt, jnp.concatenate(poss, axis=1), hbs[0], hbs[1],
                             n_rows, ne)
        y_rows = _ffn(l, tile_expert, n_used.reshape(1), x_sorted, w_gate, b_gate[:, :, None, :],
                      w_up, b_up[:, :, None, :], w_down, b_down[:, :, None, :])
        final = l == depth - 1
        xs = [_combine(lists, y_rows, probs[gi], x1s[gi], mod, boff, g_final[None],
                       roff, t, ne, final)
              for gi, (bsz, t, boff, roff) in enumerate(groups)]

    return (xs[0].reshape(bp, tp, d), xs[1].reshape(bs, ts, d),
            jnp.stack(conv_out[0]), jnp.stack(h_out[0]), jnp.stack(s_out[0]),
            jnp.stack(conv_out[1]), jnp.stack(h_out[1]), jnp.stack(s_out[1]))
```

```python
import functools
import math

import jax
import jax.numpy as jnp
from jax import lax
from jax.experimental import pallas as pl
from jax.experimental.pallas import tpu as pltpu

F32 = jnp.float32
BF16 = jnp.bfloat16
I32 = jnp.int32
HIGHEST = lax.Precision.HIGHEST

EPS = 1e-6
RG_C = 8.0
CONV_W = 4
HG_HEADS = 4
HG_D = 128
TOP_K = 4
SWIGLU_LIMIT = 7.0
SWIGLU_ALPHA = 1.702

V7X_LANES = 128
V7X_SUBLANES = 8
V7X_VMEM_LIMIT_BYTES = 56 * 1024 * 1024

ROW_TILE = 256
INPROJ_TILE = 512
HGRN_CHUNK = 256
OUTPROJ_PARTS = 2
FFN_TILE = 512
ADA_COLS = 1536

NT_DIMS = (((1,), (1,)), ((), ()))
TN_DIMS = (((0,), (0,)), ((), ()))


def _params(sem, vmem=V7X_VMEM_LIMIT_BYTES):
    return pltpu.CompilerParams(dimension_semantics=sem, vmem_limit_bytes=vmem)


def _rms(x, g):
    ms = jnp.mean(x * x, axis=-1, keepdims=True)
    return x * lax.rsqrt(ms + EPS) * g


def _mod_row(mod_ref, k, rows, part=0, parts=1):
    nb, _, d = mod_ref.shape
    if nb == 1:
        return mod_ref[0, k:k + 1, :]
    nbp = nb // parts
    per = rows // nbp
    return jnp.concatenate([jnp.broadcast_to(mod_ref[b, k:k + 1, :], (per, d))
                            for b in range(part * nbp, (part + 1) * nbp)], axis=0)


def _mod_spec(t, boff, d, tile=None):
    tile = tile or ROW_TILE
    if t >= tile:
        assert t % tile == 0
        per = t // tile
        return pl.BlockSpec((1, 6, d), lambda i, *_: (boff + i // per, 0, 0))
    nb = tile // t
    assert tile % t == 0 and boff % nb == 0
    return pl.BlockSpec((nb, 6, d), lambda i, *_: (boff // nb + i, 0, 0))


def _ada_kernel(c_ref, w_ref, b_ref, o_ref):
    c = c_ref[...]
    s = c * jax.nn.sigmoid(c)
    o_ref[0] = jnp.dot(s, w_ref[0], precision=HIGHEST, preferred_element_type=F32) + b_ref[0]


def _ada(c_all, w_ada, b_ada):
    depth, d, n6 = w_ada.shape
    rows = c_all.shape[0]
    return pl.pallas_call(
        _ada_kernel,
        grid=(depth, n6 // ADA_COLS),
        in_specs=[
            pl.BlockSpec((rows, d), lambda l, j: (0, 0)),
            pl.BlockSpec((1, d, ADA_COLS), lambda l, j: (l, 0, j)),
            pl.BlockSpec((1, 1, ADA_COLS), lambda l, j: (l, 0, j)),
        ],
        out_specs=pl.BlockSpec((1, rows, ADA_COLS), lambda l, j: (l, 0, j)),
        out_shape=jax.ShapeDtypeStruct((depth, rows, n6), F32),
        compiler_params=_params(("arbitrary", "arbitrary")),
        name="ada",
    )(c_all, w_ada, b_ada.reshape(depth, 1, n6))


def _inproj_kernel(x_ref, mod_ref, g_ref, w_ref, o_ref):
    rows = x_ref.shape[0]
    hn = _rms(x_ref[...], g_ref[...]) * (1.0 + _mod_row(mod_ref, 1, rows)) + _mod_row(mod_ref, 0, rows)
    o_ref[...] = jnp.dot(hn.astype(BF16), w_ref[...], preferred_element_type=F32)


def _inproj(x, mod, boff, g1, w_in, t):
    n, d = x.shape
    cols = w_in.shape[1]
    tm = INPROJ_TILE
    assert n % tm == 0
    return pl.pallas_call(
        _inproj_kernel,
        grid=(n // tm,),
        in_specs=[
            pl.BlockSpec((tm, d), lambda i: (i, 0)),
            _mod_spec(t, boff, d, tm),
            pl.BlockSpec((1, d), lambda i: (0, 0)),
            pl.BlockSpec((d, cols), lambda i: (0, 0)),
        ],
        out_specs=pl.BlockSpec((tm, cols), lambda i: (i, 0)),
        out_shape=jax.ShapeDtypeStruct((n, cols), F32),
        compiler_params=_params(("arbitrary",)),
        name="inproj",
    )(x, mod, g1, w_in)


def _rglru_kernel(xr_ref, yr_ref, cst_ref, h0_ref, cw_ref, cb_ref, wr_ref, br_ref, wi_ref, bi_ref,
                  lam_ref, o_ref, cnew_ref, hlast_ref, tail_s, h_s, *, tt):
    j = pl.program_id(1)
    w = xr_ref.shape[1]

    @pl.when(j == 0)
    def _():
        tail_s[...] = jnp.zeros((V7X_SUBLANES, w), F32)
        tail_s[V7X_SUBLANES - (CONV_W - 1):, :] = cst_ref[0]
        h_s[...] = h0_ref[0]

    x = xr_ref[...]
    ext = jnp.concatenate([tail_s[...], x], axis=0)
    cw = cw_ref[...]
    xc = cb_ref[...] + x * cw[CONV_W - 1:CONV_W, :]
    for d in range(1, CONV_W):
        xs = pltpu.roll(ext, d, axis=0)[V7X_SUBLANES:, :]
        xc = xc + xs * cw[CONV_W - 1 - d:CONV_W - d, :]
    tail_s[...] = x[tt - V7X_SUBLANES:, :]
    cnew_ref[0] = x[tt - (CONV_W - 1):, :]

    xb = xc.astype(BF16)
    gr = jax.nn.sigmoid(jnp.dot(xb, wr_ref[...], preferred_element_type=F32) + br_ref[...])
    gi = jax.nn.sigmoid(jnp.dot(xb, wi_ref[...], preferred_element_type=F32) + bi_ref[...])
    nl = -lam_ref[...]
    sp = jnp.maximum(nl, 0.0) + jnp.log1p(jnp.exp(-jnp.abs(nl)))
    a = jnp.exp((-RG_C) * gr * sp)
    u = jnp.sqrt(1.0 - a * a) * gi * xc

    rows = lax.broadcasted_iota(I32, (tt, w), 0)
    d = 1
    while d < tt:
        keep = rows >= d
        u = jnp.where(keep, u + a * pltpu.roll(u, d, axis=0), u)
        a = jnp.where(keep, a * pltpu.roll(a, d, axis=0), a)
        d *= 2
    h = u + a * h_s[...]
    h_s[...] = h[tt - 1:, :]
    hlast_ref[0] = h[tt - 1:, :]
    o_ref[...] = (jax.nn.gelu(yr_ref[...], approximate=True) * h).astype(BF16)


def _rglru(proj, conv_st, h0, cw, cb, wr, br, wi, bi, lam, bsz, t):
    n = proj.shape[0]
    w = cw.shape[1]
    tt = min(t, ROW_TILE)
    nt = t // tt
    assert tt >= V7X_SUBLANES and t % tt == 0
    full = lambda shape: pl.BlockSpec(shape, lambda b, j: (0,) * len(shape))
    return pl.pallas_call(
        functools.partial(_rglru_kernel, tt=tt),
        grid=(bsz, nt),
        in_specs=[
            pl.BlockSpec((tt, w), lambda b, j: (b * nt + j, 0)),
            pl.BlockSpec((tt, w), lambda b, j: (b * nt + j, 1)),
            pl.BlockSpec((1, CONV_W - 1, w), lambda b, j: (b, 0, 0)),
            pl.BlockSpec((1, 1, w), lambda b, j: (b, 0, 0)),
            full((CONV_W, w)), full((1, w)), full((w, w)), full((1, w)), full((w, w)), full((1, w)),
            full((1, w)),
        ],
        out_specs=[
            pl.BlockSpec((tt, w), lambda b, j: (b * nt + j, 0)),
            pl.BlockSpec((1, CONV_W - 1, w), lambda b, j: (b, 0, 0)),
            pl.BlockSpec((1, 1, w), lambda b, j: (b, 0, 0)),
        ],
        out_shape=[
            jax.ShapeDtypeStruct((n, w), BF16),
            jax.ShapeDtypeStruct((bsz, CONV_W - 1, w), F32),
            jax.ShapeDtypeStruct((bsz, 1, w), F32),
        ],
        scratch_shapes=[pltpu.VMEM((V7X_SUBLANES, w), F32), pltpu.VMEM((1, w), F32)],
        compiler_params=_params(("arbitrary", "arbitrary")),
        name="rglru",
    )(proj, proj, conv_st, h0.reshape(bsz, 1, w), cw, cb, wr, br, wi, bi, lam)


def _hgrn_tables(c):
    t = jnp.arange(c)[:, None]
    r = jnp.arange(c)[None, :]
    tri = (r <= t).astype(BF16)
    bd = min(c, V7X_LANES)
    t, r = t[:bd, :bd], r[:bd, :bd]
    masks = [(t == r)]
    h = 1
    while 2 * h <= bd:
        m_t = (t // (2 * h)) * (2 * h) + h
        same = (t // (2 * h)) == (r // (2 * h))
        masks.append(same & (t >= m_t) & (r < m_t))
        h *= 2
    return tri, jnp.stack(masks).astype(F32)


def _hgrn_kernel(q_ref, f_ref, v_ref, g_ref, lb_ref, ng_ref, s0_ref, tri_ref, mask_ref,
                 o_ref, slast_ref, s_s, acc_s, g_s, *, c, nlev):
    j = pl.program_id(1)
    w = f_ref.shape[1]

    @pl.when(j == 0)
    def _():
        s_s[...] = s0_ref[0]

    ff = f_ref[...]
    lb = lb_ref[...]
    log_sig = -(jnp.maximum(-ff, 0.0) + jnp.log1p(jnp.exp(-jnp.abs(ff))))
    log_f = log_sig + jnp.log1p(lb * jnp.exp(-ff))
    k = (1.0 - lb) * jax.nn.sigmoid(-ff)
    q = q_ref[...]

    hi = log_f.astype(BF16)
    r1 = log_f - hi.astype(F32)
    mid = r1.astype(BF16)
    lo = (r1 - mid.astype(F32)).astype(BF16)
    tri = tri_ref[...]
    gcum = (jnp.dot(tri, hi, preferred_element_type=F32) + jnp.dot(tri, mid, preferred_element_type=F32)
            + jnp.dot(tri, lo, preferred_element_type=F32))
    g_s[...] = gcum

    row = lax.broadcasted_iota(I32, (c, w), 0)
    shifted = {0: gcum}

    def midpoint(h):
        if h >= V7X_SUBLANES:
            blocks = [jnp.broadcast_to(g_s[b * 2 * h + h - 1:b * 2 * h + h, :], (2 * h, w))
                      for b in range(c // (2 * h))]
            return blocks[0] if len(blocks) == 1 else jnp.concatenate(blocks, axis=0)
        phase = row & (2 * h - 1)
        out = gcum
        for p in range(2 * h):
            off = (h - 1) - p
            if off not in shifted:
                shifted[off] = pltpu.roll(gcum, (-off) % c, axis=0)
            if off != 0:
                out = jnp.where(phase == p, shifted[off], out)
        return out

    bd = mask_ref.shape[1]

    def pair_scores(l, qw, kw):
        h = 0 if l == 0 else 2 ** (l - 1)
        for hd in range(HG_HEADS):
            sl = slice(hd * HG_D, (hd + 1) * HG_D)
            if 2 * h <= bd:
                keep = mask_ref[l] > 0.0
                for b in range(c // bd):
                    rs = slice(b * bd, (b + 1) * bd)
                    p = lax.dot_general(qw[rs, sl], kw[rs, sl], NT_DIMS, preferred_element_type=F32)
                    acc_s[hd, rs, rs] = jnp.where(keep, p, 0.0 if l == 0 else acc_s[hd, rs, rs])
            else:
                for b in range(c // (2 * h)):
                    rows = slice(b * 2 * h + h, (b + 1) * 2 * h)
                    cols = slice(b * 2 * h, b * 2 * h + h)
                    acc_s[hd, rows, cols] = lax.dot_general(qw[rows, sl], kw[cols, sl], NT_DIMS,
                                                            preferred_element_type=F32)

    if c > bd:
        acc_s[...] = jnp.zeros(acc_s.shape, F32)
    qb = q.astype(BF16)
    kb = k.astype(BF16)
    pair_scores(0, qb, kb)
    for l in range(1, nlev + 1):
        wgt = jnp.exp(-jnp.abs(gcum - midpoint(2 ** (l - 1)))).astype(BF16)
        pair_scores(l, qb * wgt, kb * wgt)

    glast = gcum[c - 1:, :]
    qg = (q * jnp.exp(gcum)).astype(BF16)
    kdec = (k * jnp.exp(glast - gcum)).astype(BF16)
    dlast = jnp.exp(glast)
    vb = v_ref[...].astype(BF16)
    for hd in range(HG_HEADS):
        sl = slice(hd * HG_D, (hd + 1) * HG_D)
        s_old = s_s[hd]
        o = (jnp.dot(acc_s[hd].astype(BF16), vb[:, sl], preferred_element_type=F32)
             + jnp.dot(qg[:, sl], s_old.astype(BF16), preferred_element_type=F32))
        kv = lax.dot_general(kdec[:, sl], vb[:, sl], TN_DIMS, preferred_element_type=F32)
        dcol = jnp.broadcast_to(dlast[:, sl], (HG_D, HG_D)).T
        s_s[hd] = dcol * s_old + kv
        gate = g_ref[:, sl]
        o_ref[:, sl] = (_rms(o, ng_ref[...]) * (gate * jax.nn.sigmoid(gate))).astype(BF16)
    slast_ref[0] = s_s[...]


def _hgrn(proj, lb, ng, s0, bsz, t):
    n = proj.shape[0]
    w = HG_HEADS * HG_D
    c = min(t, HGRN_CHUNK)
    nt = t // c
    nlev = int(math.log2(c))
    assert 2 ** nlev == c and t % c == 0
    tri, masks = _hgrn_tables(c)
    col = lambda k: pl.BlockSpec((c, w), lambda b, j: (b * nt + j, k))
    full = lambda shape: pl.BlockSpec(shape, lambda b, j: (0,) * len(shape))
    return pl.pallas_call(
        functools.partial(_hgrn_kernel, c=c, nlev=nlev),
        grid=(bsz, nt),
        in_specs=[
            col(2), col(3), col(4), col(5),
            full((1, w)), full((1, HG_D)),
            pl.BlockSpec((1, HG_HEADS, HG_D, HG_D), lambda b, j: (b, 0, 0, 0)),
            full(tri.shape), full(masks.shape),
        ],
        out_specs=[
            pl.BlockSpec((c, w), lambda b, j: (b * nt + j, 0)),
            pl.BlockSpec((1, HG_HEADS, HG_D, HG_D), lambda b, j: (b, 0, 0, 0)),
        ],
        out_shape=[
            jax.ShapeDtypeStruct((n, w), BF16),
            jax.ShapeDtypeStruct((bsz, HG_HEADS, HG_D, HG_D), F32),
        ],
        scratch_shapes=[pltpu.VMEM((HG_HEADS, HG_D, HG_D), F32),
                        pltpu.VMEM((HG_HEADS, c, c), F32),
                        pltpu.VMEM((c, w), F32)],
        compiler_params=_params(("arbitrary", "arbitrary")),
        name="hgrn",
    )(proj, proj, proj, proj, lb, ng, s0, tri, masks)


def _outproj_kernel(org_ref, ohg_ref, x_ref, mod_ref, wo_ref, g2_ref, wr_ref, brt_ref, tri_ref,
                    low_ref, cin_ref, x1_ref, hb_ref, pos_ref, prob_ref, segb_ref, segl_ref, cnt_ref,
                    carry_s, *, tm, parts):
    @pl.when(pl.program_id(0) == 0)
    def _():
        carry_s[...] = cin_ref[...]

    carry = carry_s[...]
    for part in range(parts):
        carry = _outproj_part(part, parts, tm, carry, org_ref, ohg_ref, x_ref, mod_ref, wo_ref,
                              g2_ref, wr_ref, brt_ref, tri_ref, low_ref, x1_ref, hb_ref, pos_ref,
                              prob_ref, segb_ref, segl_ref)
    carry_s[...] = carry
    cnt_ref[...] = carry


def _outproj_part(part, parts, tm, carry, org_ref, ohg_ref, x_ref, mod_ref, wo_ref, g2_ref, wr_ref,
                  brt_ref, tri_ref, low_ref, x1_ref, hb_ref, pos_ref, prob_ref, segb_ref, segl_ref):
    rs = slice(part * tm, (part + 1) * tm)
    mrow = lambda k: _mod_row(mod_ref, k, tm, part, parts)
    half = org_ref.shape[1]
    mix = (jnp.dot(org_ref[rs, :], wo_ref[:half, :], preferred_element_type=F32)
           + jnp.dot(ohg_ref[rs, :], wo_ref[half:, :], preferred_element_type=F32))
    x1 = x_ref[rs, :] + mrow(2) * mix
    x1_ref[rs, :] = x1
    hn = _rms(x1, g2_ref[...]) * (1.0 + mrow(4)) + mrow(3)
    hn_hi = hn.astype(BF16)
    hb_ref[rs, :] = hn_hi

    ne = brt_ref.shape[0]
    hn_lo = (hn - hn_hi.astype(F32)).astype(BF16)
    both = jnp.dot(hn_hi, wr_ref[...], preferred_element_type=F32)
    logits_tm = (both[:, :V7X_LANES] + both[:, V7X_LANES:]
                 + jnp.dot(hn_lo, wr_ref[:, :V7X_LANES], preferred_element_type=F32))
    logits = logits_tm.T[:ne, :] + brt_ref[...]
    eio = lax.broadcasted_iota(I32, (ne, tm), 0)
    cur = logits
    vals, sels = [], []
    for _ in range(TOP_K):
        mx = jnp.max(cur, axis=0, keepdims=True)
        ix = jnp.min(jnp.where(cur == mx, eio, ne), axis=0, keepdims=True)
        sel = eio == ix
        vals.append(mx)
        sels.append(sel)
        cur = jnp.where(sel, -jnp.inf, cur)
    ex = [jnp.exp(v - vals[0]) for v in vals]
    den = ex[0] + ex[1] + ex[2] + ex[3]
    probs = [e / den for e in ex]

    onehot = sels[0].astype(F32)
    for sel in sels[1:]:
        onehot = onehot + sel.astype(F32)
    incl = jnp.dot(onehot.astype(BF16), tri_ref[...], preferred_element_type=F32)
    run_len = jnp.ceil(incl[:, tm - 1:] * (1.0 / V7X_SUBLANES)) * V7X_SUBLANES
    run0 = jnp.dot(low_ref[...], jnp.broadcast_to(run_len, (ne, tm)).astype(BF16),
                   preferred_element_type=F32)
    local = run0 + incl - onehot
    slots = [jnp.sum(jnp.where(sel, local, 0.0), axis=0, keepdims=True) for sel in sels]
    segb_ref[part] = carry
    segl_ref[part] = run_len

    pos_ref[:, rs] = jnp.concatenate(slots, axis=0).astype(I32)
    sub = lax.broadcasted_iota(I32, (V7X_LANES, tm), 0)
    pt = jnp.zeros((V7X_LANES, tm), F32)
    for kk in range(TOP_K):
        pt = jnp.where(sub == kk, probs[kk], pt)
        pt = jnp.where(sub == TOP_K + kk, slots[kk], pt)
    prob_ref[rs, :] = pt.T
    return carry + run_len


def _router_weights(w_router):
    d, ne = w_router.shape
    assert ne <= V7X_LANES
    hi = w_router.astype(BF16)
    lo = (w_router - hi.astype(F32)).astype(BF16)
    pad = jnp.zeros((d, V7X_LANES - ne), BF16)
    return jnp.concatenate([hi, pad, lo, pad], axis=1)


def _outproj(o_rg, o_hg, x, mod, boff, w_out, g2, wr, brt, cnt_in, t):
    n, d = x.shape
    half = o_rg.shape[1]
    ne = brt.shape[0]
    tm = ROW_TILE
    parts = OUTPROJ_PARTS
    tile = tm * parts
    assert n % tile == 0 and tm <= 256
    nt = n // tm
    tri = (jnp.arange(tm)[:, None] <= jnp.arange(tm)[None, :]).astype(BF16)
    low = (jnp.arange(ne)[:, None] > jnp.arange(ne)[None, :]).astype(BF16)
    rowblk = lambda cols: pl.BlockSpec((tile, cols), lambda i: (i, 0))
    full = lambda shape: pl.BlockSpec(shape, lambda i: (0,) * len(shape))
    pertile = lambda: pl.BlockSpec((parts, ne, 1), lambda i: (i, 0, 0))
    return pl.pallas_call(
        functools.partial(_outproj_kernel, tm=tm, parts=parts),
        grid=(n // tile,),
        in_specs=[
            rowblk(half), rowblk(half), rowblk(d),
            _mod_spec(t, boff, d, tile),
            full((d, d)), full((1, d)), full(wr.shape), full((ne, 1)), full((tm, tm)), full((ne, ne)),
            full((ne, 1)),
        ],
        out_specs=[
            rowblk(d), rowblk(d), pl.BlockSpec((TOP_K, tile), lambda i: (0, i)), rowblk(V7X_LANES),
            pertile(), pertile(), full((ne, 1)),
        ],
        out_shape=[
            jax.ShapeDtypeStruct((n, d), F32),
            jax.ShapeDtypeStruct((n, d), BF16),
            jax.ShapeDtypeStruct((TOP_K, n), I32),
            jax.ShapeDtypeStruct((n, V7X_LANES), F32),
            jax.ShapeDtypeStruct((nt, ne, 1), F32),
            jax.ShapeDtypeStruct((nt, ne, 1), F32),
            jax.ShapeDtypeStruct((ne, 1), F32),
        ],
        scratch_shapes=[pltpu.VMEM((ne, 1), F32)],
        compiler_params=_params(("arbitrary",)),
        name="outproj",
    )(o_rg, o_hg, x, mod, w_out, g2, wr, brt, tri, low, cnt_in)


def _pow2_pieces(hi, lo=1):
    return tuple(1 << b for b in range(int(math.log2(hi)), int(math.log2(lo)) - 1, -1))


def _tile_slots(ne):
    worst = TOP_K * ROW_TILE + ne * (V7X_SUBLANES - 1)
    return -(-worst // V7X_LANES) * V7X_LANES


def _copy_run(n, make_copy, wait, pieces):
    for rows in pieces:
        done = n & (-2 * rows)

        @pl.when((n & rows) != 0)
        def _():
            cp = make_copy(done, rows)
            cp.wait() if wait else cp.start()


RUN_PIECES = _pow2_pieces(ROW_TILE, V7X_SUBLANES)
BUF_ROW_BITS = 11
LIST_HEAD = V7X_SUBLANES


def _run_lists(seg_start, seg_len, ne):
    assert _tile_slots(ne) <= 1 << BUF_ROW_BITS
    tiles = seg_len.shape[0]
    buf0 = jnp.cumsum(seg_len, axis=1) - seg_len
    j = jnp.arange(ne, dtype=I32)
    counts, entries = [], []
    for rows in RUN_PIECES:
        has = (seg_len & rows) != 0
        done = seg_len & (-2 * rows)
        packed = ((seg_start + done) << BUF_ROW_BITS) | (buf0 + done)
        rank = jnp.cumsum(has.astype(I32), axis=1) - 1
        place = has[:, :, None] & (rank[:, :, None] == j[None, None, :])
        entries.append(jnp.sum(jnp.where(place, packed[:, :, None], 0), axis=1))
        counts.append(jnp.sum(has.astype(I32), axis=1))
    head = jnp.stack(counts + [jnp.zeros((tiles,), I32)] * (LIST_HEAD - len(RUN_PIECES)), axis=1)
    flat = jnp.concatenate([head] + entries, axis=1)
    width = -(-flat.shape[1] // V7X_LANES) * V7X_LANES
    flat = jnp.pad(flat, ((0, 0), (0, width - flat.shape[1])))
    return flat.reshape(tiles, 1, width)


def _list_spec(ne, index_map):
    width = -(-(LIST_HEAD + len(RUN_PIECES) * ne) // V7X_LANES) * V7X_LANES
    return pl.BlockSpec((1, 1, width), index_map, memory_space=pltpu.SMEM)


def _tile_runs(list_ref, ne, make_copy, wait):
    for pi, rows in enumerate(RUN_PIECES):
        def one(jj, pi=pi, rows=rows):
            v = list_ref[0, 0, LIST_HEAD + pi * ne + jj]
            cp = make_copy(pl.multiple_of(v >> BUF_ROW_BITS, V7X_SUBLANES),
                           pl.multiple_of(v & ((1 << BUF_ROW_BITS) - 1), V7X_SUBLANES), rows)
            cp.wait() if wait else cp.start()

        def pair(jj, carry, one=one):
            one(2 * jj)
            one(2 * jj + 1)
            return carry

        cnt = list_ref[0, 0, pi]
        lax.fori_loop(0, cnt >> 1, pair, 0)
        pl.when((cnt & 1) != 0)(functools.partial(one, cnt - 1))


def _pack_pairs(lo_cols, hi_cols):
    a = lax.bitcast_convert_type(lo_cols, jnp.uint32)
    b = lax.bitcast_convert_type(hi_cols, jnp.uint32)
    return (a & jnp.uint32(0xFFFF0000)) | (b >> 16)


def _unpack_pairs(words):
    a = lax.bitcast_convert_type(words & jnp.uint32(0xFFFF0000), F32).astype(BF16)
    b = lax.bitcast_convert_type(words << 16, F32).astype(BF16)
    return a, b


def _dispatch_kernel(zoff_ref, zcnt_ref, lst_ref, lst1_ref, lst2_ref, pos_ref, hba_ref, hbb_ref,
                     out_ref, buf_s, zero_s, sem, zsem, *, ne, tiles_a):
    i = pl.program_id(0)
    last = pl.num_programs(0) - 1
    slot = i % 2
    tt, d = hba_ref.shape
    dh = d // 2

    def runs(list_ref, s, wait):
        _tile_runs(list_ref, ne,
                   lambda g, b, rows: pltpu.make_async_copy(
                       buf_s.at[s, pl.ds(b, rows), :], out_ref.at[pl.ds(g, rows), :], sem.at[s]),
                   wait)

    @pl.when(i >= 2)
    def _():
        runs(lst2_ref, slot, True)

    pos = pos_ref[...]
    slot_row = lax.broadcasted_iota(I32, (buf_s.shape[1], tt), 0)
    hit = slot_row == pos[0:1, :]
    for kk in range(1, TOP_K):
        hit = hit | (slot_row == pos[kk:kk + 1, :])
    perm = jnp.where(hit, 1.0, 0.0).astype(BF16)
    hb = jnp.where(i < tiles_a, hba_ref[...], hbb_ref[...])
    buf_s[slot] = _pack_pairs(jnp.dot(perm, hb[:, :dh], preferred_element_type=F32),
                              jnp.dot(perm, hb[:, dh:], preferred_element_type=F32))
    runs(lst_ref, slot, False)

    @pl.when(i == last)
    def _():
        @pl.when(i >= 1)
        def _():
            runs(lst1_ref, 1 - slot, True)
        runs(lst_ref, slot, True)

        zero_s[...] = jnp.zeros(zero_s.shape, zero_s.dtype)
        zrows = zero_s.shape[0]
        zcopy = lambda g, rows: pltpu.make_async_copy(
            zero_s.at[pl.ds(0, rows), :],
            out_ref.at[pl.ds(pl.multiple_of(g, V7X_SUBLANES), rows), :], zsem)

        def zero_run(e, carry):
            lo = zoff_ref[e] + zcnt_ref[e]
            n = zoff_ref[e + 1] - lo
            nfull = n // zrows
            rem = n - nfull * zrows
            for wait in (False, True):
                def full(c, c2):
                    cp = zcopy(lo + c * zrows, zrows)
                    cp.wait() if wait else cp.start()
                    return c2
                lax.fori_loop(0, nfull, full, 0)
                _copy_run(rem, lambda o, rows: zcopy(lo + nfull * zrows + o, rows), wait,
                          _pow2_pieces(zrows // 2, V7X_SUBLANES))
            return carry

        lax.fori_loop(0, zcnt_ref.shape[0], zero_run, 0)


def _dispatch(lists, zoff, zcnt, pos_t, hb_a, hb_b, n_rows, ne):
    d = hb_a.shape[1]
    tt = ROW_TILE
    ta, tb = hb_a.shape[0] // tt, hb_b.shape[0] // tt
    assert hb_a.shape[0] % tt == 0 and hb_b.shape[0] % tt == 0
    back = lambda k: _list_spec(ne, lambda i, *_: (jnp.maximum(i - k, 0), 0, 0))
    return pl.pallas_call(
        functools.partial(_dispatch_kernel, ne=ne, tiles_a=ta),
        grid_spec=pltpu.PrefetchScalarGridSpec(
            num_scalar_prefetch=2,
            grid=(ta + tb,),
            in_specs=[back(0), back(1), back(2),
                      pl.BlockSpec((TOP_K, tt), lambda i, *_: (0, i)),
                      pl.BlockSpec((tt, d), lambda i, *_: (jnp.minimum(i, ta - 1), 0)),
                      pl.BlockSpec((tt, d), lambda i, *_: (jnp.maximum(i - ta, 0), 0))],
            out_specs=pl.BlockSpec(memory_space=pl.ANY),
            scratch_shapes=[pltpu.VMEM((2, _tile_slots(ne), d // 2), jnp.uint32),
                            pltpu.VMEM((tt, d // 2), jnp.uint32),
                            pltpu.SemaphoreType.DMA((2,)), pltpu.SemaphoreType.DMA(())],
        ),
        out_shape=jax.ShapeDtypeStruct((n_rows, d // 2), jnp.uint32),
        compiler_params=_params(("arbitrary",)),
        name="dispatch",
    )(zoff, zcnt, lists, lists, lists, pos_t, hb_a, hb_b)


def _ffn_kernel(te_ref, nu_ref, xs_ref, wg_ref, bg_ref, wu_ref, bu_ref, wd_ref, bd_ref, y_ref,
                wg_s, wu_s, wd_s):
    i = pl.program_id(0)
    valid = i < nu_ref[0]
    changed = (i == 0) | (te_ref[i] != te_ref[jnp.maximum(i - 1, 0)])

    @pl.when(valid & changed)
    def _():
        wg_s[...] = wg_ref[...].astype(BF16)
        wu_s[...] = wu_ref[...].astype(BF16)
        wd_s[...] = wd_ref[...].astype(BF16)

    @pl.when(valid)
    def _():
        dh = xs_ref.shape[1]
        xa, xb = _unpack_pairs(xs_ref[...])
        gate = (jnp.dot(xa, wg_s[:dh, :], preferred_element_type=F32)
                + jnp.dot(xb, wg_s[dh:, :], preferred_element_type=F32) + bg_ref[...])
        up = (jnp.dot(xa, wu_s[:dh, :], preferred_element_type=F32)
              + jnp.dot(xb, wu_s[dh:, :], preferred_element_type=F32) + bu_ref[...])
        gate = jnp.minimum(gate, SWIGLU_LIMIT)
        up = jnp.clip(up, -SWIGLU_LIMIT, SWIGLU_LIMIT)
        glu = gate * jax.nn.sigmoid(SWIGLU_ALPHA * gate)
        act = ((up + 1.0) * glu).astype(BF16)
        y = (jnp.dot(act, wd_s[...], preferred_element_type=F32) + bd_ref[...]).astype(BF16).astype(F32)
        yh = y.shape[1] // 2
        y_ref[...] = _pack_pairs(y[:, :yh], y[:, yh:])

    @pl.when(jnp.logical_not(valid))
    def _():
        y_ref[...] = jnp.zeros(y_ref.shape, y_ref.dtype)


def _ffn(layer, tile_expert, n_used, xs, w_gate, b_gate, w_up, b_up, w_down, b_down):
    n_rows, dw = xs.shape
    d, dff = w_gate.shape[2], w_gate.shape[3]
    tm = FFN_TILE
    wspec = lambda r, c: pl.BlockSpec((None, None, r, c), lambda i, te, nu: (layer, te[i], 0, 0))
    return pl.pallas_call(
        _ffn_kernel,
        grid_spec=pltpu.PrefetchScalarGridSpec(
            num_scalar_prefetch=2,
            grid=(n_rows // tm,),
            in_specs=[
                pl.BlockSpec((tm, dw), lambda i, te, nu: (i, 0)),
                wspec(d, dff), wspec(1, dff), wspec(d, dff), wspec(1, dff), wspec(dff, d), wspec(1, d),
            ],
            out_specs=pl.BlockSpec((tm, d // 2), lambda i, te, nu: (i, 0)),
            scratch_shapes=[pltpu.VMEM((d, dff), BF16), pltpu.VMEM((d, dff), BF16),
                            pltpu.VMEM((dff, d), BF16)],
        ),
        out_shape=jax.ShapeDtypeStruct((n_rows, d // 2), jnp.uint32),
        compiler_params=_params(("arbitrary",)),
        name="ffn",
    )(tile_expert, n_used, xs, w_gate, b_gate, w_up, b_up, w_down, b_down)


def _combine_kernel(lst_ref, nxt_ref, y_ref, prob_ref, x1_ref, mod_ref, gf_ref, o_ref,
                    buf_s, sem, *, ne, final):
    i = pl.program_id(0)
    slot = i % 2
    tt = x1_ref.shape[0]

    def runs(list_ref, s, wait):
        _tile_runs(list_ref, ne,
                   lambda g, b, rows: pltpu.make_async_copy(
                       y_ref.at[pl.ds(g, rows), :], buf_s.at[s, pl.ds(b, rows), :], sem.at[s]),
                   wait)

    @pl.when(i == 0)
    def _():
        buf_s[...] = jnp.zeros(buf_s.shape, buf_s.dtype)
        runs(lst_ref, slot, False)

    @pl.when(i + 1 < pl.num_programs(0))
    def _():
        runs(nxt_ref, 1 - slot, False)

    runs(lst_ref, slot, True)

    ya, yb = _unpack_pairs(buf_s[slot])
    pp = prob_ref[...]
    slot_col = lax.broadcasted_iota(I32, (tt, buf_s.shape[1]), 1)
    wgt = jnp.zeros((tt, buf_s.shape[1]), F32)
    for kk in range(TOP_K):
        pos = pp[:, TOP_K + kk:TOP_K + kk + 1].astype(I32)
        wgt = jnp.where(slot_col == pos, pp[:, kk:kk + 1], wgt)
    wgt = wgt.astype(BF16)
    y = jnp.concatenate([jnp.dot(wgt, ya, preferred_element_type=F32),
                         jnp.dot(wgt, yb, preferred_element_type=F32)], axis=1)
    x2 = x1_ref[...] + _mod_row(mod_ref, 5, tt) * y
    if final:
        x2 = _rms(x2, gf_ref[...])
    o_ref[...] = x2


def _combine(lists, y_rows, prob, x1, mod, boff, g_final, roff, t, ne, final):
    n, d = x1.shape
    tt = ROW_TILE
    r0 = roff // tt
    assert roff % tt == 0 and n % tt == 0
    last = r0 + n // tt - 1
    return pl.pallas_call(
        functools.partial(_combine_kernel, ne=ne, final=final),
        grid=(n // tt,),
        in_specs=[
            _list_spec(ne, lambda i: (r0 + i, 0, 0)),
            _list_spec(ne, lambda i: (jnp.minimum(r0 + i + 1, last), 0, 0)),
            pl.BlockSpec(memory_space=pl.ANY),
            pl.BlockSpec((tt, V7X_LANES), lambda i: (i, 0)),
            pl.BlockSpec((tt, d), lambda i: (i, 0)),
            _mod_spec(t, boff, d),
            pl.BlockSpec((1, d), lambda i: (0, 0)),
        ],
        out_specs=pl.BlockSpec((tt, d), lambda i: (i, 0)),
        scratch_shapes=[pltpu.VMEM((2, _tile_slots(ne), d // 2), jnp.uint32),
                        pltpu.SemaphoreType.DMA((2,))],
        out_shape=jax.ShapeDtypeStruct((n, d), F32),
        compiler_params=_params(("arbitrary",)),
        name="combine",
    )(lists, lists, y_rows, prob, x1, mod, g_final)


def _block_diag(w):
    depth, nb, c, _ = w.shape
    eye = jnp.eye(nb, dtype=w.dtype)
    return (eye[None, :, None, :, None] * w[:, :, :, None, :]).reshape(depth, nb * c, nb * c)


def kernel(x_prompt, x_sample, state_conv, state_rglru, state_hgrn, c_prompt, c_sample, w_ada, b_ada, g_norm1, g_norm2, w_in, conv_w, conv_b, w_rgate, b_rgate, w_igate, b_igate, lru_lambda, hgrn_lb, hgrn_norm_g, w_out, w_router, b_router, w_gate, b_gate, w_up, b_up, w_down, b_down, g_final):
    depth = w_ada.shape[0]
    bp, tp, d = x_prompt.shape
    bs, ts, _ = x_sample.shape
    w_rg = conv_w.shape[2]
    ne = w_router.shape[2]
    n_p, n_s = bp * tp, bs * ts
    n_tot = n_p + n_s
    span = max(max(ROW_TILE * OUTPROJ_PARTS, INPROJ_TILE) // ts, 1)
    boff_s = -(-bp // span) * span
    groups = ((bp, tp, 0, 0), (bs, ts, boff_s, n_p))
    n_c_pad = -(-(boff_s + bs) // V7X_SUBLANES) * V7X_SUBLANES
    c_all = jnp.concatenate([c_prompt, jnp.zeros((boff_s - bp, d), F32), c_sample,
                             jnp.zeros((n_c_pad - boff_s - bs, d), F32)], axis=0)
    mod_all = _ada(c_all, w_ada, b_ada).reshape(depth, n_c_pad, 6, d)

    p_lb = jax.nn.softmax(hgrn_lb.astype(F32), axis=0)
    lb_all = jnp.cumsum(p_lb, axis=0) - p_lb[0]

    w_in_b = w_in.astype(BF16)
    w_out_b = w_out.astype(BF16)
    wr_all = _block_diag(w_rgate).astype(BF16)
    wi_all = _block_diag(w_igate).astype(BF16)

    run_pad = (n_tot // ROW_TILE) * ne * (V7X_SUBLANES - 1)
    n_tiles = -(-(TOP_K * n_tot + run_pad + ne * (FFN_TILE - 1)) // FFN_TILE)
    n_rows = n_tiles * FFN_TILE

    xs = [x_prompt.reshape(n_p, d), x_sample.reshape(n_s, d)]
    conv_in = [jnp.zeros((depth, bp, CONV_W - 1, w_rg), F32), state_conv]
    h_in = [jnp.zeros((depth, bp, w_rg), F32), state_rglru]
    s_in = [jnp.zeros((depth, bp, HG_HEADS, HG_D, HG_D), F32), state_hgrn]
    conv_out, h_out, s_out = ([], []), ([], []), ([], [])

    for l in range(depth):
        mod = mod_all[l]
        wr_d, wi_d = wr_all[l], wi_all[l]
        cnt = jnp.zeros((ne, 1), F32)
        x1s, hbs, poss, probs, segbs, segls = [], [], [], [], [], []
        for gi, (bsz, t, boff, _) in enumerate(groups):
            proj = _inproj(xs[gi], mod, boff, g_norm1[l][None], w_in_b[l], t)
            o_rg, c_new, h_last = _rglru(
                proj, conv_in[gi][l], h_in[gi][l], conv_w[l], conv_b[l][None], wr_d, b_rgate[l][None],
                wi_d, b_igate[l][None], lru_lambda[l][None], bsz, t)
            o_hg, s_last = _hgrn(proj, lb_all[l][None], hgrn_norm_g[l][None], s_in[gi][l], bsz, t)
            x1, hb, pos_t, prob, segb, segl, cnt = _outproj(
                o_rg, o_hg, xs[gi], mod, boff, w_out_b[l], g_norm2[l][None], _router_weights(w_router[l]),
                b_router[l][:, None], cnt, t)
            conv_out[gi].append(c_new)
            h_out[gi].append(h_last[:, 0])
            s_out[gi].append(s_last)
            x1s.append(x1)
            hbs.append(hb)
            poss.append(pos_t)
            probs.append(prob)
            segbs.append(segb[:, :, 0])
            segls.append(segl[:, :, 0])

        counts = cnt[:, 0].astype(I32)
        padded = (counts + FFN_TILE - 1) // FFN_TILE * FFN_TILE
        pad_end = jnp.cumsum(padded)
        off = jnp.concatenate([jnp.zeros((1,), I32), pad_end]).astype(I32)
        n_used = (pad_end[-1] // FFN_TILE).astype(I32)
        tile_start = jnp.minimum(jnp.arange(n_tiles, dtype=I32), n_used - 1) * FFN_TILE
        tile_expert = jnp.minimum(jnp.sum(tile_start[:, None] >= pad_end[None, :], axis=1), ne - 1).astype(I32)
        lists = _run_lists(off[None, :ne] + jnp.concatenate(segbs, axis=0).astype(I32),
                           jnp.concatenate(segls, axis=0).astype(I32), ne)
        zoff = jnp.concatenate([off, jnp.full((1,), n_rows, I32)])
        zcnt = jnp.concatenate([counts, jnp.zeros((1,), I32)])

        x_sorted = _dispatch(lists, zoff, zcnt, jnp.concatenate(poss, axis=1), hbs[0], hbs[1],
                             n_rows, ne)
        y_rows = _ffn(l, tile_expert, n_used.reshape(1), x_sorted, w_gate, b_gate[:, :, None, :],
                      w_up, b_up[:, :, None, :], w_down, b_down[:, :, None, :])
        final = l == depth - 1
        xs = [_combine(lists, y_rows, probs[gi], x1s[gi], mod, boff, g_final[None],
                       roff, t, ne, final)
              for gi, (bsz, t, boff, roff) in enumerate(groups)]

    return (xs[0].reshape(bp, tp, d), xs[1].reshape(bs, ts, d),
            jnp.stack(conv_out[0]), jnp.stack(h_out[0]), jnp.stack(s_out[0]),
            jnp.stack(conv_out[1]), jnp.stack(h_out[1]), jnp.stack(s_out[1]))
```

```python
import functools
import math

import jax
import jax.numpy as jnp
from jax import lax
from jax.experimental import pallas as pl
from jax.experimental.pallas import tpu as pltpu

F32 = jnp.float32
BF16 = jnp.bfloat16
I32 = jnp.int32
HIGHEST = lax.Precision.HIGHEST

EPS = 1e-6
RG_C = 8.0
CONV_W = 4
HG_HEADS = 4
HG_D = 128
TOP_K = 4
SWIGLU_LIMIT = 7.0
SWIGLU_ALPHA = 1.702

V7X_LANES = 128
V7X_SUBLANES = 8
V7X_VMEM_LIMIT_BYTES = 56 * 1024 * 1024

ROW_TILE = 256
INPROJ_TILE = 512
HGRN_CHUNK = 256
OUTPROJ_PARTS = 2
FFN_TILE = 512
ROW_PARTS = 4
RUN_ALIGN = V7X_SUBLANES // ROW_PARTS
ADA_COLS = 1536

NT_DIMS = (((1,), (1,)), ((), ()))
TN_DIMS = (((0,), (0,)), ((), ()))


def _params(sem, vmem=V7X_VMEM_LIMIT_BYTES):
    return pltpu.CompilerParams(dimension_semantics=sem, vmem_limit_bytes=vmem)


def _rms(x, g):
    ms = jnp.mean(x * x, axis=-1, keepdims=True)
    return x * lax.rsqrt(ms + EPS) * g


def _mod_row(mod_ref, k, rows, part=0, parts=1):
    nb, _, d = mod_ref.shape
    if nb == 1:
        return mod_ref[0, k:k + 1, :]
    nbp = nb // parts
    per = rows // nbp
    return jnp.concatenate([jnp.broadcast_to(mod_ref[b, k:k + 1, :], (per, d))
                            for b in range(part * nbp, (part + 1) * nbp)], axis=0)


def _mod_spec(t, boff, d, tile=None):
    tile = tile or ROW_TILE
    if t >= tile:
        assert t % tile == 0
        per = t // tile
        return pl.BlockSpec((1, 6, d), lambda i, *_: (boff + i // per, 0, 0))
    nb = tile // t
    assert tile % t == 0 and boff % nb == 0
    return pl.BlockSpec((nb, 6, d), lambda i, *_: (boff // nb + i, 0, 0))


def _ada_kernel(c_ref, w_ref, b_ref, o_ref):
    c = c_ref[...]
    s = c * jax.nn.sigmoid(c)
    o_ref[0] = jnp.dot(s, w_ref[0], precision=HIGHEST, preferred_element_type=F32) + b_ref[0]


def _ada(c_all, w_ada, b_ada):
    depth, d, n6 = w_ada.shape
    rows = c_all.shape[0]
    return pl.pallas_call(
        _ada_kernel,
        grid=(depth, n6 // ADA_COLS),
        in_specs=[
            pl.BlockSpec((rows, d), lambda l, j: (0, 0)),
            pl.BlockSpec((1, d, ADA_COLS), lambda l, j: (l, 0, j)),
            pl.BlockSpec((1, 1, ADA_COLS), lambda l, j: (l, 0, j)),
        ],
        out_specs=pl.BlockSpec((1, rows, ADA_COLS), lambda l, j: (l, 0, j)),
        out_shape=jax.ShapeDtypeStruct((depth, rows, n6), F32),
        compiler_params=_params(("arbitrary", "arbitrary")),
        name="ada",
    )(c_all, w_ada, b_ada.reshape(depth, 1, n6))


def _inproj_kernel(x_ref, mod_ref, g_ref, w_ref, o_ref):
    rows = x_ref.shape[0]
    hn = _rms(x_ref[...], g_ref[...]) * (1.0 + _mod_row(mod_ref, 1, rows)) + _mod_row(mod_ref, 0, rows)
    o_ref[...] = jnp.dot(hn.astype(BF16), w_ref[...], preferred_element_type=F32)


def _inproj(x, mod, boff, g1, w_in, t):
    n, d = x.shape
    cols = w_in.shape[1]
    tm = INPROJ_TILE
    assert n % tm == 0
    return pl.pallas_call(
        _inproj_kernel,
        grid=(n // tm,),
        in_specs=[
            pl.BlockSpec((tm, d), lambda i: (i, 0)),
            _mod_spec(t, boff, d, tm),
            pl.BlockSpec((1, d), lambda i: (0, 0)),
            pl.BlockSpec((d, cols), lambda i: (0, 0)),
        ],
        out_specs=pl.BlockSpec((tm, cols), lambda i: (i, 0)),
        out_shape=jax.ShapeDtypeStruct((n, cols), F32),
        compiler_params=_params(("arbitrary",)),
        name="inproj",
    )(x, mod, g1, w_in)


def _rglru_kernel(xr_ref, yr_ref, cst_ref, h0_ref, cw_ref, cb_ref, wr_ref, br_ref, wi_ref, bi_ref,
                  lam_ref, o_ref, cnew_ref, hlast_ref, tail_s, h_s, *, tt):
    j = pl.program_id(1)
    w = xr_ref.shape[1]

    @pl.when(j == 0)
    def _():
        tail_s[...] = jnp.zeros((V7X_SUBLANES, w), F32)
        tail_s[V7X_SUBLANES - (CONV_W - 1):, :] = cst_ref[0]
        h_s[...] = h0_ref[0]

    x = xr_ref[...]
    ext = jnp.concatenate([tail_s[...], x], axis=0)
    cw = cw_ref[...]
    xc = cb_ref[...] + x * cw[CONV_W - 1:CONV_W, :]
    for d in range(1, CONV_W):
        xs = pltpu.roll(ext, d, axis=0)[V7X_SUBLANES:, :]
        xc = xc + xs * cw[CONV_W - 1 - d:CONV_W - d, :]
    tail_s[...] = x[tt - V7X_SUBLANES:, :]
    cnew_ref[0] = x[tt - (CONV_W - 1):, :]

    xb = xc.astype(BF16)
    gr = jax.nn.sigmoid(jnp.dot(xb, wr_ref[...], preferred_element_type=F32) + br_ref[...])
    gi = jax.nn.sigmoid(jnp.dot(xb, wi_ref[...], preferred_element_type=F32) + bi_ref[...])
    nl = -lam_ref[...]
    sp = jnp.maximum(nl, 0.0) + jnp.log1p(jnp.exp(-jnp.abs(nl)))
    a = jnp.exp((-RG_C) * gr * sp)
    u = jnp.sqrt(1.0 - a * a) * gi * xc

    rows = lax.broadcasted_iota(I32, (tt, w), 0)
    d = 1
    while d < tt:
        keep = rows >= d
        u = jnp.where(keep, u + a * pltpu.roll(u, d, axis=0), u)
        a = jnp.where(keep, a * pltpu.roll(a, d, axis=0), a)
        d *= 2
    h = u + a * h_s[...]
    h_s[...] = h[tt - 1:, :]
    hlast_ref[0] = h[tt - 1:, :]
    o_ref[...] = (jax.nn.gelu(yr_ref[...], approximate=True) * h).astype(BF16)


def _rglru(proj, conv_st, h0, cw, cb, wr, br, wi, bi, lam, bsz, t):
    n = proj.shape[0]
    w = cw.shape[1]
    tt = min(t, ROW_TILE)
    nt = t // tt
    assert tt >= V7X_SUBLANES and t % tt == 0
    full = lambda shape: pl.BlockSpec(shape, lambda b, j: (0,) * len(shape))
    return pl.pallas_call(
        functools.partial(_rglru_kernel, tt=tt),
        grid=(bsz, nt),
        in_specs=[
            pl.BlockSpec((tt, w), lambda b, j: (b * nt + j, 0)),
            pl.BlockSpec((tt, w), lambda b, j: (b * nt + j, 1)),
            pl.BlockSpec((1, CONV_W - 1, w), lambda b, j: (b, 0, 0)),
            pl.BlockSpec((1, 1, w), lambda b, j: (b, 0, 0)),
            full((CONV_W, w)), full((1, w)), full((w, w)), full((1, w)), full((w, w)), full((1, w)),
            full((1, w)),
        ],
        out_specs=[
            pl.BlockSpec((tt, w), lambda b, j: (b * nt + j, 0)),
            pl.BlockSpec((1, CONV_W - 1, w), lambda b, j: (b, 0, 0)),
            pl.BlockSpec((1, 1, w), lambda b, j: (b, 0, 0)),
        ],
        out_shape=[
            jax.ShapeDtypeStruct((n, w), BF16),
            jax.ShapeDtypeStruct((bsz, CONV_W - 1, w), F32),
            jax.ShapeDtypeStruct((bsz, 1, w), F32),
        ],
        scratch_shapes=[pltpu.VMEM((V7X_SUBLANES, w), F32), pltpu.VMEM((1, w), F32)],
        compiler_params=_params(("arbitrary", "arbitrary")),
        name="rglru",
    )(proj, proj, conv_st, h0.reshape(bsz, 1, w), cw, cb, wr, br, wi, bi, lam)


def _hgrn_tables(c):
    t = jnp.arange(c)[:, None]
    r = jnp.arange(c)[None, :]
    tri = (r <= t).astype(BF16)
    bd = min(c, V7X_LANES)
    t, r = t[:bd, :bd], r[:bd, :bd]
    masks = [(t == r)]
    h = 1
    while 2 * h <= bd:
        m_t = (t // (2 * h)) * (2 * h) + h
        same = (t // (2 * h)) == (r // (2 * h))
        masks.append(same & (t >= m_t) & (r < m_t))
        h *= 2
    return tri, jnp.stack(masks).astype(F32)


def _hgrn_kernel(q_ref, f_ref, v_ref, g_ref, lb_ref, ng_ref, s0_ref, tri_ref, mask_ref,
                 o_ref, slast_ref, s_s, acc_s, g_s, *, c, nlev):
    j = pl.program_id(1)
    w = f_ref.shape[1]

    @pl.when(j == 0)
    def _():
        s_s[...] = s0_ref[0]

    ff = f_ref[...]
    lb = lb_ref[...]
    log_sig = -(jnp.maximum(-ff, 0.0) + jnp.log1p(jnp.exp(-jnp.abs(ff))))
    log_f = log_sig + jnp.log1p(lb * jnp.exp(-ff))
    k = (1.0 - lb) * jax.nn.sigmoid(-ff)
    q = q_ref[...]

    hi = log_f.astype(BF16)
    r1 = log_f - hi.astype(F32)
    mid = r1.astype(BF16)
    lo = (r1 - mid.astype(F32)).astype(BF16)
    tri = tri_ref[...]
    gcum = (jnp.dot(tri, hi, preferred_element_type=F32) + jnp.dot(tri, mid, preferred_element_type=F32)
            + jnp.dot(tri, lo, preferred_element_type=F32))
    g_s[...] = gcum

    row = lax.broadcasted_iota(I32, (c, w), 0)
    shifted = {0: gcum}

    def midpoint(h):
        if h >= V7X_SUBLANES:
            blocks = [jnp.broadcast_to(g_s[b * 2 * h + h - 1:b * 2 * h + h, :], (2 * h, w))
                      for b in range(c // (2 * h))]
            return blocks[0] if len(blocks) == 1 else jnp.concatenate(blocks, axis=0)
        phase = row & (2 * h - 1)
        out = gcum
        for p in range(2 * h):
            off = (h - 1) - p
            if off not in shifted:
                shifted[off] = pltpu.roll(gcum, (-off) % c, axis=0)
            if off != 0:
                out = jnp.where(phase == p, shifted[off], out)
        return out

    bd = mask_ref.shape[1]

    def pair_scores(l, qw, kw):
        h = 0 if l == 0 else 2 ** (l - 1)
        for hd in range(HG_HEADS):
            sl = slice(hd * HG_D, (hd + 1) * HG_D)
            if 2 * h <= bd:
                keep = mask_ref[l] > 0.0
                for b in range(c // bd):
                    rs = slice(b * bd, (b + 1) * bd)
                    p = lax.dot_general(qw[rs, sl], kw[rs, sl], NT_DIMS, preferred_element_type=F32)
                    acc_s[hd, rs, rs] = jnp.where(keep, p, 0.0 if l == 0 else acc_s[hd, rs, rs])
            else:
                for b in range(c // (2 * h)):
                    rows = slice(b * 2 * h + h, (b + 1) * 2 * h)
                    cols = slice(b * 2 * h, b * 2 * h + h)
                    acc_s[hd, rows, cols] = lax.dot_general(qw[rows, sl], kw[cols, sl], NT_DIMS,
                                                            preferred_element_type=F32)

    if c > bd:
        acc_s[...] = jnp.zeros(acc_s.shape, F32)
    qb = q.astype(BF16)
    kb = k.astype(BF16)
    pair_scores(0, qb, kb)
    for l in range(1, nlev + 1):
        wgt = jnp.exp(-jnp.abs(gcum - midpoint(2 ** (l - 1)))).astype(BF16)
        pair_scores(l, qb * wgt, kb * wgt)

    glast = gcum[c - 1:, :]
    qg = (q * jnp.exp(gcum)).astype(BF16)
    kdec = (k * jnp.exp(glast - gcum)).astype(BF16)
    dlast = jnp.exp(glast)
    vb = v_ref[...].astype(BF16)
    for hd in range(HG_HEADS):
        sl = slice(hd * HG_D, (hd + 1) * HG_D)
        s_old = s_s[hd]
        o = (jnp.dot(acc_s[hd].astype(BF16), vb[:, sl], preferred_element_type=F32)
             + jnp.dot(qg[:, sl], s_old.astype(BF16), preferred_element_type=F32))
        kv = lax.dot_general(kdec[:, sl], vb[:, sl], TN_DIMS, preferred_element_type=F32)
        dcol = jnp.broadcast_to(dlast[:, sl], (HG_D, HG_D)).T
        s_s[hd] = dcol * s_old + kv
        gate = g_ref[:, sl]
        o_ref[:, sl] = (_rms(o, ng_ref[...]) * (gate * jax.nn.sigmoid(gate))).astype(BF16)
    slast_ref[0] = s_s[...]


def _hgrn(proj, lb, ng, s0, bsz, t):
    n = proj.shape[0]
    w = HG_HEADS * HG_D
    c = min(t, HGRN_CHUNK)
    nt = t // c
    nlev = int(math.log2(c))
    assert 2 ** nlev == c and t % c == 0
    tri, masks = _hgrn_tables(c)
    col = lambda k: pl.BlockSpec((c, w), lambda b, j: (b * nt + j, k))
    full = lambda shape: pl.BlockSpec(shape, lambda b, j: (0,) * len(shape))
    return pl.pallas_call(
        functools.partial(_hgrn_kernel, c=c, nlev=nlev),
        grid=(bsz, nt),
        in_specs=[
            col(2), col(3), col(4), col(5),
            full((1, w)), full((1, HG_D)),
            pl.BlockSpec((1, HG_HEADS, HG_D, HG_D), lambda b, j: (b, 0, 0, 0)),
            full(tri.shape), full(masks.shape),
        ],
        out_specs=[
            pl.BlockSpec((c, w), lambda b, j: (b * nt + j, 0)),
            pl.BlockSpec((1, HG_HEADS, HG_D, HG_D), lambda b, j: (b, 0, 0, 0)),
        ],
        out_shape=[
            jax.ShapeDtypeStruct((n, w), BF16),
            jax.ShapeDtypeStruct((bsz, HG_HEADS, HG_D, HG_D), F32),
        ],
        scratch_shapes=[pltpu.VMEM((HG_HEADS, HG_D, HG_D), F32),
                        pltpu.VMEM((HG_HEADS, c, c), F32),
                        pltpu.VMEM((c, w), F32)],
        compiler_params=_params(("arbitrary", "arbitrary")),
        name="hgrn",
    )(proj, proj, proj, proj, lb, ng, s0, tri, masks)


def _outproj_kernel(org_ref, ohg_ref, x_ref, mod_ref, wo_ref, g2_ref, wr_ref, brt_ref, tri_ref,
                    low_ref, cin_ref, x1_ref, hb_ref, pos_ref, prob_ref, segb_ref, segl_ref, cnt_ref,
                    carry_s, *, tm, parts):
    @pl.when(pl.program_id(0) == 0)
    def _():
        carry_s[...] = cin_ref[...]

    carry = carry_s[...]
    for part in range(parts):
        carry = _outproj_part(part, parts, tm, carry, org_ref, ohg_ref, x_ref, mod_ref, wo_ref,
                              g2_ref, wr_ref, brt_ref, tri_ref, low_ref, x1_ref, hb_ref, pos_ref,
                              prob_ref, segb_ref, segl_ref)
    carry_s[...] = carry
    cnt_ref[...] = carry


def _outproj_part(part, parts, tm, carry, org_ref, ohg_ref, x_ref, mod_ref, wo_ref, g2_ref, wr_ref,
                  brt_ref, tri_ref, low_ref, x1_ref, hb_ref, pos_ref, prob_ref, segb_ref, segl_ref):
    rs = slice(part * tm, (part + 1) * tm)
    mrow = lambda k: _mod_row(mod_ref, k, tm, part, parts)
    half = org_ref.shape[1]
    mix = (jnp.dot(org_ref[rs, :], wo_ref[:half, :], preferred_element_type=F32)
           + jnp.dot(ohg_ref[rs, :], wo_ref[half:, :], preferred_element_type=F32))
    x1 = x_ref[rs, :] + mrow(2) * mix
    x1_ref[rs, :] = x1
    hn = _rms(x1, g2_ref[...]) * (1.0 + mrow(4)) + mrow(3)
    hn_hi = hn.astype(BF16)
    hb_ref[rs, :] = hn_hi

    ne = brt_ref.shape[0]
    hn_lo = (hn - hn_hi.astype(F32)).astype(BF16)
    both = jnp.dot(hn_hi, wr_ref[...], preferred_element_type=F32)
    logits_tm = (both[:, :V7X_LANES] + both[:, V7X_LANES:]
                 + jnp.dot(hn_lo, wr_ref[:, :V7X_LANES], preferred_element_type=F32))
    logits = logits_tm.T[:ne, :] + brt_ref[...]
    eio = lax.broadcasted_iota(I32, (ne, tm), 0)
    cur = logits
    vals, sels = [], []
    for _ in range(TOP_K):
        mx = jnp.max(cur, axis=0, keepdims=True)
        ix = jnp.min(jnp.where(cur == mx, eio, ne), axis=0, keepdims=True)
        sel = eio == ix
        vals.append(mx)
        sels.append(sel)
        cur = jnp.where(sel, -jnp.inf, cur)
    ex = [jnp.exp(v - vals[0]) for v in vals]
    den = ex[0] + ex[1] + ex[2] + ex[3]
    probs = [e / den for e in ex]

    onehot = sels[0].astype(F32)
    for sel in sels[1:]:
        onehot = onehot + sel.astype(F32)
    incl = jnp.dot(onehot.astype(BF16), tri_ref[...], preferred_element_type=F32)
    run_len = jnp.ceil(incl[:, tm - 1:] * (1.0 / RUN_ALIGN)) * RUN_ALIGN
    run0 = jnp.dot(low_ref[...], jnp.broadcast_to(run_len, (ne, tm)).astype(BF16),
                   preferred_element_type=F32)
    local = run0 + incl - onehot
    slots = [jnp.sum(jnp.where(sel, local, 0.0), axis=0, keepdims=True) for sel in sels]
    segb_ref[part] = carry
    segl_ref[part] = run_len

    pos_ref[:, rs] = jnp.concatenate(slots, axis=0).astype(I32)
    sub = lax.broadcasted_iota(I32, (V7X_LANES, tm), 0)
    pt = jnp.zeros((V7X_LANES, tm), F32)
    for kk in range(TOP_K):
        pt = jnp.where(sub == kk, probs[kk], pt)
        pt = jnp.where(sub == TOP_K + kk, slots[kk], pt)
    prob_ref[rs, :] = pt.T
    return carry + run_len


def _router_weights(w_router):
    d, ne = w_router.shape
    assert ne <= V7X_LANES
    hi = w_router.astype(BF16)
    lo = (w_router - hi.astype(F32)).astype(BF16)
    pad = jnp.zeros((d, V7X_LANES - ne), BF16)
    return jnp.concatenate([hi, pad, lo, pad], axis=1)


def _outproj(o_rg, o_hg, x, mod, boff, w_out, g2, wr, brt, cnt_in, t):
    n, d = x.shape
    half = o_rg.shape[1]
    ne = brt.shape[0]
    tm = ROW_TILE
    parts = OUTPROJ_PARTS
    tile = tm * parts
    assert n % tile == 0 and tm <= 256
    nt = n // tm
    tri = (jnp.arange(tm)[:, None] <= jnp.arange(tm)[None, :]).astype(BF16)
    low = (jnp.arange(ne)[:, None] > jnp.arange(ne)[None, :]).astype(BF16)
    rowblk = lambda cols: pl.BlockSpec((tile, cols), lambda i: (i, 0))
    full = lambda shape: pl.BlockSpec(shape, lambda i: (0,) * len(shape))
    pertile = lambda: pl.BlockSpec((parts, ne, 1), lambda i: (i, 0, 0))
    return pl.pallas_call(
        functools.partial(_outproj_kernel, tm=tm, parts=parts),
        grid=(n // tile,),
        in_specs=[
            rowblk(half), rowblk(half), rowblk(d),
            _mod_spec(t, boff, d, tile),
            full((d, d)), full((1, d)), full(wr.shape), full((ne, 1)), full((tm, tm)), full((ne, ne)),
            full((ne, 1)),
        ],
        out_specs=[
            rowblk(d), rowblk(d), pl.BlockSpec((TOP_K, tile), lambda i: (0, i)), rowblk(V7X_LANES),
            pertile(), pertile(), full((ne, 1)),
        ],
        out_shape=[
            jax.ShapeDtypeStruct((n, d), F32),
            jax.ShapeDtypeStruct((n, d), BF16),
            jax.ShapeDtypeStruct((TOP_K, n), I32),
            jax.ShapeDtypeStruct((n, V7X_LANES), F32),
            jax.ShapeDtypeStruct((nt, ne, 1), F32),
            jax.ShapeDtypeStruct((nt, ne, 1), F32),
            jax.ShapeDtypeStruct((ne, 1), F32),
        ],
        scratch_shapes=[pltpu.VMEM((ne, 1), F32)],
        compiler_params=_params(("arbitrary",)),
        name="outproj",
    )(o_rg, o_hg, x, mod, w_out, g2, wr, brt, tri, low, cnt_in)


def _pow2_pieces(hi, lo=1):
    return tuple(1 << b for b in range(int(math.log2(hi)), int(math.log2(lo)) - 1, -1))


def _tile_slots(ne):
    worst = TOP_K * ROW_TILE + ne * (RUN_ALIGN - 1)
    return -(-worst // V7X_LANES) * V7X_LANES


def _stored_rows(ref, first_row, rows):
    start = first_row * ROW_PARTS
    if not isinstance(start, int):
        start = pl.multiple_of(start, V7X_SUBLANES)
    return ref.at[pl.ds(start, rows * ROW_PARTS), :]


def _store_packed(ref, words):
    rows = words.shape[0]
    for c in range(ROW_PARTS):
        ref[pl.ds(c, rows, stride=ROW_PARTS), :] = words[:, c * V7X_LANES:(c + 1) * V7X_LANES]


def _load_packed(ref):
    rows = ref.shape[0] // ROW_PARTS
    return jnp.concatenate([ref[pl.ds(c, rows, stride=ROW_PARTS), :] for c in range(ROW_PARTS)], axis=1)


def _copy_run(n, make_copy, wait, pieces):
    for rows in pieces:
        done = n & (-2 * rows)

        @pl.when((n & rows) != 0)
        def _():
            cp = make_copy(done, rows)
            cp.wait() if wait else cp.start()


RUN_PIECES = _pow2_pieces(ROW_TILE, RUN_ALIGN)
BUF_ROW_BITS = 11
LIST_HEAD = V7X_SUBLANES


def _run_lists(seg_start, seg_len, ne):
    assert _tile_slots(ne) <= 1 << BUF_ROW_BITS
    tiles = seg_len.shape[0]
    buf0 = jnp.cumsum(seg_len, axis=1) - seg_len
    j = jnp.arange(ne, dtype=I32)
    counts, entries = [], []
    for rows in RUN_PIECES:
        has = (seg_len & rows) != 0
        done = seg_len & (-2 * rows)
        packed = ((seg_start + done) << BUF_ROW_BITS) | (buf0 + done)
        rank = jnp.cumsum(has.astype(I32), axis=1) - 1
        place = has[:, :, None] & (rank[:, :, None] == j[None, None, :])
        entries.append(jnp.sum(jnp.where(place, packed[:, :, None], 0), axis=1))
        counts.append(jnp.sum(has.astype(I32), axis=1))
    head = jnp.stack(counts + [jnp.zeros((tiles,), I32)] * (LIST_HEAD - len(RUN_PIECES)), axis=1)
    flat = jnp.concatenate([head] + entries, axis=1)
    width = -(-flat.shape[1] // V7X_LANES) * V7X_LANES
    flat = jnp.pad(flat, ((0, 0), (0, width - flat.shape[1])))
    return flat.reshape(tiles, 1, width)


def _list_spec(ne, index_map):
    width = -(-(LIST_HEAD + len(RUN_PIECES) * ne) // V7X_LANES) * V7X_LANES
    return pl.BlockSpec((1, 1, width), index_map, memory_space=pltpu.SMEM)


def _tile_runs(list_ref, ne, make_copy, wait):
    for pi, rows in enumerate(RUN_PIECES):
        def one(jj, pi=pi, rows=rows):
            v = list_ref[0, 0, LIST_HEAD + pi * ne + jj]
            cp = make_copy(v >> BUF_ROW_BITS, v & ((1 << BUF_ROW_BITS) - 1), rows)
            cp.wait() if wait else cp.start()

        def pair(jj, carry, one=one):
            one(2 * jj)
            one(2 * jj + 1)
            return carry

        cnt = list_ref[0, 0, pi]
        lax.fori_loop(0, cnt >> 1, pair, 0)
        pl.when((cnt & 1) != 0)(functools.partial(one, cnt - 1))


def _pack_pairs(lo_cols, hi_cols):
    a = lax.bitcast_convert_type(lo_cols, jnp.uint32)
    b = lax.bitcast_convert_type(hi_cols, jnp.uint32)
    return (a & jnp.uint32(0xFFFF0000)) | (b >> 16)


def _unpack_pairs(words):
    a = lax.bitcast_convert_type(words & jnp.uint32(0xFFFF0000), F32).astype(BF16)
    b = lax.bitcast_convert_type(words << 16, F32).astype(BF16)
    return a, b


def _dispatch_kernel(zoff_ref, zcnt_ref, lst_ref, lst1_ref, lst2_ref, pos_ref, hba_ref, hbb_ref,
                     out_ref, buf_s, zero_s, sem, zsem, *, ne, tiles_a):
    i = pl.program_id(0)
    last = pl.num_programs(0) - 1
    slot = i % 2
    tt, d = hba_ref.shape
    dh = d // 2

    def runs(list_ref, s, wait):
        _tile_runs(list_ref, ne,
                   lambda g, b, rows: pltpu.make_async_copy(
                       _stored_rows(buf_s.at[s], b, rows), _stored_rows(out_ref, g, rows), sem.at[s]),
                   wait)

    @pl.when(i >= 2)
    def _():
        runs(lst2_ref, slot, True)

    pos = pos_ref[...]
    slot_row = lax.broadcasted_iota(I32, (buf_s.shape[1] // ROW_PARTS, tt), 0)
    hit = slot_row == pos[0:1, :]
    for kk in range(1, TOP_K):
        hit = hit | (slot_row == pos[kk:kk + 1, :])
    perm = jnp.where(hit, 1.0, 0.0).astype(BF16)
    hb = jnp.where(i < tiles_a, hba_ref[...], hbb_ref[...])
    _store_packed(buf_s.at[slot], _pack_pairs(jnp.dot(perm, hb[:, :dh], preferred_element_type=F32),
                                              jnp.dot(perm, hb[:, dh:], preferred_element_type=F32)))
    runs(lst_ref, slot, False)

    @pl.when(i == last)
    def _():
        @pl.when(i >= 1)
        def _():
            runs(lst1_ref, 1 - slot, True)
        runs(lst_ref, slot, True)

        zero_s[...] = jnp.zeros(zero_s.shape, zero_s.dtype)
        zrows = zero_s.shape[0] // ROW_PARTS
        zcopy = lambda g, rows: pltpu.make_async_copy(
            _stored_rows(zero_s, 0, rows), _stored_rows(out_ref, g, rows), zsem)

        def zero_run(e, carry):
            lo = zoff_ref[e] + zcnt_ref[e]
            n = zoff_ref[e + 1] - lo
            nfull = n // zrows
            rem = n - nfull * zrows
            for wait in (False, True):
                def full(c, c2):
                    cp = zcopy(lo + c * zrows, zrows)
                    cp.wait() if wait else cp.start()
                    return c2
                lax.fori_loop(0, nfull, full, 0)
                _copy_run(rem, lambda o, rows: zcopy(lo + nfull * zrows + o, rows), wait,
                          _pow2_pieces(zrows // 2, RUN_ALIGN))
            return carry

        lax.fori_loop(0, zcnt_ref.shape[0], zero_run, 0)


def _dispatch(lists, zoff, zcnt, pos_t, hb_a, hb_b, n_rows, ne):
    d = hb_a.shape[1]
    tt = ROW_TILE
    ta, tb = hb_a.shape[0] // tt, hb_b.shape[0] // tt
    assert hb_a.shape[0] % tt == 0 and hb_b.shape[0] % tt == 0
    back = lambda k: _list_spec(ne, lambda i, *_: (jnp.maximum(i - k, 0), 0, 0))
    return pl.pallas_call(
        functools.partial(_dispatch_kernel, ne=ne, tiles_a=ta),
        grid_spec=pltpu.PrefetchScalarGridSpec(
            num_scalar_prefetch=2,
            grid=(ta + tb,),
            in_specs=[back(0), back(1), back(2),
                      pl.BlockSpec((TOP_K, tt), lambda i, *_: (0, i)),
                      pl.BlockSpec((tt, d), lambda i, *_: (jnp.minimum(i, ta - 1), 0)),
                      pl.BlockSpec((tt, d), lambda i, *_: (jnp.maximum(i - ta, 0), 0))],
            out_specs=pl.BlockSpec(memory_space=pl.ANY),
            scratch_shapes=[pltpu.VMEM((2, _tile_slots(ne) * ROW_PARTS, V7X_LANES), jnp.uint32),
                            pltpu.VMEM((tt * ROW_PARTS, V7X_LANES), jnp.uint32),
                            pltpu.SemaphoreType.DMA((2,)), pltpu.SemaphoreType.DMA(())],
        ),
        out_shape=jax.ShapeDtypeStruct((n_rows * ROW_PARTS, V7X_LANES), jnp.uint32),
        compiler_params=_params(("arbitrary",)),
        name="dispatch",
    )(zoff, zcnt, lists, lists, lists, pos_t, hb_a, hb_b)


def _ffn_kernel(te_ref, nu_ref, xs_ref, wg_ref, bg_ref, wu_ref, bu_ref, wd_ref, bd_ref, y_ref,
                wg_s, wu_s, wd_s):
    i = pl.program_id(0)
    valid = i < nu_ref[0]
    changed = (i == 0) | (te_ref[i] != te_ref[jnp.maximum(i - 1, 0)])

    @pl.when(valid & changed)
    def _():
        wg_s[...] = wg_ref[...].astype(BF16)
        wu_s[...] = wu_ref[...].astype(BF16)
        wd_s[...] = wd_ref[...].astype(BF16)

    @pl.when(valid)
    def _():
        dh = ROW_PARTS * xs_ref.shape[1]
        xa, xb = _unpack_pairs(_load_packed(xs_ref))
        gate = (jnp.dot(xa, wg_s[:dh, :], preferred_element_type=F32)
                + jnp.dot(xb, wg_s[dh:, :], preferred_element_type=F32) + bg_ref[...])
        up = (jnp.dot(xa, wu_s[:dh, :], preferred_element_type=F32)
              + jnp.dot(xb, wu_s[dh:, :], preferred_element_type=F32) + bu_ref[...])
        gate = jnp.minimum(gate, SWIGLU_LIMIT)
        up = jnp.clip(up, -SWIGLU_LIMIT, SWIGLU_LIMIT)
        glu = gate * jax.nn.sigmoid(SWIGLU_ALPHA * gate)
        act = ((up + 1.0) * glu).astype(BF16)
        y = (jnp.dot(act, wd_s[...], preferred_element_type=F32) + bd_ref[...]).astype(BF16).astype(F32)
        yh = y.shape[1] // 2
        _store_packed(y_ref, _pack_pairs(y[:, :yh], y[:, yh:]))

    @pl.when(jnp.logical_not(valid))
    def _():
        y_ref[...] = jnp.zeros(y_ref.shape, y_ref.dtype)


def _ffn(layer, tile_expert, n_used, xs, w_gate, b_gate, w_up, b_up, w_down, b_down):
    d, dff = w_gate.shape[2], w_gate.shape[3]
    tm = FFN_TILE * ROW_PARTS
    assert xs.shape[1] == V7X_LANES and xs.shape[0] % tm == 0
    wspec = lambda r, c: pl.BlockSpec((None, None, r, c), lambda i, te, nu: (layer, te[i], 0, 0))
    return pl.pallas_call(
        _ffn_kernel,
        grid_spec=pltpu.PrefetchScalarGridSpec(
            num_scalar_prefetch=2,
            grid=(xs.shape[0] // tm,),
            in_specs=[
                pl.BlockSpec((tm, V7X_LANES), lambda i, te, nu: (i, 0)),
                wspec(d, dff), wspec(1, dff), wspec(d, dff), wspec(1, dff), wspec(dff, d), wspec(1, d),
            ],
            out_specs=pl.BlockSpec((tm, V7X_LANES), lambda i, te, nu: (i, 0)),
            scratch_shapes=[pltpu.VMEM((d, dff), BF16), pltpu.VMEM((d, dff), BF16),
                            pltpu.VMEM((dff, d), BF16)],
        ),
        out_shape=jax.ShapeDtypeStruct(xs.shape, jnp.uint32),
        compiler_params=_params(("arbitrary",)),
        name="ffn",
    )(tile_expert, n_used, xs, w_gate, b_gate, w_up, b_up, w_down, b_down)


def _combine_kernel(lst_ref, nxt_ref, y_ref, prob_ref, x1_ref, mod_ref, gf_ref, o_ref,
                    buf_s, sem, *, ne, final):
    i = pl.program_id(0)
    slot = i % 2
    tt = x1_ref.shape[0]

    def runs(list_ref, s, wait):
        _tile_runs(list_ref, ne,
                   lambda g, b, rows: pltpu.make_async_copy(
                       _stored_rows(y_ref, g, rows), _stored_rows(buf_s.at[s], b, rows), sem.at[s]),
                   wait)

    @pl.when(i == 0)
    def _():
        buf_s[...] = jnp.zeros(buf_s.shape, buf_s.dtype)
        runs(lst_ref, slot, False)

    @pl.when(i + 1 < pl.num_programs(0))
    def _():
        runs(nxt_ref, 1 - slot, False)

    runs(lst_ref, slot, True)

    ya, yb = _unpack_pairs(_load_packed(buf_s.at[slot]))
    pp = prob_ref[...]
    slots = buf_s.shape[1] // ROW_PARTS
    slot_col = lax.broadcasted_iota(I32, (tt, slots), 1)
    wgt = jnp.zeros((tt, slots), F32)
    for kk in range(TOP_K):
        pos = pp[:, TOP_K + kk:TOP_K + kk + 1].astype(I32)
        wgt = jnp.where(slot_col == pos, pp[:, kk:kk + 1], wgt)
    wgt = wgt.astype(BF16)
    y = jnp.concatenate([jnp.dot(wgt, ya, preferred_element_type=F32),
                         jnp.dot(wgt, yb, preferred_element_type=F32)], axis=1)
    x2 = x1_ref[...] + _mod_row(mod_ref, 5, tt) * y
    if final:
        x2 = _rms(x2, gf_ref[...])
    o_ref[...] = x2


def _combine(lists, y_rows, prob, x1, mod, boff, g_final, roff, t, ne, final):
    n, d = x1.shape
    tt = ROW_TILE
    r0 = roff // tt
    assert roff % tt == 0 and n % tt == 0
    last = r0 + n // tt - 1
    return pl.pallas_call(
        functools.partial(_combine_kernel, ne=ne, final=final),
        grid=(n // tt,),
        in_specs=[
            _list_spec(ne, lambda i: (r0 + i, 0, 0)),
            _list_spec(ne, lambda i: (jnp.minimum(r0 + i + 1, last), 0, 0)),
            pl.BlockSpec(memory_space=pl.ANY),
            pl.BlockSpec((tt, V7X_LANES), lambda i: (i, 0)),
            pl.BlockSpec((tt, d), lambda i: (i, 0)),
            _mod_spec(t, boff, d),
            pl.BlockSpec((1, d), lambda i: (0, 0)),
        ],
        out_specs=pl.BlockSpec((tt, d), lambda i: (i, 0)),
        scratch_shapes=[pltpu.VMEM((2, _tile_slots(ne) * ROW_PARTS, V7X_LANES), jnp.uint32),
                        pltpu.SemaphoreType.DMA((2,))],
        out_shape=jax.ShapeDtypeStruct((n, d), F32),
        compiler_params=_params(("arbitrary",)),
        name="combine",
    )(lists, lists, y_rows, prob, x1, mod, g_final)


def _block_diag(w):
    depth, nb, c, _ = w.shape
    eye = jnp.eye(nb, dtype=w.dtype)
    return (eye[None, :, None, :, None] * w[:, :, :, None, :]).reshape(depth, nb * c, nb * c)


def kernel(x_prompt, x_sample, state_conv, state_rglru, state_hgrn, c_prompt, c_sample, w_ada, b_ada, g_norm1, g_norm2, w_in, conv_w, conv_b, w_rgate, b_rgate, w_igate, b_igate, lru_lambda, hgrn_lb, hgrn_norm_g, w_out, w_router, b_router, w_gate, b_gate, w_up, b_up, w_down, b_down, g_final):
    depth = w_ada.shape[0]
    bp, tp, d = x_prompt.shape
    bs, ts, _ = x_sample.shape
    w_rg = conv_w.shape[2]
    ne = w_router.shape[2]
    n_p, n_s = bp * tp, bs * ts
    n_tot = n_p + n_s
    span = max(max(ROW_TILE * OUTPROJ_PARTS, INPROJ_TILE) // ts, 1)
    boff_s = -(-bp // span) * span
    groups = ((bp, tp, 0, 0), (bs, ts, boff_s, n_p))
    n_c_pad = -(-(boff_s + bs) // V7X_SUBLANES) * V7X_SUBLANES
    c_all = jnp.concatenate([c_prompt, jnp.zeros((boff_s - bp, d), F32), c_sample,
                             jnp.zeros((n_c_pad - boff_s - bs, d), F32)], axis=0)
    mod_all = _ada(c_all, w_ada, b_ada).reshape(depth, n_c_pad, 6, d)

    p_lb = jax.nn.softmax(hgrn_lb.astype(F32), axis=0)
    lb_all = jnp.cumsum(p_lb, axis=0) - p_lb[0]

    w_in_b = w_in.astype(BF16)
    w_out_b = w_out.astype(BF16)
    wr_all = _block_diag(w_rgate).astype(BF16)
    wi_all = _block_diag(w_igate).astype(BF16)

    assert d // 2 == ROW_PARTS * V7X_LANES
    run_pad = (n_tot // ROW_TILE) * ne * (RUN_ALIGN - 1)
    n_tiles = -(-(TOP_K * n_tot + run_pad + ne * (FFN_TILE - 1)) // FFN_TILE)
    n_rows = n_tiles * FFN_TILE

    xs = [x_prompt.reshape(n_p, d), x_sample.reshape(n_s, d)]
    conv_in = [jnp.zeros((depth, bp, CONV_W - 1, w_rg), F32), state_conv]
    h_in = [jnp.zeros((depth, bp, w_rg), F32), state_rglru]
    s_in = [jnp.zeros((depth, bp, HG_HEADS, HG_D, HG_D), F32), state_hgrn]
    conv_out, h_out, s_out = ([], []), ([], []), ([], [])

    for l in range(depth):
        mod = mod_all[l]
        wr_d, wi_d = wr_all[l], wi_all[l]
        cnt = jnp.zeros((ne, 1), F32)
        x1s, hbs, poss, probs, segbs, segls = [], [], [], [], [], []
        for gi, (bsz, t, boff, _) in enumerate(groups):
            proj = _inproj(xs[gi], mod, boff, g_norm1[l][None], w_in_b[l], t)
            o_rg, c_new, h_last = _rglru(
                proj, conv_in[gi][l], h_in[gi][l], conv_w[l], conv_b[l][None], wr_d, b_rgate[l][None],
                wi_d, b_igate[l][None], lru_lambda[l][None], bsz, t)
            o_hg, s_last = _hgrn(proj, lb_all[l][None], hgrn_norm_g[l][None], s_in[gi][l], bsz, t)
            x1, hb, pos_t, prob, segb, segl, cnt = _outproj(
                o_rg, o_hg, xs[gi], mod, boff, w_out_b[l], g_norm2[l][None], _router_weights(w_router[l]),
                b_router[l][:, None], cnt, t)
            conv_out[gi].append(c_new)
            h_out[gi].append(h_last[:, 0])
            s_out[gi].append(s_last)
            x1s.append(x1)
            hbs.append(hb)
            poss.append(pos_t)
            probs.append(prob)
            segbs.append(segb[:, :, 0])
            segls.append(segl[:, :, 0])

        counts = cnt[:, 0].astype(I32)
        padded = (counts + FFN_TILE - 1) // FFN_TILE * FFN_TILE
        pad_end = jnp.cumsum(padded)
        off = jnp.concatenate([jnp.zeros((1,), I32), pad_end]).astype(I32)
        n_used = (pad_end[-1] // FFN_TILE).astype(I32)
        tile_start = jnp.minimum(jnp.arange(n_tiles, dtype=I32), n_used - 1) * FFN_TILE
        tile_expert = jnp.minimum(jnp.sum(tile_start[:, None] >= pad_end[None, :], axis=1), ne - 1).astype(I32)
        lists = _run_lists(off[None, :ne] + jnp.concatenate(segbs, axis=0).astype(I32),
                           jnp.concatenate(segls, axis=0).astype(I32), ne)
        zoff = jnp.concatenate([off, jnp.full((1,), n_rows, I32)])
        zcnt = jnp.concatenate([counts, jnp.zeros((1,), I32)])

        x_sorted = _dispatch(lists, zoff, zcnt, jnp.concatenate(poss, axis=1), hbs[0], hbs[1],
                             n_rows, ne)
        y_rows = _ffn(l, tile_expert, n_used.reshape(1), x_sorted, w_gate, b_gate[:, :, None, :],
                      w_up, b_up[:, :, None, :], w_down, b_down[:, :, None, :])
        final = l == depth - 1
        xs = [_combine(lists, y_rows, probs[gi], x1s[gi], mod, boff, g_final[None],
                       roff, t, ne, final)
              for gi, (bsz, t, boff, roff) in enumerate(groups)]

    return (xs[0].reshape(bp, tp, d), xs[1].reshape(bs, ts, d),
            jnp.stack(conv_out[0]), jnp.stack(h_out[0]), jnp.stack(s_out[0]),
            jnp.stack(conv_out[1]), jnp.stack(h_out[1]), jnp.stack(s_out[1]))
```

```python
import functools
import math

import jax
import jax.numpy as jnp
from jax import lax
from jax.experimental import pallas as pl
from jax.experimental.pallas import tpu as pltpu

F32 = jnp.float32
BF16 = jnp.bfloat16
I32 = jnp.int32
HIGHEST = lax.Precision.HIGHEST

EPS = 1e-6
RG_C = 8.0
CONV_W = 4
HG_HEADS = 4
HG_D = 128
TOP_K = 4
SWIGLU_LIMIT = 7.0
SWIGLU_ALPHA = 1.702
LOG2_E = 1.4426950408889634

V7X_LANES = 128
V7X_SUBLANES = 8
V7X_VMEM_LIMIT_BYTES = 56 * 1024 * 1024

ROW_TILE = 256
INPROJ_TILE = 512
HGRN_CHUNK = 256
OUTPROJ_PARTS = 2
FFN_TILE = 512
ROW_PARTS = 4
RUN_ALIGN = V7X_SUBLANES // ROW_PARTS
ADA_COLS = 1536

NT_DIMS = (((1,), (1,)), ((), ()))
TN_DIMS = (((0,), (0,)), ((), ()))


def _params(sem, vmem=V7X_VMEM_LIMIT_BYTES):
    return pltpu.CompilerParams(dimension_semantics=sem, vmem_limit_bytes=vmem)


def _rms(x, g):
    ms = jnp.mean(x * x, axis=-1, keepdims=True)
    return x * lax.rsqrt(ms + EPS) * g


def _mod_row(mod_ref, k, rows, part=0, parts=1):
    nb, _, d = mod_ref.shape
    if nb == 1:
        return mod_ref[0, k:k + 1, :]
    nbp = nb // parts
    per = rows // nbp
    return jnp.concatenate([jnp.broadcast_to(mod_ref[b, k:k + 1, :], (per, d))
                            for b in range(part * nbp, (part + 1) * nbp)], axis=0)


def _mod_spec(t, boff, d, tile=None):
    tile = tile or ROW_TILE
    if t >= tile:
        assert t % tile == 0
        per = t // tile
        return pl.BlockSpec((1, 6, d), lambda i, *_: (boff + i // per, 0, 0))
    nb = tile // t
    assert tile % t == 0 and boff % nb == 0
    return pl.BlockSpec((nb, 6, d), lambda i, *_: (boff // nb + i, 0, 0))


def _ada_kernel(c_ref, w_ref, b_ref, o_ref):
    c = c_ref[...]
    s = c * jax.nn.sigmoid(c)
    o_ref[0] = jnp.dot(s, w_ref[0], precision=HIGHEST, preferred_element_type=F32) + b_ref[0]


def _ada(c_all, w_ada, b_ada):
    depth, d, n6 = w_ada.shape
    rows = c_all.shape[0]
    return pl.pallas_call(
        _ada_kernel,
        grid=(depth, n6 // ADA_COLS),
        in_specs=[
            pl.BlockSpec((rows, d), lambda l, j: (0, 0)),
            pl.BlockSpec((1, d, ADA_COLS), lambda l, j: (l, 0, j)),
            pl.BlockSpec((1, 1, ADA_COLS), lambda l, j: (l, 0, j)),
        ],
        out_specs=pl.BlockSpec((1, rows, ADA_COLS), lambda l, j: (l, 0, j)),
        out_shape=jax.ShapeDtypeStruct((depth, rows, n6), F32),
        compiler_params=_params(("arbitrary", "arbitrary")),
        name="ada",
    )(c_all, w_ada, b_ada.reshape(depth, 1, n6))


def _inproj_kernel(x_ref, mod_ref, g_ref, w_ref, o_ref):
    rows = x_ref.shape[0]
    hn = _rms(x_ref[...], g_ref[...]) * (1.0 + _mod_row(mod_ref, 1, rows)) + _mod_row(mod_ref, 0, rows)
    o_ref[...] = jnp.dot(hn.astype(BF16), w_ref[...], preferred_element_type=F32)


def _inproj(x, mod, boff, g1, w_in, t):
    n, d = x.shape
    cols = w_in.shape[1]
    tm = INPROJ_TILE
    assert n % tm == 0
    return pl.pallas_call(
        _inproj_kernel,
        grid=(n // tm,),
        in_specs=[
            pl.BlockSpec((tm, d), lambda i: (i, 0)),
            _mod_spec(t, boff, d, tm),
            pl.BlockSpec((1, d), lambda i: (0, 0)),
            pl.BlockSpec((d, cols), lambda i: (0, 0)),
        ],
        out_specs=pl.BlockSpec((tm, cols), lambda i: (i, 0)),
        out_shape=jax.ShapeDtypeStruct((n, cols), F32),
        compiler_params=_params(("arbitrary",)),
        name="inproj",
    )(x, mod, g1, w_in)


def _rglru_kernel(xr_ref, yr_ref, cst_ref, h0_ref, cw_ref, cb_ref, wr_ref, br_ref, wi_ref, bi_ref,
                  lam_ref, o_ref, cnew_ref, hlast_ref, tail_s, h_s, *, tt):
    j = pl.program_id(1)
    w = xr_ref.shape[1]

    @pl.when(j == 0)
    def _():
        tail_s[...] = jnp.zeros((V7X_SUBLANES, w), F32)
        tail_s[V7X_SUBLANES - (CONV_W - 1):, :] = cst_ref[0]
        h_s[...] = h0_ref[0]

    x = xr_ref[...]
    ext = jnp.concatenate([tail_s[...], x], axis=0)
    cw = cw_ref[...]
    xc = cb_ref[...] + x * cw[CONV_W - 1:CONV_W, :]
    for d in range(1, CONV_W):
        xs = pltpu.roll(ext, d, axis=0)[V7X_SUBLANES:, :]
        xc = xc + xs * cw[CONV_W - 1 - d:CONV_W - d, :]
    tail_s[...] = x[tt - V7X_SUBLANES:, :]
    cnew_ref[0] = x[tt - (CONV_W - 1):, :]

    xb = xc.astype(BF16)
    gr = jax.nn.sigmoid(jnp.dot(xb, wr_ref[...], preferred_element_type=F32) + br_ref[...])
    gi = jax.nn.sigmoid(jnp.dot(xb, wi_ref[...], preferred_element_type=F32) + bi_ref[...])
    nl = -lam_ref[...]
    sp = jnp.maximum(nl, 0.0) + jnp.log1p(jnp.exp(-jnp.abs(nl)))
    a = jnp.exp((-RG_C) * gr * sp)
    u = jnp.sqrt(1.0 - a * a) * gi * xc

    grp = V7X_SUBLANES
    ngrp = tt // grp
    a3 = a.reshape(ngrp, grp, w)
    u3 = u.reshape(ngrp, grp, w)
    sub = lax.broadcasted_iota(I32, (ngrp, grp, w), 1)
    d = 1
    while d < grp:
        keep = sub >= d
        u3 = jnp.where(keep, u3 + a3 * pltpu.roll(u3, d, axis=1), u3)
        a3 = jnp.where(keep, a3 * pltpu.roll(a3, d, axis=1), a3)
        d *= 2
    last = lambda x3: jnp.broadcast_to(x3[:, grp - 1:grp, :], x3.shape)
    a_end, u_end = last(a3), last(u3)
    carry = jnp.broadcast_to(h_s[...], (grp, w))
    h_groups = []
    for gi_ in range(ngrp):
        h_groups.append(u3[gi_] + a3[gi_] * carry)
        carry = u_end[gi_] + a_end[gi_] * carry
    h = jnp.concatenate(h_groups, axis=0)
    h_s[...] = carry[:1, :]
    hlast_ref[0] = carry[:1, :]
    o_ref[...] = (jax.nn.gelu(yr_ref[...], approximate=True) * h).astype(BF16)


def _rglru(proj, conv_st, h0, cw, cb, wr, br, wi, bi, lam, bsz, t):
    n = proj.shape[0]
    w = cw.shape[1]
    tt = min(t, ROW_TILE)
    nt = t // tt
    assert tt >= V7X_SUBLANES and t % tt == 0
    full = lambda shape: pl.BlockSpec(shape, lambda b, j: (0,) * len(shape))
    return pl.pallas_call(
        functools.partial(_rglru_kernel, tt=tt),
        grid=(bsz, nt),
        in_specs=[
            pl.BlockSpec((tt, w), lambda b, j: (b * nt + j, 0)),
            pl.BlockSpec((tt, w), lambda b, j: (b * nt + j, 1)),
            pl.BlockSpec((1, CONV_W - 1, w), lambda b, j: (b, 0, 0)),
            pl.BlockSpec((1, 1, w), lambda b, j: (b, 0, 0)),
            full((CONV_W, w)), full((1, w)), full((w, w)), full((1, w)), full((w, w)), full((1, w)),
            full((1, w)),
        ],
        out_specs=[
            pl.BlockSpec((tt, w), lambda b, j: (b * nt + j, 0)),
            pl.BlockSpec((1, CONV_W - 1, w), lambda b, j: (b, 0, 0)),
            pl.BlockSpec((1, 1, w), lambda b, j: (b, 0, 0)),
        ],
        out_shape=[
            jax.ShapeDtypeStruct((n, w), BF16),
            jax.ShapeDtypeStruct((bsz, CONV_W - 1, w), F32),
            jax.ShapeDtypeStruct((bsz, 1, w), F32),
        ],
        scratch_shapes=[pltpu.VMEM((V7X_SUBLANES, w), F32), pltpu.VMEM((1, w), F32)],
        compiler_params=_params(("arbitrary", "arbitrary")),
        name="rglru",
    )(proj, proj, conv_st, h0.reshape(bsz, 1, w), cw, cb, wr, br, wi, bi, lam)


def _hgrn_tables(c):
    t = jnp.arange(c)[:, None]
    r = jnp.arange(c)[None, :]
    tri = (r <= t).astype(BF16)
    bd = min(c, V7X_LANES)
    t, r = t[:bd, :bd], r[:bd, :bd]
    masks = [(t == r)]
    h = 1
    while 2 * h <= bd:
        m_t = (t // (2 * h)) * (2 * h) + h
        same = (t // (2 * h)) == (r // (2 * h))
        masks.append(same & (t >= m_t) & (r < m_t))
        h *= 2
    return tri, jnp.stack(masks).astype(F32)


def _hgrn_kernel(q_ref, f_ref, v_ref, g_ref, lb_ref, ng_ref, s0_ref, tri_ref, mask_ref,
                 o_ref, slast_ref, s_s, acc_s, g_s, *, c, nlev):
    j = pl.program_id(1)
    w = f_ref.shape[1]

    @pl.when(j == 0)
    def _():
        s_s[...] = s0_ref[0]

    ff = f_ref[...]
    lb = lb_ref[...]
    log_sig = -(jnp.maximum(-ff, 0.0) + jnp.log1p(jnp.exp(-jnp.abs(ff))))
    log_f = log_sig + jnp.log1p(lb * jnp.exp(-ff))
    k = (1.0 - lb) * jax.nn.sigmoid(-ff)
    q = q_ref[...]

    hi = log_f.astype(BF16)
    r1 = log_f - hi.astype(F32)
    mid = r1.astype(BF16)
    lo = (r1 - mid.astype(F32)).astype(BF16)
    tri = tri_ref[...]
    gcum = (jnp.dot(tri, hi, preferred_element_type=F32) + jnp.dot(tri, mid, preferred_element_type=F32)
            + jnp.dot(tri, lo, preferred_element_type=F32))
    gcum = gcum * LOG2_E
    g_s[...] = gcum

    row = lax.broadcasted_iota(I32, (c, w), 0)
    grp = V7X_SUBLANES
    groups = gcum.reshape(c // grp, grp, w)

    def group_row(r):
        return jnp.broadcast_to(groups[:, r:r + 1, :], groups.shape).reshape(c, w)

    def level_exponent(h):
        if h >= grp:
            parts = []
            for b in range(c // (2 * h)):
                lo0, m, hi0 = b * 2 * h, b * 2 * h + h, (b + 1) * 2 * h
                g_mid = g_s[m - 1:m, :]
                parts += [g_mid - gcum[lo0:m, :], gcum[m:hi0, :] - g_mid]
            return jnp.concatenate(parts, axis=0)
        if h == 1:
            g_mid = jnp.where((row & 1) == 1, pltpu.roll(gcum, 1, axis=0), gcum)
        else:
            g_mid = group_row(h - 1)
            for b in range(1, grp // (2 * h)):
                g_mid = jnp.where((row & (grp - 1)) >= b * 2 * h, group_row(b * 2 * h + h - 1), g_mid)
        return -jnp.abs(gcum - g_mid)

    bd = mask_ref.shape[1]

    def pair_scores(l, qw, kw):
        h = 0 if l == 0 else 2 ** (l - 1)
        keep = mask_ref[l] > 0.0 if 2 * h <= bd else None
        for hd in range(HG_HEADS):
            sl = slice(hd * HG_D, (hd + 1) * HG_D)
            if 2 * h <= bd:
                for b in range(c // bd):
                    rs = slice(b * bd, (b + 1) * bd)
                    p = lax.dot_general(qw[rs, sl], kw[rs, sl], NT_DIMS, preferred_element_type=F32)
                    acc_s[hd, rs, rs] = jnp.where(keep, p, 0.0 if l == 0 else acc_s[hd, rs, rs])
            else:
                for b in range(c // (2 * h)):
                    rows = slice(b * 2 * h + h, (b + 1) * 2 * h)
                    cols = slice(b * 2 * h, b * 2 * h + h)
                    acc_s[hd, rows, cols] = lax.dot_general(qw[rows, sl], kw[cols, sl], NT_DIMS,
                                                            preferred_element_type=F32)

    if c > bd:
        acc_s[...] = jnp.zeros(acc_s.shape, F32)
    qb = q.astype(BF16)
    kb = k.astype(BF16)
    pair_scores(0, qb, kb)
    for l in range(1, nlev + 1):
        wgt = jnp.exp2(level_exponent(2 ** (l - 1))).astype(BF16)
        pair_scores(l, qb * wgt, kb * wgt)

    glast = gcum[c - 1:, :]
    qg = (q * jnp.exp2(gcum)).astype(BF16)
    kdec = (k * jnp.exp2(glast - gcum)).astype(BF16)
    dlast = jnp.exp2(glast)
    vb = v_ref[...].astype(BF16)
    for hd in range(HG_HEADS):
        sl = slice(hd * HG_D, (hd + 1) * HG_D)
        s_old = s_s[hd]
        o = (jnp.dot(acc_s[hd].astype(BF16), vb[:, sl], preferred_element_type=F32)
             + jnp.dot(qg[:, sl], s_old.astype(BF16), preferred_element_type=F32))
        kv = lax.dot_general(kdec[:, sl], vb[:, sl], TN_DIMS, preferred_element_type=F32)
        dcol = jnp.broadcast_to(dlast[:, sl], (HG_D, HG_D)).T
        s_s[hd] = dcol * s_old + kv
        gate = g_ref[:, sl]
        o_ref[:, sl] = (_rms(o, ng_ref[...]) * (gate * jax.nn.sigmoid(gate))).astype(BF16)
    slast_ref[0] = s_s[...]


def _hgrn(proj, lb, ng, s0, bsz, t):
    n = proj.shape[0]
    w = HG_HEADS * HG_D
    c = min(t, HGRN_CHUNK)
    nt = t // c
    nlev = int(math.log2(c))
    assert 2 ** nlev == c and t % c == 0
    tri, masks = _hgrn_tables(c)
    col = lambda k: pl.BlockSpec((c, w), lambda b, j: (b * nt + j, k))
    full = lambda shape: pl.BlockSpec(shape, lambda b, j: (0,) * len(shape))
    return pl.pallas_call(
        functools.partial(_hgrn_kernel, c=c, nlev=nlev),
        grid=(bsz, nt),
        in_specs=[
            col(2), col(3), col(4), col(5),
            full((1, w)), full((1, HG_D)),
            pl.BlockSpec((1, HG_HEADS, HG_D, HG_D), lambda b, j: (b, 0, 0, 0)),
            full(tri.shape), full(masks.shape),
        ],
        out_specs=[
            pl.BlockSpec((c, w), lambda b, j: (b * nt + j, 0)),
            pl.BlockSpec((1, HG_HEADS, HG_D, HG_D), lambda b, j: (b, 0, 0, 0)),
        ],
        out_shape=[
            jax.ShapeDtypeStruct((n, w), BF16),
            jax.ShapeDtypeStruct((bsz, HG_HEADS, HG_D, HG_D), F32),
        ],
        scratch_shapes=[pltpu.VMEM((HG_HEADS, HG_D, HG_D), F32),
                        pltpu.VMEM((HG_HEADS, c, c), F32),
                        pltpu.VMEM((c, w), F32)],
        compiler_params=_params(("arbitrary", "arbitrary")),
        name="hgrn",
    )(proj, proj, proj, proj, lb, ng, s0, tri, masks)


def _outproj_kernel(org_ref, ohg_ref, x_ref, mod_ref, wo_ref, g2_ref, wr_ref, brt_ref, tri_ref,
                    low_ref, cin_ref, x1_ref, hb_ref, pos_ref, prob_ref, segb_ref, segl_ref, cnt_ref,
                    carry_s, *, tm, parts):
    @pl.when(pl.program_id(0) == 0)
    def _():
        carry_s[...] = cin_ref[...]

    carry = carry_s[...]
    for part in range(parts):
        carry = _outproj_part(part, parts, tm, carry, org_ref, ohg_ref, x_ref, mod_ref, wo_ref,
                              g2_ref, wr_ref, brt_ref, tri_ref, low_ref, x1_ref, hb_ref, pos_ref,
                              prob_ref, segb_ref, segl_ref)
    carry_s[...] = carry
    cnt_ref[...] = carry


def _outproj_part(part, parts, tm, carry, org_ref, ohg_ref, x_ref, mod_ref, wo_ref, g2_ref, wr_ref,
                  brt_ref, tri_ref, low_ref, x1_ref, hb_ref, pos_ref, prob_ref, segb_ref, segl_ref):
    rs = slice(part * tm, (part + 1) * tm)
    mrow = lambda k: _mod_row(mod_ref, k, tm, part, parts)
    half = org_ref.shape[1]
    mix = (jnp.dot(org_ref[rs, :], wo_ref[:half, :], preferred_element_type=F32)
           + jnp.dot(ohg_ref[rs, :], wo_ref[half:, :], preferred_element_type=F32))
    x1 = x_ref[rs, :] + mrow(2) * mix
    x1_ref[rs, :] = x1
    hn = _rms(x1, g2_ref[...]) * (1.0 + mrow(4)) + mrow(3)
    hn_hi = hn.astype(BF16)
    hb_ref[rs, :] = hn_hi

    ne = brt_ref.shape[0]
    hn_lo = (hn - hn_hi.astype(F32)).astype(BF16)
    both = jnp.dot(hn_hi, wr_ref[...], preferred_element_type=F32)
    logits_tm = (both[:, :V7X_LANES] + both[:, V7X_LANES:]
                 + jnp.dot(hn_lo, wr_ref[:, :V7X_LANES], preferred_element_type=F32))
    logits = logits_tm.T[:ne, :] + brt_ref[...]
    eio = lax.broadcasted_iota(I32, (ne, tm), 0)
    cur = logits
    vals, sels = [], []
    for _ in range(TOP_K):
        mx = jnp.max(cur, axis=0, keepdims=True)
        ix = jnp.min(jnp.where(cur == mx, eio, ne), axis=0, keepdims=True)
        sel = eio == ix
        vals.append(mx)
        sels.append(sel)
        cur = jnp.where(sel, -jnp.inf, cur)
    ex = [jnp.exp(v - vals[0]) for v in vals]
    den = ex[0] + ex[1] + ex[2] + ex[3]
    probs = [e / den for e in ex]

    onehot = sels[0].astype(F32)
    for sel in sels[1:]:
        onehot = onehot + sel.astype(F32)
    incl = jnp.dot(onehot.astype(BF16), tri_ref[...], preferred_element_type=F32)
    run_len = jnp.ceil(incl[:, tm - 1:] * (1.0 / RUN_ALIGN)) * RUN_ALIGN
    run0 = jnp.dot(low_ref[...], jnp.broadcast_to(run_len, (ne, tm)).astype(BF16),
                   preferred_element_type=F32)
    local = run0 + incl - onehot
    slots = [jnp.sum(jnp.where(sel, local, 0.0), axis=0, keepdims=True) for sel in sels]
    segb_ref[part] = carry
    segl_ref[part] = run_len

    pos_ref[:, rs] = jnp.concatenate(slots, axis=0).astype(I32)
    sub = lax.broadcasted_iota(I32, (V7X_LANES, tm), 0)
    pt = jnp.zeros((V7X_LANES, tm), F32)
    for kk in range(TOP_K):
        pt = jnp.where(sub == kk, probs[kk], pt)
        pt = jnp.where(sub == TOP_K + kk, slots[kk], pt)
    prob_ref[rs, :] = pt.T
    return carry + run_len


def _router_weights(w_router):
    d, ne = w_router.shape
    assert ne <= V7X_LANES
    hi = w_router.astype(BF16)
    lo = (w_router - hi.astype(F32)).astype(BF16)
    pad = jnp.zeros((d, V7X_LANES - ne), BF16)
    return jnp.concatenate([hi, pad, lo, pad], axis=1)


def _outproj(o_rg, o_hg, x, mod, boff, w_out, g2, wr, brt, cnt_in, t):
    n, d = x.shape
    half = o_rg.shape[1]
    ne = brt.shape[0]
    tm = ROW_TILE
    parts = OUTPROJ_PARTS
    tile = tm * parts
    assert n % tile == 0 and tm <= 256
    nt = n // tm
    tri = (jnp.arange(tm)[:, None] <= jnp.arange(tm)[None, :]).astype(BF16)
    low = (jnp.arange(ne)[:, None] > jnp.arange(ne)[None, :]).astype(BF16)
    rowblk = lambda cols: pl.BlockSpec((tile, cols), lambda i: (i, 0))
    full = lambda shape: pl.BlockSpec(shape, lambda i: (0,) * len(shape))
    pertile = lambda: pl.BlockSpec((parts, ne, 1), lambda i: (i, 0, 0))
    return pl.pallas_call(
        functools.partial(_outproj_kernel, tm=tm, parts=parts),
        grid=(n // tile,),
        in_specs=[
            rowblk(half), rowblk(half), rowblk(d),
            _mod_spec(t, boff, d, tile),
            full((d, d)), full((1, d)), full(wr.shape), full((ne, 1)), full((tm, tm)), full((ne, ne)),
            full((ne, 1)),
        ],
        out_specs=[
            rowblk(d), rowblk(d), pl.BlockSpec((TOP_K, tile), lambda i: (0, i)), rowblk(V7X_LANES),
            pertile(), pertile(), full((ne, 1)),
        ],
        out_shape=[
            jax.ShapeDtypeStruct((n, d), F32),
            jax.ShapeDtypeStruct((n, d), BF16),
            jax.ShapeDtypeStruct((TOP_K, n), I32),
            jax.ShapeDtypeStruct((n, V7X_LANES), F32),
            jax.ShapeDtypeStruct((nt, ne, 1), F32),
            jax.ShapeDtypeStruct((nt, ne, 1), F32),
            jax.ShapeDtypeStruct((ne, 1), F32),
        ],
        scratch_shapes=[pltpu.VMEM((ne, 1), F32)],
        compiler_params=_params(("arbitrary",)),
        name="outproj",
    )(o_rg, o_hg, x, mod, w_out, g2, wr, brt, tri, low, cnt_in)


def _pow2_pieces(hi, lo=1):
    return tuple(1 << b for b in range(int(math.log2(hi)), int(math.log2(lo)) - 1, -1))


def _tile_slots(ne):
    worst = TOP_K * ROW_TILE + ne * (RUN_ALIGN - 1)
    return -(-worst // V7X_LANES) * V7X_LANES


def _stored_rows(ref, first_row, rows):
    start = first_row * ROW_PARTS
    if not isinstance(start, int):
        start = pl.multiple_of(start, V7X_SUBLANES)
    return ref.at[pl.ds(start, rows * ROW_PARTS), :]


def _store_packed(ref, words):
    rows = words.shape[0]
    for c in range(ROW_PARTS):
        ref[pl.ds(c, rows, stride=ROW_PARTS), :] = words[:, c * V7X_LANES:(c + 1) * V7X_LANES]


def _load_packed(ref):
    rows = ref.shape[0] // ROW_PARTS
    return jnp.concatenate([ref[pl.ds(c, rows, stride=ROW_PARTS), :] for c in range(ROW_PARTS)], axis=1)


def _copy_run(n, make_copy, wait, pieces):
    for rows in pieces:
        done = n & (-2 * rows)

        @pl.when((n & rows) != 0)
        def _():
            cp = make_copy(done, rows)
            cp.wait() if wait else cp.start()


RUN_PIECES = _pow2_pieces(ROW_TILE, RUN_ALIGN)
BUF_ROW_BITS = 11
LIST_HEAD = V7X_SUBLANES


def _run_lists(seg_start, seg_len, ne):
    assert _tile_slots(ne) <= 1 << BUF_ROW_BITS
    tiles = seg_len.shape[0]
    buf0 = jnp.cumsum(seg_len, axis=1) - seg_len
    j = jnp.arange(ne, dtype=I32)
    counts, entries = [], []
    for rows in RUN_PIECES:
        has = (seg_len & rows) != 0
        done = seg_len & (-2 * rows)
        packed = ((seg_start + done) << BUF_ROW_BITS) | (buf0 + done)
        rank = jnp.cumsum(has.astype(I32), axis=1) - 1
        place = has[:, :, None] & (rank[:, :, None] == j[None, None, :])
        entries.append(jnp.sum(jnp.where(place, packed[:, :, None], 0), axis=1))
        counts.append(jnp.sum(has.astype(I32), axis=1))
    head = jnp.stack(counts + [jnp.zeros((tiles,), I32)] * (LIST_HEAD - len(RUN_PIECES)), axis=1)
    flat = jnp.concatenate([head] + entries, axis=1)
    width = -(-flat.shape[1] // V7X_LANES) * V7X_LANES
    flat = jnp.pad(flat, ((0, 0), (0, width - flat.shape[1])))
    return flat.reshape(tiles, 1, width)


def _list_spec(ne, index_map):
    width = -(-(LIST_HEAD + len(RUN_PIECES) * ne) // V7X_LANES) * V7X_LANES
    return pl.BlockSpec((1, 1, width), index_map, memory_space=pltpu.SMEM)


def _tile_runs(list_ref, ne, make_copy, wait):
    for pi, rows in enumerate(RUN_PIECES):
        def one(jj, pi=pi, rows=rows):
            v = list_ref[0, 0, LIST_HEAD + pi * ne + jj]
            cp = make_copy(v >> BUF_ROW_BITS, v & ((1 << BUF_ROW_BITS) - 1), rows)
            cp.wait() if wait else cp.start()

        def pair(jj, carry, one=one):
            one(2 * jj)
            one(2 * jj + 1)
            return carry

        cnt = list_ref[0, 0, pi]
        lax.fori_loop(0, cnt >> 1, pair, 0)
        pl.when((cnt & 1) != 0)(functools.partial(one, cnt - 1))


def _pack_pairs(lo_cols, hi_cols):
    a = lax.bitcast_convert_type(lo_cols, jnp.uint32)
    b = lax.bitcast_convert_type(hi_cols, jnp.uint32)
    return (a & jnp.uint32(0xFFFF0000)) | (b >> 16)


def _unpack_pairs(words):
    a = lax.bitcast_convert_type(words & jnp.uint32(0xFFFF0000), F32).astype(BF16)
    b = lax.bitcast_convert_type(words << 16, F32).astype(BF16)
    return a, b


def _dispatch_kernel(zoff_ref, zcnt_ref, lst_ref, lst1_ref, lst2_ref, pos_ref, hba_ref, hbb_ref,
                     out_ref, buf_s, zero_s, sem, zsem, *, ne, tiles_a):
    i = pl.program_id(0)
    last = pl.num_programs(0) - 1
    slot = i % 2
    tt, d = hba_ref.shape
    dh = d // 2

    def runs(list_ref, s, wait):
        _tile_runs(list_ref, ne,
                   lambda g, b, rows: pltpu.make_async_copy(
                       _stored_rows(buf_s.at[s], b, rows), _stored_rows(out_ref, g, rows), sem.at[s]),
                   wait)

    @pl.when(i >= 2)
    def _():
        runs(lst2_ref, slot, True)

    pos = pos_ref[...]
    slot_row = lax.broadcasted_iota(I32, (buf_s.shape[1] // ROW_PARTS, tt), 0)
    hit = slot_row == pos[0:1, :]
    for kk in range(1, TOP_K):
        hit = hit | (slot_row == pos[kk:kk + 1, :])
    perm = jnp.where(hit, 1.0, 0.0).astype(BF16)
    hb = jnp.where(i < tiles_a, hba_ref[...], hbb_ref[...])
    _store_packed(buf_s.at[slot], _pack_pairs(jnp.dot(perm, hb[:, :dh], preferred_element_type=F32),
                                              jnp.dot(perm, hb[:, dh:], preferred_element_type=F32)))
    runs(lst_ref, slot, False)

    @pl.when(i == last)
    def _():
        @pl.when(i >= 1)
        def _():
            runs(lst1_ref, 1 - slot, True)
        runs(lst_ref, slot, True)

        zero_s[...] = jnp.zeros(zero_s.shape, zero_s.dtype)
        zrows = zero_s.shape[0] // ROW_PARTS
        zcopy = lambda g, rows: pltpu.make_async_copy(
            _stored_rows(zero_s, 0, rows), _stored_rows(out_ref, g, rows), zsem)

        def zero_run(e, carry):
            lo = zoff_ref[e] + zcnt_ref[e]
            n = zoff_ref[e + 1] - lo
            nfull = n // zrows
            rem = n - nfull * zrows
            for wait in (False, True):
                def full(c, c2):
                    cp = zcopy(lo + c * zrows, zrows)
                    cp.wait() if wait else cp.start()
                    return c2
                lax.fori_loop(0, nfull, full, 0)
                _copy_run(rem, lambda o, rows: zcopy(lo + nfull * zrows + o, rows), wait,
                          _pow2_pieces(zrows // 2, RUN_ALIGN))
            return carry

        lax.fori_loop(0, zcnt_ref.shape[0], zero_run, 0)


def _dispatch(lists, zoff, zcnt, pos_t, hb_a, hb_b, n_rows, ne):
    d = hb_a.shape[1]
    tt = ROW_TILE
    ta, tb = hb_a.shape[0] // tt, hb_b.shape[0] // tt
    assert hb_a.shape[0] % tt == 0 and hb_b.shape[0] % tt == 0
    back = lambda k: _list_spec(ne, lambda i, *_: (jnp.maximum(i - k, 0), 0, 0))
    return pl.pallas_call(
        functools.partial(_dispatch_kernel, ne=ne, tiles_a=ta),
        grid_spec=pltpu.PrefetchScalarGridSpec(
            num_scalar_prefetch=2,
            grid=(ta + tb,),
            in_specs=[back(0), back(1), back(2),
                      pl.BlockSpec((TOP_K, tt), lambda i, *_: (0, i)),
                      pl.BlockSpec((tt, d), lambda i, *_: (jnp.minimum(i, ta - 1), 0)),
                      pl.BlockSpec((tt, d), lambda i, *_: (jnp.maximum(i - ta, 0), 0))],
            out_specs=pl.BlockSpec(memory_space=pl.ANY),
            scratch_shapes=[pltpu.VMEM((2, _tile_slots(ne) * ROW_PARTS, V7X_LANES), jnp.uint32),
                            pltpu.VMEM((tt * ROW_PARTS, V7X_LANES), jnp.uint32),
                            pltpu.SemaphoreType.DMA((2,)), pltpu.SemaphoreType.DMA(())],
        ),
        out_shape=jax.ShapeDtypeStruct((n_rows * ROW_PARTS, V7X_LANES), jnp.uint32),
        compiler_params=_params(("arbitrary",)),
        name="dispatch",
    )(zoff, zcnt, lists, lists, lists, pos_t, hb_a, hb_b)


def _ffn_kernel(te_ref, nu_ref, xs_ref, wg_ref, bg_ref, wu_ref, bu_ref, wd_ref, bd_ref, y_ref,
                wg_s, wu_s, wd_s):
    i = pl.program_id(0)
    valid = i < nu_ref[0]
    changed = (i == 0) | (te_ref[i] != te_ref[jnp.maximum(i - 1, 0)])

    @pl.when(valid & changed)
    def _():
        wg_s[...] = wg_ref[...].astype(BF16)
        wu_s[...] = wu_ref[...].astype(BF16)
        wd_s[...] = wd_ref[...].astype(BF16)

    @pl.when(valid)
    def _():
        dh = ROW_PARTS * xs_ref.shape[1]
        xa, xb = _unpack_pairs(_load_packed(xs_ref))
        gate = (jnp.dot(xa, wg_s[:dh, :], preferred_element_type=F32)
                + jnp.dot(xb, wg_s[dh:, :], preferred_element_type=F32) + bg_ref[...])
        up = (jnp.dot(xa, wu_s[:dh, :], preferred_element_type=F32)
              + jnp.dot(xb, wu_s[dh:, :], preferred_element_type=F32) + bu_ref[...])
        gate = jnp.minimum(gate, SWIGLU_LIMIT)
        up = jnp.clip(up, -SWIGLU_LIMIT, SWIGLU_LIMIT)
        glu = gate * jax.nn.sigmoid(SWIGLU_ALPHA * gate)
        act = ((up + 1.0) * glu).astype(BF16)
        y = (jnp.dot(act, wd_s[...], preferred_element_type=F32) + bd_ref[...]).astype(BF16).astype(F32)
        yh = y.shape[1] // 2
        _store_packed(y_ref, _pack_pairs(y[:, :yh], y[:, yh:]))

    @pl.when(jnp.logical_not(valid))
    def _():
        y_ref[...] = jnp.zeros(y_ref.shape, y_ref.dtype)


def _ffn(layer, tile_expert, n_used, xs, w_gate, b_gate, w_up, b_up, w_down, b_down):
    d, dff = w_gate.shape[2], w_gate.shape[3]
    tm = FFN_TILE * ROW_PARTS
    assert xs.shape[1] == V7X_LANES and xs.shape[0] % tm == 0
    wspec = lambda r, c: pl.BlockSpec((None, None, r, c), lambda i, te, nu: (layer, te[i], 0, 0))
    return pl.pallas_call(
        _ffn_kernel,
        grid_spec=pltpu.PrefetchScalarGridSpec(
            num_scalar_prefetch=2,
            grid=(xs.shape[0] // tm,),
            in_specs=[
                pl.BlockSpec((tm, V7X_LANES), lambda i, te, nu: (i, 0)),
                wspec(d, dff), wspec(1, dff), wspec(d, dff), wspec(1, dff), wspec(dff, d), wspec(1, d),
            ],
            out_specs=pl.BlockSpec((tm, V7X_LANES), lambda i, te, nu: (i, 0)),
            scratch_shapes=[pltpu.VMEM((d, dff), BF16), pltpu.VMEM((d, dff), BF16),
                            pltpu.VMEM((dff, d), BF16)],
        ),
        out_shape=jax.ShapeDtypeStruct(xs.shape, jnp.uint32),
        compiler_params=_params(("arbitrary",)),
        name="ffn",
    )(tile_expert, n_used, xs, w_gate, b_gate, w_up, b_up, w_down, b_down)


def _combine_kernel(lst_ref, nxt_ref, y_ref, prob_ref, x1_ref, mod_ref, gf_ref, o_ref,
                    buf_s, sem, *, ne, final):
    i = pl.program_id(0)
    slot = i % 2
    tt = x1_ref.shape[0]

    def runs(list_ref, s, wait):
        _tile_runs(list_ref, ne,
                   lambda g, b, rows: pltpu.make_async_copy(
                       _stored_rows(y_ref, g, rows), _stored_rows(buf_s.at[s], b, rows), sem.at[s]),
                   wait)

    @pl.when(i == 0)
    def _():
        buf_s[...] = jnp.zeros(buf_s.shape, buf_s.dtype)
        runs(lst_ref, slot, False)

    @pl.when(i + 1 < pl.num_programs(0))
    def _():
        runs(nxt_ref, 1 - slot, False)

    runs(lst_ref, slot, True)

    ya, yb = _unpack_pairs(_load_packed(buf_s.at[slot]))
    pp = prob_ref[...]
    slots = buf_s.shape[1] // ROW_PARTS
    slot_col = lax.broadcasted_iota(I32, (tt, slots), 1)
    wgt = jnp.zeros((tt, slots), F32)
    for kk in range(TOP_K):
        pos = pp[:, TOP_K + kk:TOP_K + kk + 1].astype(I32)
        wgt = jnp.where(slot_col == pos, pp[:, kk:kk + 1], wgt)
    wgt = wgt.astype(BF16)
    y = jnp.concatenate([jnp.dot(wgt, ya, preferred_element_type=F32),
                         jnp.dot(wgt, yb, preferred_element_type=F32)], axis=1)
    x2 = x1_ref[...] + _mod_row(mod_ref, 5, tt) * y
    if final:
        x2 = _rms(x2, gf_ref[...])
    o_ref[...] = x2


def _combine(lists, y_rows, prob, x1, mod, boff, g_final, roff, t, ne, final):
    n, d = x1.shape
    tt = ROW_TILE
    r0 = roff // tt
    assert roff % tt == 0 and n % tt == 0
    last = r0 + n // tt - 1
    return pl.pallas_call(
        functools.partial(_combine_kernel, ne=ne, final=final),
        grid=(n // tt,),
        in_specs=[
            _list_spec(ne, lambda i: (r0 + i, 0, 0)),
            _list_spec(ne, lambda i: (jnp.minimum(r0 + i + 1, last), 0, 0)),
            pl.BlockSpec(memory_space=pl.ANY),
            pl.BlockSpec((tt, V7X_LANES), lambda i: (i, 0)),
            pl.BlockSpec((tt, d), lambda i: (i, 0)),
            _mod_spec(t, boff, d),
            pl.BlockSpec((1, d), lambda i: (0, 0)),
        ],
        out_specs=pl.BlockSpec((tt, d), lambda i: (i, 0)),
        scratch_shapes=[pltpu.VMEM((2, _tile_slots(ne) * ROW_PARTS, V7X_LANES), jnp.uint32),
                        pltpu.SemaphoreType.DMA((2,))],
        out_shape=jax.ShapeDtypeStruct((n, d), F32),
        compiler_params=_params(("arbitrary",)),
        name="combine",
    )(lists, lists, y_rows, prob, x1, mod, g_final)


def _block_diag(w):
    depth, nb, c, _ = w.shape
    eye = jnp.eye(nb, dtype=w.dtype)
    return (eye[None, :, None, :, None] * w[:, :, :, None, :]).reshape(depth, nb * c, nb * c)


def kernel(x_prompt, x_sample, state_conv, state_rglru, state_hgrn, c_prompt, c_sample, w_ada, b_ada, g_norm1, g_norm2, w_in, conv_w, conv_b, w_rgate, b_rgate, w_igate, b_igate, lru_lambda, hgrn_lb, hgrn_norm_g, w_out, w_router, b_router, w_gate, b_gate, w_up, b_up, w_down, b_down, g_final):
    depth = w_ada.shape[0]
    bp, tp, d = x_prompt.shape
    bs, ts, _ = x_sample.shape
    w_rg = conv_w.shape[2]
    ne = w_router.shape[2]
    n_p, n_s = bp * tp, bs * ts
    n_tot = n_p + n_s
    span = max(max(ROW_TILE * OUTPROJ_PARTS, INPROJ_TILE) // ts, 1)
    boff_s = -(-bp // span) * span
    groups = ((bp, tp, 0, 0), (bs, ts, boff_s, n_p))
    n_c_pad = -(-(boff_s + bs) // V7X_SUBLANES) * V7X_SUBLANES
    c_all = jnp.concatenate([c_prompt, jnp.zeros((boff_s - bp, d), F32), c_sample,
                             jnp.zeros((n_c_pad - boff_s - bs, d), F32)], axis=0)
    mod_all = _ada(c_all, w_ada, b_ada).reshape(depth, n_c_pad, 6, d)

    p_lb = jax.nn.softmax(hgrn_lb.astype(F32), axis=0)
    lb_all = jnp.cumsum(p_lb, axis=0) - p_lb[0]

    w_in_b = w_in.astype(BF16)
    w_out_b = w_out.astype(BF16)
    wr_all = _block_diag(w_rgate).astype(BF16)
    wi_all = _block_diag(w_igate).astype(BF16)

    assert d // 2 == ROW_PARTS * V7X_LANES
    run_pad = (n_tot // ROW_TILE) * ne * (RUN_ALIGN - 1)
    n_tiles = -(-(TOP_K * n_tot + run_pad + ne * (FFN_TILE - 1)) // FFN_TILE)
    n_rows = n_tiles * FFN_TILE

    xs = [x_prompt.reshape(n_p, d), x_sample.reshape(n_s, d)]
    conv_in = [jnp.zeros((depth, bp, CONV_W - 1, w_rg), F32), state_conv]
    h_in = [jnp.zeros((depth, bp, w_rg), F32), state_rglru]
    s_in = [jnp.zeros((depth, bp, HG_HEADS, HG_D, HG_D), F32), state_hgrn]
    conv_out, h_out, s_out = ([], []), ([], []), ([], [])

    for l in range(depth):
        mod = mod_all[l]
        wr_d, wi_d = wr_all[l], wi_all[l]
        cnt = jnp.zeros((ne, 1), F32)
        x1s, hbs, poss, probs, segbs, segls = [], [], [], [], [], []
        for gi, (bsz, t, boff, _) in enumerate(groups):
            proj = _inproj(xs[gi], mod, boff, g_norm1[l][None], w_in_b[l], t)
            o_rg, c_new, h_last = _rglru(
                proj, conv_in[gi][l], h_in[gi][l], conv_w[l], conv_b[l][None], wr_d, b_rgate[l][None],
                wi_d, b_igate[l][None], lru_lambda[l][None], bsz, t)
            o_hg, s_last = _hgrn(proj, lb_all[l][None], hgrn_norm_g[l][None], s_in[gi][l], bsz, t)
            x1, hb, pos_t, prob, segb, segl, cnt = _outproj(
                o_rg, o_hg, xs[gi], mod, boff, w_out_b[l], g_norm2[l][None], _router_weights(w_router[l]),
                b_router[l][:, None], cnt, t)
            conv_out[gi].append(c_new)
            h_out[gi].append(h_last[:, 0])
            s_out[gi].append(s_last)
            x1s.append(x1)
            hbs.append(hb)
            poss.append(pos_t)
            probs.append(prob)
            segbs.append(segb[:, :, 0])
            segls.append(segl[:, :, 0])

        counts = cnt[:, 0].astype(I32)
        padded = (counts + FFN_TILE - 1) // FFN_TILE * FFN_TILE
        pad_end = jnp.cumsum(padded)
        off = jnp.concatenate([jnp.zeros((1,), I32), pad_end]).astype(I32)
        n_used = (pad_end[-1] // FFN_TILE).astype(I32)
        tile_start = jnp.minimum(jnp.arange(n_tiles, dtype=I32), n_used - 1) * FFN_TILE
        tile_expert = jnp.minimum(jnp.sum(tile_start[:, None] >= pad_end[None, :], axis=1), ne - 1).astype(I32)
        lists = _run_lists(off[None, :ne] + jnp.concatenate(segbs, axis=0).astype(I32),
                           jnp.concatenate(segls, axis=0).astype(I32), ne)
        zoff = jnp.concatenate([off, jnp.full((1,), n_rows, I32)])
        zcnt = jnp.concatenate([counts, jnp.zeros((1,), I32)])

        x_sorted = _dispatch(lists, zoff, zcnt, jnp.concatenate(poss, axis=1), hbs[0], hbs[1],
                             n_rows, ne)
        y_rows = _ffn(l, tile_expert, n_used.reshape(1), x_sorted, w_gate, b_gate[:, :, None, :],
                      w_up, b_up[:, :, None, :], w_down, b_down[:, :, None, :])
        final = l == depth - 1
        xs = [_combine(lists, y_rows, probs[gi], x1s[gi], mod, boff, g_final[None],
                       roff, t, ne, final)
              for gi, (bsz, t, boff, roff) in enumerate(groups)]

    return (xs[0].reshape(bp, tp, d), xs[1].reshape(bs, ts, d),
            jnp.stack(conv_out[0]), jnp.stack(h_out[0]), jnp.stack(s_out[0]),
            jnp.stack(conv_out[1]), jnp.stack(h_out[1]), jnp.stack(s_out[1]))
```

```python
import functools
import math

import jax
import jax.numpy as jnp
from jax import lax
from jax.experimental import pallas as pl
from jax.experimental.pallas import tpu as pltpu

F32 = jnp.float32
BF16 = jnp.bfloat16
I32 = jnp.int32
HIGHEST = lax.Precision.HIGHEST

EPS = 1e-6
RG_C = 8.0
CONV_W = 4
HG_HEADS = 4
HG_D = 128
TOP_K = 4
SWIGLU_LIMIT = 7.0
SWIGLU_ALPHA = 1.702
LOG2_E = 1.4426950408889634

V7X_LANES = 128
V7X_SUBLANES = 8
V7X_VMEM_LIMIT_BYTES = 56 * 1024 * 1024

ROW_TILE = 256
INPROJ_TILE = 512
HGRN_CHUNK = 256
SEQ_TILES_PER_STEP = 2
OUTPROJ_PARTS = 2
FFN_TILE = 512
ROW_PARTS = 4
RUN_ALIGN = V7X_SUBLANES // ROW_PARTS
ADA_COLS = 1536

NT_DIMS = (((1,), (1,)), ((), ()))
TN_DIMS = (((0,), (0,)), ((), ()))


def _params(sem, vmem=V7X_VMEM_LIMIT_BYTES):
    return pltpu.CompilerParams(dimension_semantics=sem, vmem_limit_bytes=vmem)


def _rms(x, g):
    ms = jnp.mean(x * x, axis=-1, keepdims=True)
    return x * lax.rsqrt(ms + EPS) * g


def _mod_row(mod_ref, k, rows, part=0, parts=1):
    nb, _, d = mod_ref.shape
    if nb == 1:
        return mod_ref[0, k:k + 1, :]
    nbp = nb // parts
    per = rows // nbp
    return jnp.concatenate([jnp.broadcast_to(mod_ref[b, k:k + 1, :], (per, d))
                            for b in range(part * nbp, (part + 1) * nbp)], axis=0)


def _mod_spec(t, boff, d, tile=None):
    tile = tile or ROW_TILE
    if t >= tile:
        assert t % tile == 0
        per = t // tile
        return pl.BlockSpec((1, 6, d), lambda i, *_: (boff + i // per, 0, 0))
    nb = tile // t
    assert tile % t == 0 and boff % nb == 0
    return pl.BlockSpec((nb, 6, d), lambda i, *_: (boff // nb + i, 0, 0))


def _ada_kernel(c_ref, w_ref, b_ref, o_ref):
    c = c_ref[...]
    s = c * jax.nn.sigmoid(c)
    o_ref[0] = jnp.dot(s, w_ref[0], precision=HIGHEST, preferred_element_type=F32) + b_ref[0]


def _ada(c_all, w_ada, b_ada):
    depth, d, n6 = w_ada.shape
    rows = c_all.shape[0]
    return pl.pallas_call(
        _ada_kernel,
        grid=(depth, n6 // ADA_COLS),
        in_specs=[
            pl.BlockSpec((rows, d), lambda l, j: (0, 0)),
            pl.BlockSpec((1, d, ADA_COLS), lambda l, j: (l, 0, j)),
            pl.BlockSpec((1, 1, ADA_COLS), lambda l, j: (l, 0, j)),
        ],
        out_specs=pl.BlockSpec((1, rows, ADA_COLS), lambda l, j: (l, 0, j)),
        out_shape=jax.ShapeDtypeStruct((depth, rows, n6), F32),
        compiler_params=_params(("arbitrary", "arbitrary")),
        name="ada",
    )(c_all, w_ada, b_ada.reshape(depth, 1, n6))


def _inproj_kernel(x_ref, mod_ref, g_ref, w_ref, o_ref):
    rows = x_ref.shape[0]
    hn = _rms(x_ref[...], g_ref[...]) * (1.0 + _mod_row(mod_ref, 1, rows)) + _mod_row(mod_ref, 0, rows)
    o_ref[...] = jnp.dot(hn.astype(BF16), w_ref[...], preferred_element_type=F32)


def _inproj(x, mod, boff, g1, w_in, t):
    n, d = x.shape
    cols = w_in.shape[1]
    tm = INPROJ_TILE
    assert n % tm == 0
    return pl.pallas_call(
        _inproj_kernel,
        grid=(n // tm,),
        in_specs=[
            pl.BlockSpec((tm, d), lambda i: (i, 0)),
            _mod_spec(t, boff, d, tm),
            pl.BlockSpec((1, d), lambda i: (0, 0)),
            pl.BlockSpec((d, cols), lambda i: (0, 0)),
        ],
        out_specs=pl.BlockSpec((tm, cols), lambda i: (i, 0)),
        out_shape=jax.ShapeDtypeStruct((n, cols), F32),
        compiler_params=_params(("arbitrary",)),
        name="inproj",
    )(x, mod, g1, w_in)


def _rglru_kernel(xr_ref, yr_ref, cst_ref, h0_ref, cw_ref, cb_ref, wr_ref, br_ref, wi_ref, bi_ref,
                  lam_ref, o_ref, cnew_ref, hlast_ref, tail_s, h_s, *, tt):
    j = pl.program_id(1)
    w = xr_ref.shape[1]

    @pl.when(j == 0)
    def _():
        tail_s[...] = jnp.zeros((V7X_SUBLANES, w), F32)
        tail_s[V7X_SUBLANES - (CONV_W - 1):, :] = cst_ref[0]
        h_s[...] = h0_ref[0]

    for r0 in range(0, xr_ref.shape[0], tt):
        _rglru_tile(slice(r0, r0 + tt), xr_ref, yr_ref, cw_ref, cb_ref, wr_ref, br_ref, wi_ref, bi_ref,
                    lam_ref, o_ref, cnew_ref, hlast_ref, tail_s, h_s)


def _rglru_tile(rs, xr_ref, yr_ref, cw_ref, cb_ref, wr_ref, br_ref, wi_ref, bi_ref, lam_ref,
                o_ref, cnew_ref, hlast_ref, tail_s, h_s):
    tt = rs.stop - rs.start
    w = xr_ref.shape[1]
    x = xr_ref[rs, :]
    ext = jnp.concatenate([tail_s[...], x], axis=0)
    cw = cw_ref[...]
    xc = cb_ref[...] + x * cw[CONV_W - 1:CONV_W, :]
    for d in range(1, CONV_W):
        xs = pltpu.roll(ext, d, axis=0)[V7X_SUBLANES:, :]
        xc = xc + xs * cw[CONV_W - 1 - d:CONV_W - d, :]
    tail_s[...] = x[tt - V7X_SUBLANES:, :]
    cnew_ref[0] = x[tt - (CONV_W - 1):, :]

    xb = xc.astype(BF16)
    gr = jax.nn.sigmoid(jnp.dot(xb, wr_ref[...], preferred_element_type=F32) + br_ref[...])
    gi = jax.nn.sigmoid(jnp.dot(xb, wi_ref[...], preferred_element_type=F32) + bi_ref[...])
    nl = -lam_ref[...]
    sp = jnp.maximum(nl, 0.0) + jnp.log1p(jnp.exp(-jnp.abs(nl)))
    a = jnp.exp((-RG_C) * gr * sp)
    u = jnp.sqrt(1.0 - a * a) * gi * xc

    grp = V7X_SUBLANES
    ngrp = tt // grp
    a3 = a.reshape(ngrp, grp, w)
    u3 = u.reshape(ngrp, grp, w)
    sub = lax.broadcasted_iota(I32, (ngrp, grp, w), 1)
    d = 1
    while d < grp:
        keep = sub >= d
        u3 = jnp.where(keep, u3 + a3 * pltpu.roll(u3, d, axis=1), u3)
        a3 = jnp.where(keep, a3 * pltpu.roll(a3, d, axis=1), a3)
        d *= 2
    last = lambda x3: jnp.broadcast_to(x3[:, grp - 1:grp, :], x3.shape)
    a_end, u_end = last(a3), last(u3)
    carry = jnp.broadcast_to(h_s[...], (grp, w))
    h_groups = []
    for gi_ in range(ngrp):
        h_groups.append(u3[gi_] + a3[gi_] * carry)
        carry = u_end[gi_] + a_end[gi_] * carry
    h = jnp.concatenate(h_groups, axis=0)
    h_s[...] = carry[:1, :]
    hlast_ref[0] = carry[:1, :]
    o_ref[rs, :] = (jax.nn.gelu(yr_ref[rs, :], approximate=True) * h).astype(BF16)


def _seq_step_rows(t, tile):
    per = min(SEQ_TILES_PER_STEP, t // tile)
    while t % (tile * per):
        per -= 1
    return tile * per


def _rglru(proj, conv_st, h0, cw, cb, wr, br, wi, bi, lam, bsz, t):
    n = proj.shape[0]
    w = cw.shape[1]
    tile = min(t, ROW_TILE)
    assert tile >= V7X_SUBLANES and t % tile == 0
    tt = _seq_step_rows(t, tile)
    nt = t // tt
    full = lambda shape: pl.BlockSpec(shape, lambda b, j: (0,) * len(shape))
    return pl.pallas_call(
        functools.partial(_rglru_kernel, tt=tile),
        grid=(bsz, nt),
        in_specs=[
            pl.BlockSpec((tt, w), lambda b, j: (b * nt + j, 0)),
            pl.BlockSpec((tt, w), lambda b, j: (b * nt + j, 1)),
            pl.BlockSpec((1, CONV_W - 1, w), lambda b, j: (b, 0, 0)),
            pl.BlockSpec((1, 1, w), lambda b, j: (b, 0, 0)),
            full((CONV_W, w)), full((1, w)), full((w, w)), full((1, w)), full((w, w)), full((1, w)),
            full((1, w)),
        ],
        out_specs=[
            pl.BlockSpec((tt, w), lambda b, j: (b * nt + j, 0)),
            pl.BlockSpec((1, CONV_W - 1, w), lambda b, j: (b, 0, 0)),
            pl.BlockSpec((1, 1, w), lambda b, j: (b, 0, 0)),
        ],
        out_shape=[
            jax.ShapeDtypeStruct((n, w), BF16),
            jax.ShapeDtypeStruct((bsz, CONV_W - 1, w), F32),
            jax.ShapeDtypeStruct((bsz, 1, w), F32),
        ],
        scratch_shapes=[pltpu.VMEM((V7X_SUBLANES, w), F32), pltpu.VMEM((1, w), F32)],
        compiler_params=_params(("arbitrary", "arbitrary")),
        name="rglru",
    )(proj, proj, conv_st, h0.reshape(bsz, 1, w), cw, cb, wr, br, wi, bi, lam)


def _hgrn_tables(c):
    t = jnp.arange(c)[:, None]
    r = jnp.arange(c)[None, :]
    tri = (r <= t).astype(BF16)
    bd = min(c, V7X_LANES)
    t, r = t[:bd, :bd], r[:bd, :bd]
    masks = [(t == r)]
    h = 1
    while 2 * h <= bd:
        m_t = (t // (2 * h)) * (2 * h) + h
        same = (t // (2 * h)) == (r // (2 * h))
        masks.append(same & (t >= m_t) & (r < m_t))
        h *= 2
    return tri, jnp.stack(masks).astype(F32)


def _hgrn_kernel(q_ref, f_ref, v_ref, g_ref, lb_ref, ng_ref, s0_ref, tri_ref, mask_ref,
                 o_ref, slast_ref, s_s, acc_s, g_s, *, c, nlev):
    @pl.when(pl.program_id(1) == 0)
    def _():
        s_s[...] = s0_ref[0]

    for r0 in range(0, f_ref.shape[0], c):
        _hgrn_chunk(slice(r0, r0 + c), q_ref, f_ref, v_ref, g_ref, lb_ref, ng_ref, tri_ref, mask_ref,
                    o_ref, s_s, acc_s, g_s, nlev)
    slast_ref[0] = s_s[...]


def _hgrn_chunk(rs, q_ref, f_ref, v_ref, g_ref, lb_ref, ng_ref, tri_ref, mask_ref, o_ref,
                s_s, acc_s, g_s, nlev):
    c = rs.stop - rs.start
    w = f_ref.shape[1]
    ff = f_ref[rs, :]
    lb = lb_ref[...]
    log_sig = -(jnp.maximum(-ff, 0.0) + jnp.log1p(jnp.exp(-jnp.abs(ff))))
    log_f = log_sig + jnp.log1p(lb * jnp.exp(-ff))
    k = (1.0 - lb) * jax.nn.sigmoid(-ff)
    q = q_ref[rs, :]

    hi = log_f.astype(BF16)
    r1 = log_f - hi.astype(F32)
    mid = r1.astype(BF16)
    lo = (r1 - mid.astype(F32)).astype(BF16)
    tri = tri_ref[...]
    gcum = (jnp.dot(tri, hi, preferred_element_type=F32) + jnp.dot(tri, mid, preferred_element_type=F32)
            + jnp.dot(tri, lo, preferred_element_type=F32))
    gcum = gcum * LOG2_E
    g_s[...] = gcum

    row = lax.broadcasted_iota(I32, (c, w), 0)
    grp = V7X_SUBLANES
    groups = gcum.reshape(c // grp, grp, w)

    def group_row(r):
        return jnp.broadcast_to(groups[:, r:r + 1, :], groups.shape).reshape(c, w)

    def level_exponent(h):
        if h >= grp:
            parts = []
            for b in range(c // (2 * h)):
                lo0, m, hi0 = b * 2 * h, b * 2 * h + h, (b + 1) * 2 * h
                g_mid = g_s[m - 1:m, :]
                parts += [g_mid - gcum[lo0:m, :], gcum[m:hi0, :] - g_mid]
            return jnp.concatenate(parts, axis=0)
        if h == 1:
            g_mid = jnp.where((row & 1) == 1, pltpu.roll(gcum, 1, axis=0), gcum)
        else:
            g_mid = group_row(h - 1)
            for b in range(1, grp // (2 * h)):
                g_mid = jnp.where((row & (grp - 1)) >= b * 2 * h, group_row(b * 2 * h + h - 1), g_mid)
        return -jnp.abs(gcum - g_mid)

    bd = mask_ref.shape[1]

    def pair_scores(l, qw, kw):
        h = 0 if l == 0 else 2 ** (l - 1)
        keep = mask_ref[l] > 0.0 if 2 * h <= bd else None
        for hd in range(HG_HEADS):
            sl = slice(hd * HG_D, (hd + 1) * HG_D)
            if 2 * h <= bd:
                for b in range(c // bd):
                    rs = slice(b * bd, (b + 1) * bd)
                    p = lax.dot_general(qw[rs, sl], kw[rs, sl], NT_DIMS, preferred_element_type=F32)
                    acc_s[hd, rs, rs] = jnp.where(keep, p, 0.0 if l == 0 else acc_s[hd, rs, rs])
            else:
                for b in range(c // (2 * h)):
                    rows = slice(b * 2 * h + h, (b + 1) * 2 * h)
                    cols = slice(b * 2 * h, b * 2 * h + h)
                    acc_s[hd, rows, cols] = lax.dot_general(qw[rows, sl], kw[cols, sl], NT_DIMS,
                                                            preferred_element_type=F32)

    if c > bd:
        acc_s[...] = jnp.zeros(acc_s.shape, F32)
    qb = q.astype(BF16)
    kb = k.astype(BF16)
    pair_scores(0, qb, kb)
    for l in range(1, nlev + 1):
        wgt = jnp.exp2(level_exponent(2 ** (l - 1))).astype(BF16)
        pair_scores(l, qb * wgt, kb * wgt)

    glast = gcum[c - 1:, :]
    qg = (q * jnp.exp2(gcum)).astype(BF16)
    kdec = (k * jnp.exp2(glast - gcum)).astype(BF16)
    dlast = jnp.exp2(glast)
    vb = v_ref[rs, :].astype(BF16)
    for hd in range(HG_HEADS):
        sl = slice(hd * HG_D, (hd + 1) * HG_D)
        s_old = s_s[hd]
        o = (jnp.dot(acc_s[hd].astype(BF16), vb[:, sl], preferred_element_type=F32)
             + jnp.dot(qg[:, sl], s_old.astype(BF16), preferred_element_type=F32))
        kv = lax.dot_general(kdec[:, sl], vb[:, sl], TN_DIMS, preferred_element_type=F32)
        dcol = jnp.broadcast_to(dlast[:, sl], (HG_D, HG_D)).T
        s_s[hd] = dcol * s_old + kv
        gate = g_ref[rs, sl]
        o_ref[rs, sl] = (_rms(o, ng_ref[...]) * (gate * jax.nn.sigmoid(gate))).astype(BF16)


def _hgrn(proj, lb, ng, s0, bsz, t):
    n = proj.shape[0]
    w = HG_HEADS * HG_D
    c = min(t, HGRN_CHUNK)
    nlev = int(math.log2(c))
    assert 2 ** nlev == c and t % c == 0
    tri, masks = _hgrn_tables(c)
    tt = _seq_step_rows(t, c)
    nt = t // tt
    col = lambda k: pl.BlockSpec((tt, w), lambda b, j: (b * nt + j, k))
    full = lambda shape: pl.BlockSpec(shape, lambda b, j: (0,) * len(shape))
    return pl.pallas_call(
        functools.partial(_hgrn_kernel, c=c, nlev=nlev),
        grid=(bsz, nt),
        in_specs=[
            col(2), col(3), col(4), col(5),
            full((1, w)), full((1, HG_D)),
            pl.BlockSpec((1, HG_HEADS, HG_D, HG_D), lambda b, j: (b, 0, 0, 0)),
            full(tri.shape), full(masks.shape),
        ],
        out_specs=[
            pl.BlockSpec((tt, w), lambda b, j: (b * nt + j, 0)),
            pl.BlockSpec((1, HG_HEADS, HG_D, HG_D), lambda b, j: (b, 0, 0, 0)),
        ],
        out_shape=[
            jax.ShapeDtypeStruct((n, w), BF16),
            jax.ShapeDtypeStruct((bsz, HG_HEADS, HG_D, HG_D), F32),
        ],
        scratch_shapes=[pltpu.VMEM((HG_HEADS, HG_D, HG_D), F32),
                        pltpu.VMEM((HG_HEADS, c, c), F32),
                        pltpu.VMEM((c, w), F32)],
        compiler_params=_params(("arbitrary", "arbitrary")),
        name="hgrn",
    )(proj, proj, proj, proj, lb, ng, s0, tri, masks)


def _outproj_kernel(org_ref, ohg_ref, x_ref, mod_ref, wo_ref, g2_ref, wr_ref, brt_ref, tri_ref,
                    low_ref, cin_ref, x1_ref, hb_ref, pos_ref, prob_ref, segb_ref, segl_ref, cnt_ref,
                    carry_s, *, tm, parts):
    @pl.when(pl.program_id(0) == 0)
    def _():
        carry_s[...] = cin_ref[...]

    carry = carry_s[...]
    for part in range(parts):
        carry = _outproj_part(part, parts, tm, carry, org_ref, ohg_ref, x_ref, mod_ref, wo_ref,
                              g2_ref, wr_ref, brt_ref, tri_ref, low_ref, x1_ref, hb_ref, pos_ref,
                              prob_ref, segb_ref, segl_ref)
    carry_s[...] = carry
    cnt_ref[...] = carry


def _outproj_part(part, parts, tm, carry, org_ref, ohg_ref, x_ref, mod_ref, wo_ref, g2_ref, wr_ref,
                  brt_ref, tri_ref, low_ref, x1_ref, hb_ref, pos_ref, prob_ref, segb_ref, segl_ref):
    rs = slice(part * tm, (part + 1) * tm)
    mrow = lambda k: _mod_row(mod_ref, k, tm, part, parts)
    half = org_ref.shape[1]
    mix = (jnp.dot(org_ref[rs, :], wo_ref[:half, :], preferred_element_type=F32)
           + jnp.dot(ohg_ref[rs, :], wo_ref[half:, :], preferred_element_type=F32))
    x1 = x_ref[rs, :] + mrow(2) * mix
    x1_ref[rs, :] = x1
    hn = _rms(x1, g2_ref[...]) * (1.0 + mrow(4)) + mrow(3)
    hn_hi = hn.astype(BF16)
    hb_ref[rs, :] = hn_hi

    ne = brt_ref.shape[0]
    hn_lo = (hn - hn_hi.astype(F32)).astype(BF16)
    both = jnp.dot(hn_hi, wr_ref[...], preferred_element_type=F32)
    logits_tm = (both[:, :V7X_LANES] + both[:, V7X_LANES:]
                 + jnp.dot(hn_lo, wr_ref[:, :V7X_LANES], preferred_element_type=F32))
    logits = logits_tm.T[:ne, :] + brt_ref[...]
    eio = lax.broadcasted_iota(I32, (ne, tm), 0)
    cur = logits
    vals, sels = [], []
    for _ in range(TOP_K):
        mx = jnp.max(cur, axis=0, keepdims=True)
        ix = jnp.min(jnp.where(cur == mx, eio, ne), axis=0, keepdims=True)
        sel = eio == ix
        vals.append(mx)
        sels.append(sel)
        cur = jnp.where(sel, -jnp.inf, cur)
    ex = [jnp.exp(v - vals[0]) for v in vals]
    den = ex[0] + ex[1] + ex[2] + ex[3]
    probs = [e / den for e in ex]

    onehot = sels[0].astype(F32)
    for sel in sels[1:]:
        onehot = onehot + sel.astype(F32)
    incl = jnp.dot(onehot.astype(BF16), tri_ref[...], preferred_element_type=F32)
    run_len = jnp.ceil(incl[:, tm - 1:] * (1.0 / RUN_ALIGN)) * RUN_ALIGN
    run0 = jnp.dot(low_ref[...], jnp.broadcast_to(run_len, (ne, tm)).astype(BF16),
                   preferred_element_type=F32)
    local = run0 + incl - onehot
    slots = [jnp.sum(jnp.where(sel, local, 0.0), axis=0, keepdims=True) for sel in sels]
    segb_ref[part] = carry
    segl_ref[part] = run_len

    pos_ref[:, rs] = jnp.concatenate(slots, axis=0).astype(I32)
    sub = lax.broadcasted_iota(I32, (V7X_LANES, tm), 0)
    pt = jnp.zeros((V7X_LANES, tm), F32)
    for kk in range(TOP_K):
        pt = jnp.where(sub == kk, probs[kk], pt)
        pt = jnp.where(sub == TOP_K + kk, slots[kk], pt)
    prob_ref[rs, :] = pt.T
    return carry + run_len


def _router_weights(w_router):
    d, ne = w_router.shape
    assert ne <= V7X_LANES
    hi = w_router.astype(BF16)
    lo = (w_router - hi.astype(F32)).astype(BF16)
    pad = jnp.zeros((d, V7X_LANES - ne), BF16)
    return jnp.concatenate([hi, pad, lo, pad], axis=1)


def _outproj(o_rg, o_hg, x, mod, boff, w_out, g2, wr, brt, cnt_in, t):
    n, d = x.shape
    half = o_rg.shape[1]
    ne = brt.shape[0]
    tm = ROW_TILE
    parts = OUTPROJ_PARTS
    tile = tm * parts
    assert n % tile == 0 and tm <= 256
    nt = n // tm
    tri = (jnp.arange(tm)[:, None] <= jnp.arange(tm)[None, :]).astype(BF16)
    low = (jnp.arange(ne)[:, None] > jnp.arange(ne)[None, :]).astype(BF16)
    rowblk = lambda cols: pl.BlockSpec((tile, cols), lambda i: (i, 0))
    full = lambda shape: pl.BlockSpec(shape, lambda i: (0,) * len(shape))
    pertile = lambda: pl.BlockSpec((parts, ne, 1), lambda i: (i, 0, 0))
    return pl.pallas_call(
        functools.partial(_outproj_kernel, tm=tm, parts=parts),
        grid=(n // tile,),
        in_specs=[
            rowblk(half), rowblk(half), rowblk(d),
            _mod_spec(t, boff, d, tile),
            full((d, d)), full((1, d)), full(wr.shape), full((ne, 1)), full((tm, tm)), full((ne, ne)),
            full((ne, 1)),
        ],
        out_specs=[
            rowblk(d), rowblk(d), pl.BlockSpec((TOP_K, tile), lambda i: (0, i)), rowblk(V7X_LANES),
            pertile(), pertile(), full((ne, 1)),
        ],
        out_shape=[
            jax.ShapeDtypeStruct((n, d), F32),
            jax.ShapeDtypeStruct((n, d), BF16),
            jax.ShapeDtypeStruct((TOP_K, n), I32),
            jax.ShapeDtypeStruct((n, V7X_LANES), F32),
            jax.ShapeDtypeStruct((nt, ne, 1), F32),
            jax.ShapeDtypeStruct((nt, ne, 1), F32),
            jax.ShapeDtypeStruct((ne, 1), F32),
        ],
        scratch_shapes=[pltpu.VMEM((ne, 1), F32)],
        compiler_params=_params(("arbitrary",)),
        name="outproj",
    )(o_rg, o_hg, x, mod, w_out, g2, wr, brt, tri, low, cnt_in)


def _pow2_pieces(hi, lo=1):
    return tuple(1 << b for b in range(int(math.log2(hi)), int(math.log2(lo)) - 1, -1))


def _tile_slots(ne):
    worst = TOP_K * ROW_TILE + ne * (RUN_ALIGN - 1)
    return -(-worst // V7X_LANES) * V7X_LANES


def _stored_rows(ref, first_row, rows):
    start = first_row * ROW_PARTS
    if not isinstance(start, int):
        start = pl.multiple_of(start, V7X_SUBLANES)
    return ref.at[pl.ds(start, rows * ROW_PARTS), :]


def _store_packed(ref, words):
    rows = words.shape[0]
    for c in range(ROW_PARTS):
        ref[pl.ds(c, rows, stride=ROW_PARTS), :] = words[:, c * V7X_LANES:(c + 1) * V7X_LANES]


def _load_packed(ref):
    rows = ref.shape[0] // ROW_PARTS
    return jnp.concatenate([ref[pl.ds(c, rows, stride=ROW_PARTS), :] for c in range(ROW_PARTS)], axis=1)


def _copy_run(n, make_copy, wait, pieces):
    for rows in pieces:
        done = n & (-2 * rows)

        @pl.when((n & rows) != 0)
        def _():
            cp = make_copy(done, rows)
            cp.wait() if wait else cp.start()


RUN_PIECES = _pow2_pieces(ROW_TILE, RUN_ALIGN)
BUF_ROW_BITS = 11
LIST_HEAD = V7X_SUBLANES


def _run_lists(seg_start, seg_len, ne):
    assert _tile_slots(ne) <= 1 << BUF_ROW_BITS
    tiles = seg_len.shape[0]
    buf0 = jnp.cumsum(seg_len, axis=1) - seg_len
    j = jnp.arange(ne, dtype=I32)
    counts, entries = [], []
    for rows in RUN_PIECES:
        has = (seg_len & rows) != 0
        done = seg_len & (-2 * rows)
        packed = ((seg_start + done) << BUF_ROW_BITS) | (buf0 + done)
        rank = jnp.cumsum(has.astype(I32), axis=1) - 1
        place = has[:, :, None] & (rank[:, :, None] == j[None, None, :])
        entries.append(jnp.sum(jnp.where(place, packed[:, :, None], 0), axis=1))
        counts.append(jnp.sum(has.astype(I32), axis=1))
    head = jnp.stack(counts + [jnp.zeros((tiles,), I32)] * (LIST_HEAD - len(RUN_PIECES)), axis=1)
    flat = jnp.concatenate([head] + entries, axis=1)
    width = -(-flat.shape[1] // V7X_LANES) * V7X_LANES
    flat = jnp.pad(flat, ((0, 0), (0, width - flat.shape[1])))
    return flat.reshape(tiles, 1, width)


def _list_spec(ne, index_map):
    width = -(-(LIST_HEAD + len(RUN_PIECES) * ne) // V7X_LANES) * V7X_LANES
    return pl.BlockSpec((1, 1, width), index_map, memory_space=pltpu.SMEM)


def _tile_runs(list_ref, ne, make_copy, wait):
    for pi, rows in enumerate(RUN_PIECES):
        def one(jj, pi=pi, rows=rows):
            v = list_ref[0, 0, LIST_HEAD + pi * ne + jj]
            cp = make_copy(v >> BUF_ROW_BITS, v & ((1 << BUF_ROW_BITS) - 1), rows)
            cp.wait() if wait else cp.start()

        def pair(jj, carry, one=one):
            one(2 * jj)
            one(2 * jj + 1)
            return carry

        cnt = list_ref[0, 0, pi]
        lax.fori_loop(0, cnt >> 1, pair, 0)
        pl.when((cnt & 1) != 0)(functools.partial(one, cnt - 1))


def _pack_pairs(lo_cols, hi_cols):
    a = lax.bitcast_convert_type(lo_cols, jnp.uint32)
    b = lax.bitcast_convert_type(hi_cols, jnp.uint32)
    return (a & jnp.uint32(0xFFFF0000)) | (b >> 16)


def _unpack_pairs(words):
    a = lax.bitcast_convert_type(words & jnp.uint32(0xFFFF0000), F32).astype(BF16)
    b = lax.bitcast_convert_type(words << 16, F32).astype(BF16)
    return a, b


def _dispatch_kernel(zoff_ref, zcnt_ref, lst_ref, lst1_ref, lst2_ref, pos_ref, hba_ref, hbb_ref,
                     out_ref, buf_s, zero_s, sem, zsem, *, ne, tiles_a):
    i = pl.program_id(0)
    last = pl.num_programs(0) - 1
    slot = i % 2
    tt, d = hba_ref.shape
    dh = d // 2

    def runs(list_ref, s, wait):
        _tile_runs(list_ref, ne,
                   lambda g, b, rows: pltpu.make_async_copy(
                       _stored_rows(buf_s.at[s], b, rows), _stored_rows(out_ref, g, rows), sem.at[s]),
                   wait)

    @pl.when(i >= 2)
    def _():
        runs(lst2_ref, slot, True)

    pos = pos_ref[...]
    slot_row = lax.broadcasted_iota(I32, (buf_s.shape[1] // ROW_PARTS, tt), 0)
    hit = slot_row == pos[0:1, :]
    for kk in range(1, TOP_K):
        hit = hit | (slot_row == pos[kk:kk + 1, :])
    perm = jnp.where(hit, 1.0, 0.0).astype(BF16)
    hb = jnp.where(i < tiles_a, hba_ref[...], hbb_ref[...])
    _store_packed(buf_s.at[slot], _pack_pairs(jnp.dot(perm, hb[:, :dh], preferred_element_type=F32),
                                              jnp.dot(perm, hb[:, dh:], preferred_element_type=F32)))
    runs(lst_ref, slot, False)

    @pl.when(i == last)
    def _():
        @pl.when(i >= 1)
        def _():
            runs(lst1_ref, 1 - slot, True)
        runs(lst_ref, slot, True)

        zero_s[...] = jnp.zeros(zero_s.shape, zero_s.dtype)
        zrows = zero_s.shape[0] // ROW_PARTS
        zcopy = lambda g, rows: pltpu.make_async_copy(
            _stored_rows(zero_s, 0, rows), _stored_rows(out_ref, g, rows), zsem)

        def zero_run(e, carry):
            lo = zoff_ref[e] + zcnt_ref[e]
            n = zoff_ref[e + 1] - lo
            nfull = n // zrows
            rem = n - nfull * zrows
            for wait in (False, True):
                def full(c, c2):
                    cp = zcopy(lo + c * zrows, zrows)
                    cp.wait() if wait else cp.start()
                    return c2
                lax.fori_loop(0, nfull, full, 0)
                _copy_run(rem, lambda o, rows: zcopy(lo + nfull * zrows + o, rows), wait,
                          _pow2_pieces(zrows // 2, RUN_ALIGN))
            return carry

        lax.fori_loop(0, zcnt_ref.shape[0], zero_run, 0)


def _dispatch(lists, zoff, zcnt, pos_t, hb_a, hb_b, n_rows, ne):
    d = hb_a.shape[1]
    tt = ROW_TILE
    ta, tb = hb_a.shape[0] // tt, hb_b.shape[0] // tt
    assert hb_a.shape[0] % tt == 0 and hb_b.shape[0] % tt == 0
    back = lambda k: _list_spec(ne, lambda i, *_: (jnp.maximum(i - k, 0), 0, 0))
    return pl.pallas_call(
        functools.partial(_dispatch_kernel, ne=ne, tiles_a=ta),
        grid_spec=pltpu.PrefetchScalarGridSpec(
            num_scalar_prefetch=2,
            grid=(ta + tb,),
            in_specs=[back(0), back(1), back(2),
                      pl.BlockSpec((TOP_K, tt), lambda i, *_: (0, i)),
                      pl.BlockSpec((tt, d), lambda i, *_: (jnp.minimum(i, ta - 1), 0)),
                      pl.BlockSpec((tt, d), lambda i, *_: (jnp.maximum(i - ta, 0), 0))],
            out_specs=pl.BlockSpec(memory_space=pl.ANY),
            scratch_shapes=[pltpu.VMEM((2, _tile_slots(ne) * ROW_PARTS, V7X_LANES), jnp.uint32),
                            pltpu.VMEM((tt * ROW_PARTS, V7X_LANES), jnp.uint32),
                            pltpu.SemaphoreType.DMA((2,)), pltpu.SemaphoreType.DMA(())],
        ),
        out_shape=jax.ShapeDtypeStruct((n_rows * ROW_PARTS, V7X_LANES), jnp.uint32),
        compiler_params=_params(("arbitrary",)),
        name="dispatch",
    )(zoff, zcnt, lists, lists, lists, pos_t, hb_a, hb_b)


def _ffn_kernel(te_ref, tr_ref, xs_ref, wg_ref, bg_ref, wu_ref, bu_ref, wd_ref, bd_ref, y_ref,
                wg_s, wu_s, wd_s):
    i = pl.program_id(0)
    rows = tr_ref[i]
    valid = rows > 0
    changed = (i == 0) | (te_ref[i] != te_ref[jnp.maximum(i - 1, 0)])

    @pl.when(valid & changed)
    def _():
        wg_s[...] = wg_ref[...].astype(BF16)
        wu_s[...] = wu_ref[...].astype(BF16)
        wd_s[...] = wd_ref[...].astype(BF16)

    full = xs_ref.shape[0] // ROW_PARTS
    half = full // 2
    dh = ROW_PARTS * xs_ref.shape[1]

    def run(n):
        stored = pl.ds(0, n * ROW_PARTS)
        xa, xb = _unpack_pairs(_load_packed(xs_ref.at[stored, :]))
        gate = (jnp.dot(xa, wg_s[:dh, :], preferred_element_type=F32)
                + jnp.dot(xb, wg_s[dh:, :], preferred_element_type=F32) + bg_ref[...])
        up = (jnp.dot(xa, wu_s[:dh, :], preferred_element_type=F32)
              + jnp.dot(xb, wu_s[dh:, :], preferred_element_type=F32) + bu_ref[...])
        gate = jnp.minimum(gate, SWIGLU_LIMIT)
        up = jnp.clip(up, -SWIGLU_LIMIT, SWIGLU_LIMIT)
        glu = gate * jax.nn.sigmoid(SWIGLU_ALPHA * gate)
        act = ((up + 1.0) * glu).astype(BF16)
        y = (jnp.dot(act, wd_s[...], preferred_element_type=F32) + bd_ref[...]).astype(BF16).astype(F32)
        yh = y.shape[1] // 2
        _store_packed(y_ref.at[stored, :], _pack_pairs(y[:, :yh], y[:, yh:]))

    pl.when(rows > half)(lambda: run(full))

    @pl.when(valid & (rows <= half))
    def _():
        run(half)
        y_ref[half * ROW_PARTS:, :] = jnp.zeros((half * ROW_PARTS, y_ref.shape[1]), y_ref.dtype)

    @pl.when(jnp.logical_not(valid))
    def _():
        y_ref[...] = jnp.zeros(y_ref.shape, y_ref.dtype)


def _ffn(layer, tile_expert, tile_rows, xs, w_gate, b_gate, w_up, b_up, w_down, b_down):
    d, dff = w_gate.shape[2], w_gate.shape[3]
    tm = FFN_TILE * ROW_PARTS
    assert xs.shape[1] == V7X_LANES and xs.shape[0] % tm == 0
    wspec = lambda r, c: pl.BlockSpec((None, None, r, c), lambda i, te, nu: (layer, te[i], 0, 0))
    return pl.pallas_call(
        _ffn_kernel,
        grid_spec=pltpu.PrefetchScalarGridSpec(
            num_scalar_prefetch=2,
            grid=(xs.shape[0] // tm,),
            in_specs=[
                pl.BlockSpec((tm, V7X_LANES), lambda i, te, nu: (i, 0)),
                wspec(d, dff), wspec(1, dff), wspec(d, dff), wspec(1, dff), wspec(dff, d), wspec(1, d),
            ],
            out_specs=pl.BlockSpec((tm, V7X_LANES), lambda i, te, nu: (i, 0)),
            scratch_shapes=[pltpu.VMEM((d, dff), BF16), pltpu.VMEM((d, dff), BF16),
                            pltpu.VMEM((dff, d), BF16)],
        ),
        out_shape=jax.ShapeDtypeStruct(xs.shape, jnp.uint32),
        compiler_params=_params(("arbitrary",)),
        name="ffn",
    )(tile_expert, tile_rows, xs, w_gate, b_gate, w_up, b_up, w_down, b_down)


def _combine_kernel(lst_ref, nxt_ref, y_ref, prob_ref, x1_ref, mod_ref, gf_ref, o_ref,
                    buf_s, sem, *, ne, final):
    i = pl.program_id(0)
    slot = i % 2
    tt = x1_ref.shape[0]

    def runs(list_ref, s, wait):
        _tile_runs(list_ref, ne,
                   lambda g, b, rows: pltpu.make_async_copy(
                       _stored_rows(y_ref, g, rows), _stored_rows(buf_s.at[s], b, rows), sem.at[s]),
                   wait)

    @pl.when(i == 0)
    def _():
        buf_s[...] = jnp.zeros(buf_s.shape, buf_s.dtype)
        runs(lst_ref, slot, False)

    @pl.when(i + 1 < pl.num_programs(0))
    def _():
        runs(nxt_ref, 1 - slot, False)

    runs(lst_ref, slot, True)

    ya, yb = _unpack_pairs(_load_packed(buf_s.at[slot]))
    pp = prob_ref[...]
    slots = buf_s.shape[1] // ROW_PARTS
    slot_col = lax.broadcasted_iota(I32, (tt, slots), 1)
    wgt = jnp.zeros((tt, slots), F32)
    for kk in range(TOP_K):
        pos = pp[:, TOP_K + kk:TOP_K + kk + 1].astype(I32)
        wgt = jnp.where(slot_col == pos, pp[:, kk:kk + 1], wgt)
    wgt = wgt.astype(BF16)
    y = jnp.concatenate([jnp.dot(wgt, ya, preferred_element_type=F32),
                         jnp.dot(wgt, yb, preferred_element_type=F32)], axis=1)
    x2 = x1_ref[...] + _mod_row(mod_ref, 5, tt) * y
    if final:
        x2 = _rms(x2, gf_ref[...])
    o_ref[...] = x2


def _combine(lists, y_rows, prob, x1, mod, boff, g_final, roff, t, ne, final):
    n, d = x1.shape
    tt = ROW_TILE
    r0 = roff // tt
    assert roff % tt == 0 and n % tt == 0
    last = r0 + n // tt - 1
    return pl.pallas_call(
        functools.partial(_combine_kernel, ne=ne, final=final),
        grid=(n // tt,),
        in_specs=[
            _list_spec(ne, lambda i: (r0 + i, 0, 0)),
            _list_spec(ne, lambda i: (jnp.minimum(r0 + i + 1, last), 0, 0)),
            pl.BlockSpec(memory_space=pl.ANY),
            pl.BlockSpec((tt, V7X_LANES), lambda i: (i, 0)),
            pl.BlockSpec((tt, d), lambda i: (i, 0)),
            _mod_spec(t, boff, d),
            pl.BlockSpec((1, d), lambda i: (0, 0)),
        ],
        out_specs=pl.BlockSpec((tt, d), lambda i: (i, 0)),
        scratch_shapes=[pltpu.VMEM((2, _tile_slots(ne) * ROW_PARTS, V7X_LANES), jnp.uint32),
                        pltpu.SemaphoreType.DMA((2,))],
        out_shape=jax.ShapeDtypeStruct((n, d), F32),
        compiler_params=_params(("arbitrary",)),
        name="combine",
    )(lists, lists, y_rows, prob, x1, mod, g_final)


def _block_diag(w):
    depth, nb, c, _ = w.shape
    eye = jnp.eye(nb, dtype=w.dtype)
    return (eye[None, :, None, :, None] * w[:, :, :, None, :]).reshape(depth, nb * c, nb * c)


def kernel(x_prompt, x_sample, state_conv, state_rglru, state_hgrn, c_prompt, c_sample, w_ada, b_ada, g_norm1, g_norm2, w_in, conv_w, conv_b, w_rgate, b_rgate, w_igate, b_igate, lru_lambda, hgrn_lb, hgrn_norm_g, w_out, w_router, b_router, w_gate, b_gate, w_up, b_up, w_down, b_down, g_final):
    depth = w_ada.shape[0]
    bp, tp, d = x_prompt.shape
    bs, ts, _ = x_sample.shape
    w_rg = conv_w.shape[2]
    ne = w_router.shape[2]
    n_p, n_s = bp * tp, bs * ts
    n_tot = n_p + n_s
    span = max(max(ROW_TILE * OUTPROJ_PARTS, INPROJ_TILE) // ts, 1)
    boff_s = -(-bp // span) * span
    groups = ((bp, tp, 0, 0), (bs, ts, boff_s, n_p))
    n_c_pad = -(-(boff_s + bs) // V7X_SUBLANES) * V7X_SUBLANES
    c_all = jnp.concatenate([c_prompt, jnp.zeros((boff_s - bp, d), F32), c_sample,
                             jnp.zeros((n_c_pad - boff_s - bs, d), F32)], axis=0)
    mod_all = _ada(c_all, w_ada, b_ada).reshape(depth, n_c_pad, 6, d)

    p_lb = jax.nn.softmax(hgrn_lb.astype(F32), axis=0)
    lb_all = jnp.cumsum(p_lb, axis=0) - p_lb[0]

    w_in_b = w_in.astype(BF16)
    w_out_b = w_out.astype(BF16)
    wr_all = _block_diag(w_rgate).astype(BF16)
    wi_all = _block_diag(w_igate).astype(BF16)

    assert d // 2 == ROW_PARTS * V7X_LANES
    run_pad = (n_tot // ROW_TILE) * ne * (RUN_ALIGN - 1)
    n_tiles = -(-(TOP_K * n_tot + run_pad + ne * (FFN_TILE - 1)) // FFN_TILE)
    n_rows = n_tiles * FFN_TILE

    xs = [x_prompt.reshape(n_p, d), x_sample.reshape(n_s, d)]
    conv_in = [jnp.zeros((depth, bp, CONV_W - 1, w_rg), F32), state_conv]
    h_in = [jnp.zeros((depth, bp, w_rg), F32), state_rglru]
    s_in = [jnp.zeros((depth, bp, HG_HEADS, HG_D, HG_D), F32), state_hgrn]
    conv_out, h_out, s_out = ([], []), ([], []), ([], [])

    for l in range(depth):
        mod = mod_all[l]
        wr_d, wi_d = wr_all[l], wi_all[l]
        cnt = jnp.zeros((ne, 1), F32)
        x1s, hbs, poss, probs, segbs, segls = [], [], [], [], [], []
        for gi, (bsz, t, boff, _) in enumerate(groups):
            proj = _inproj(xs[gi], mod, boff, g_norm1[l][None], w_in_b[l], t)
            o_rg, c_new, h_last = _rglru(
                proj, conv_in[gi][l], h_in[gi][l], conv_w[l], conv_b[l][None], wr_d, b_rgate[l][None],
                wi_d, b_igate[l][None], lru_lambda[l][None], bsz, t)
            o_hg, s_last = _hgrn(proj, lb_all[l][None], hgrn_norm_g[l][None], s_in[gi][l], bsz, t)
            x1, hb, pos_t, prob, segb, segl, cnt = _outproj(
                o_rg, o_hg, xs[gi], mod, boff, w_out_b[l], g_norm2[l][None], _router_weights(w_router[l]),
                b_router[l][:, None], cnt, t)
            conv_out[gi].append(c_new)
            h_out[gi].append(h_last[:, 0])
            s_out[gi].append(s_last)
            x1s.append(x1)
            hbs.append(hb)
            poss.append(pos_t)
            probs.append(prob)
            segbs.append(segb[:, :, 0])
            segls.append(segl[:, :, 0])

        counts = cnt[:, 0].astype(I32)
        padded = (counts + FFN_TILE - 1) // FFN_TILE * FFN_TILE
        pad_end = jnp.cumsum(padded)
        off = jnp.concatenate([jnp.zeros((1,), I32), pad_end]).astype(I32)
        n_used = (pad_end[-1] // FFN_TILE).astype(I32)
        tile_start = jnp.minimum(jnp.arange(n_tiles, dtype=I32), n_used - 1) * FFN_TILE
        tile_expert = jnp.minimum(jnp.sum(tile_start[:, None] >= pad_end[None, :], axis=1), ne - 1).astype(I32)
        tile_idx = jnp.arange(n_tiles, dtype=I32)
        tile_rows = jnp.where(
            tile_idx < n_used,
            jnp.clip(counts[tile_expert] - (tile_idx * FFN_TILE - off[tile_expert]), 0, FFN_TILE), 0).astype(I32)
        lists = _run_lists(off[None, :ne] + jnp.concatenate(segbs, axis=0).astype(I32),
                           jnp.concatenate(segls, axis=0).astype(I32), ne)
        zoff = jnp.concatenate([off, jnp.full((1,), n_rows, I32)])
        zcnt = jnp.concatenate([counts, jnp.zeros((1,), I32)])

        x_sorted = _dispatch(lists, zoff, zcnt, jnp.concatenate(poss, axis=1), hbs[0], hbs[1],
                             n_rows, ne)
        y_rows = _ffn(l, tile_expert, tile_rows, x_sorted, w_gate, b_gate[:, :, None, :],
                      w_up, b_up[:, :, None, :], w_down, b_down[:, :, None, :])
        final = l == depth - 1
        xs = [_combine(lists, y_rows, probs[gi], x1s[gi], mod, boff, g_final[None],
                       roff, t, ne, final)
              for gi, (bsz, t, boff, roff) in enumerate(groups)]

    return (xs[0].reshape(bp, tp, d), xs[1].reshape(bs, ts, d),
            jnp.stack(conv_out[0]), jnp.stack(h_out[0]), jnp.stack(s_out[0]),
            jnp.stack(conv_out[1]), jnp.stack(h_out[1]), jnp.stack(s_out[1]))
```

```python
import functools
import math

import jax
import jax.numpy as jnp
from jax import lax
from jax.experimental import pallas as pl
from jax.experimental.pallas import tpu as pltpu

F32 = jnp.float32
BF16 = jnp.bfloat16
I32 = jnp.int32
HIGHEST = lax.Precision.HIGHEST

EPS = 1e-6
RG_C = 8.0
CONV_W = 4
HG_HEADS = 4
HG_D = 128
TOP_K = 4
SWIGLU_LIMIT = 7.0
SWIGLU_ALPHA = 1.702
LOG2_E = 1.4426950408889634

V7X_LANES = 128
V7X_SUBLANES = 8
V7X_VMEM_LIMIT_BYTES = 56 * 1024 * 1024

ROW_TILE = 256
INPROJ_TILE = 512
HGRN_CHUNK = 256
SEQ_TILES_PER_STEP = 2
OUTPROJ_PARTS = 2
FFN_TILE = 512
ROW_PARTS = 4
RUN_ALIGN = V7X_SUBLANES // ROW_PARTS
ADA_COLS = 1536

NT_DIMS = (((1,), (1,)), ((), ()))
TN_DIMS = (((0,), (0,)), ((), ()))


def _params(sem, vmem=V7X_VMEM_LIMIT_BYTES):
    return pltpu.CompilerParams(dimension_semantics=sem, vmem_limit_bytes=vmem)


def _rms(x, g):
    ms = jnp.mean(x * x, axis=-1, keepdims=True)
    return x * lax.rsqrt(ms + EPS) * g


def _mod_row(mod_ref, k, rows, part=0, parts=1):
    nb, _, d = mod_ref.shape
    if nb == 1:
        return mod_ref[0, k:k + 1, :]
    nbp = nb // parts
    per = rows // nbp
    return jnp.concatenate([jnp.broadcast_to(mod_ref[b, k:k + 1, :], (per, d))
                            for b in range(part * nbp, (part + 1) * nbp)], axis=0)


def _mod_spec(t, boff, d, tile=None):
    tile = tile or ROW_TILE
    if t >= tile:
        assert t % tile == 0
        per = t // tile
        return pl.BlockSpec((1, 6, d), lambda i, *_: (boff + i // per, 0, 0))
    nb = tile // t
    assert tile % t == 0 and boff % nb == 0
    return pl.BlockSpec((nb, 6, d), lambda i, *_: (boff // nb + i, 0, 0))


def _ada_kernel(c_ref, w_ref, b_ref, o_ref):
    c = c_ref[...]
    s = c * jax.nn.sigmoid(c)
    o_ref[0] = jnp.dot(s, w_ref[0], precision=HIGHEST, preferred_element_type=F32) + b_ref[0]


def _ada(c_all, w_ada, b_ada):
    depth, d, n6 = w_ada.shape
    rows = c_all.shape[0]
    return pl.pallas_call(
        _ada_kernel,
        grid=(depth, n6 // ADA_COLS),
        in_specs=[
            pl.BlockSpec((rows, d), lambda l, j: (0, 0)),
            pl.BlockSpec((1, d, ADA_COLS), lambda l, j: (l, 0, j)),
            pl.BlockSpec((1, 1, ADA_COLS), lambda l, j: (l, 0, j)),
        ],
        out_specs=pl.BlockSpec((1, rows, ADA_COLS), lambda l, j: (l, 0, j)),
        out_shape=jax.ShapeDtypeStruct((depth, rows, n6), F32),
        compiler_params=_params(("arbitrary", "arbitrary")),
        name="ada",
    )(c_all, w_ada, b_ada.reshape(depth, 1, n6))


def _inproj_kernel(x_ref, mod_ref, g_ref, w_ref, o_ref):
    rows = x_ref.shape[0]
    hn = _rms(x_ref[...], g_ref[...]) * (1.0 + _mod_row(mod_ref, 1, rows)) + _mod_row(mod_ref, 0, rows)
    o_ref[...] = jnp.dot(hn.astype(BF16), w_ref[...], preferred_element_type=F32)


def _inproj(x, mod, boff, g1, w_in, t):
    n, d = x.shape
    cols = w_in.shape[1]
    tm = INPROJ_TILE
    assert n % tm == 0
    return pl.pallas_call(
        _inproj_kernel,
        grid=(n // tm,),
        in_specs=[
            pl.BlockSpec((tm, d), lambda i: (i, 0)),
            _mod_spec(t, boff, d, tm),
            pl.BlockSpec((1, d), lambda i: (0, 0)),
            pl.BlockSpec((d, cols), lambda i: (0, 0)),
        ],
        out_specs=pl.BlockSpec((tm, cols), lambda i: (i, 0)),
        out_shape=jax.ShapeDtypeStruct((n, cols), F32),
        compiler_params=_params(("arbitrary",)),
        name="inproj",
    )(x, mod, g1, w_in)


def _rglru_kernel(xr_ref, yr_ref, cst_ref, h0_ref, cw_ref, cb_ref, wr_ref, br_ref, wi_ref, bi_ref,
                  lam_ref, o_ref, cnew_ref, hlast_ref, tail_s, h_s, *, tt):
    j = pl.program_id(1)
    w = xr_ref.shape[1]

    @pl.when(j == 0)
    def _():
        tail_s[...] = jnp.zeros((V7X_SUBLANES, w), F32)
        tail_s[V7X_SUBLANES - (CONV_W - 1):, :] = cst_ref[0]
        h_s[...] = h0_ref[0]

    for r0 in range(0, xr_ref.shape[0], tt):
        _rglru_tile(slice(r0, r0 + tt), xr_ref, yr_ref, cw_ref, cb_ref, wr_ref, br_ref, wi_ref, bi_ref,
                    lam_ref, o_ref, cnew_ref, hlast_ref, tail_s, h_s)


def _rglru_tile(rs, xr_ref, yr_ref, cw_ref, cb_ref, wr_ref, br_ref, wi_ref, bi_ref, lam_ref,
                o_ref, cnew_ref, hlast_ref, tail_s, h_s):
    tt = rs.stop - rs.start
    w = xr_ref.shape[1]
    x = xr_ref[rs, :]
    ext = jnp.concatenate([tail_s[...], x], axis=0)
    cw = cw_ref[...]
    xc = cb_ref[...] + x * cw[CONV_W - 1:CONV_W, :]
    for d in range(1, CONV_W):
        xs = pltpu.roll(ext, d, axis=0)[V7X_SUBLANES:, :]
        xc = xc + xs * cw[CONV_W - 1 - d:CONV_W - d, :]
    tail_s[...] = x[tt - V7X_SUBLANES:, :]
    cnew_ref[0] = x[tt - (CONV_W - 1):, :]

    xb = xc.astype(BF16)
    gr = jax.nn.sigmoid(jnp.dot(xb, wr_ref[...], preferred_element_type=F32) + br_ref[...])
    gi = jax.nn.sigmoid(jnp.dot(xb, wi_ref[...], preferred_element_type=F32) + bi_ref[...])
    nl = -lam_ref[...]
    sp = jnp.maximum(nl, 0.0) + jnp.log1p(jnp.exp(-jnp.abs(nl)))
    a = jnp.exp((-RG_C) * gr * sp)
    u = jnp.sqrt(1.0 - a * a) * gi * xc

    grp = V7X_SUBLANES
    ngrp = tt // grp
    a3 = a.reshape(ngrp, grp, w)
    u3 = u.reshape(ngrp, grp, w)
    sub = lax.broadcasted_iota(I32, (ngrp, grp, w), 1)
    d = 1
    while d < grp:
        keep = sub >= d
        u3 = jnp.where(keep, u3 + a3 * pltpu.roll(u3, d, axis=1), u3)
        a3 = jnp.where(keep, a3 * pltpu.roll(a3, d, axis=1), a3)
        d *= 2
    last = lambda x3: jnp.broadcast_to(x3[:, grp - 1:grp, :], x3.shape)
    a_end, u_end = last(a3), last(u3)
    carry = jnp.broadcast_to(h_s[...], (grp, w))
    h_groups = []
    for gi_ in range(ngrp):
        h_groups.append(u3[gi_] + a3[gi_] * carry)
        carry = u_end[gi_] + a_end[gi_] * carry
    h = jnp.concatenate(h_groups, axis=0)
    h_s[...] = carry[:1, :]
    hlast_ref[0] = carry[:1, :]
    o_ref[rs, :] = (jax.nn.gelu(yr_ref[rs, :], approximate=True) * h).astype(BF16)


def _seq_step_rows(t, tile):
    per = min(SEQ_TILES_PER_STEP, t // tile)
    while t % (tile * per):
        per -= 1
    return tile * per


def _rglru(proj, conv_st, h0, cw, cb, wr, br, wi, bi, lam, bsz, t):
    n = proj.shape[0]
    w = cw.shape[1]
    tile = min(t, ROW_TILE)
    assert tile >= V7X_SUBLANES and t % tile == 0
    tt = _seq_step_rows(t, tile)
    nt = t // tt
    full = lambda shape: pl.BlockSpec(shape, lambda b, j: (0,) * len(shape))
    return pl.pallas_call(
        functools.partial(_rglru_kernel, tt=tile),
        grid=(bsz, nt),
        in_specs=[
            pl.BlockSpec((tt, w), lambda b, j: (b * nt + j, 0)),
            pl.BlockSpec((tt, w), lambda b, j: (b * nt + j, 1)),
            pl.BlockSpec((1, CONV_W - 1, w), lambda b, j: (b, 0, 0)),
            pl.BlockSpec((1, 1, w), lambda b, j: (b, 0, 0)),
            full((CONV_W, w)), full((1, w)), full((w, w)), full((1, w)), full((w, w)), full((1, w)),
            full((1, w)),
        ],
        out_specs=[
            pl.BlockSpec((tt, w), lambda b, j: (b * nt + j, 0)),
            pl.BlockSpec((1, CONV_W - 1, w), lambda b, j: (b, 0, 0)),
            pl.BlockSpec((1, 1, w), lambda b, j: (b, 0, 0)),
        ],
        out_shape=[
            jax.ShapeDtypeStruct((n, w), BF16),
            jax.ShapeDtypeStruct((bsz, CONV_W - 1, w), F32),
            jax.ShapeDtypeStruct((bsz, 1, w), F32),
        ],
        scratch_shapes=[pltpu.VMEM((V7X_SUBLANES, w), F32), pltpu.VMEM((1, w), F32)],
        compiler_params=_params(("arbitrary", "arbitrary")),
        name="rglru",
    )(proj, proj, conv_st, h0.reshape(bsz, 1, w), cw, cb, wr, br, wi, bi, lam)


def _hgrn_tables(c):
    t = jnp.arange(c)[:, None]
    r = jnp.arange(c)[None, :]
    tri = (r <= t).astype(BF16)
    bd = min(c, V7X_LANES)
    t, r = t[:bd, :bd], r[:bd, :bd]
    masks = [(t == r)]
    h = 1
    while 2 * h <= bd:
        m_t = (t // (2 * h)) * (2 * h) + h
        same = (t // (2 * h)) == (r // (2 * h))
        masks.append(same & (t >= m_t) & (r < m_t))
        h *= 2
    return tri, jnp.stack(masks).astype(F32)


def _hgrn_kernel(q_ref, f_ref, v_ref, g_ref, lb_ref, ng_ref, s0_ref, tri_ref, mask_ref,
                 o_ref, slast_ref, s_s, acc_s, g_s, *, c, nlev):
    @pl.when(pl.program_id(1) == 0)
    def _():
        s_s[...] = s0_ref[0]

    for r0 in range(0, f_ref.shape[0], c):
        _hgrn_chunk(slice(r0, r0 + c), q_ref, f_ref, v_ref, g_ref, lb_ref, ng_ref, tri_ref, mask_ref,
                    o_ref, s_s, acc_s, g_s, nlev)
    slast_ref[0] = s_s[...]


def _hgrn_chunk(rs, q_ref, f_ref, v_ref, g_ref, lb_ref, ng_ref, tri_ref, mask_ref, o_ref,
                s_s, acc_s, g_s, nlev):
    c = rs.stop - rs.start
    w = f_ref.shape[1]
    ff = f_ref[rs, :]
    lb = lb_ref[...]
    log_sig = -(jnp.maximum(-ff, 0.0) + jnp.log1p(jnp.exp(-jnp.abs(ff))))
    log_f = log_sig + jnp.log1p(lb * jnp.exp(-ff))
    k = (1.0 - lb) * jax.nn.sigmoid(-ff)
    q = q_ref[rs, :]

    hi = log_f.astype(BF16)
    r1 = log_f - hi.astype(F32)
    mid = r1.astype(BF16)
    lo = (r1 - mid.astype(F32)).astype(BF16)
    tri = tri_ref[...]
    gcum = (jnp.dot(tri, hi, preferred_element_type=F32) + jnp.dot(tri, mid, preferred_element_type=F32)
            + jnp.dot(tri, lo, preferred_element_type=F32))
    gcum = gcum * LOG2_E
    g_s[...] = gcum

    row = lax.broadcasted_iota(I32, (c, w), 0)
    grp = V7X_SUBLANES
    groups = gcum.reshape(c // grp, grp, w)

    def group_row(r):
        return jnp.broadcast_to(groups[:, r:r + 1, :], groups.shape).reshape(c, w)

    def level_exponent(h):
        if h >= grp:
            parts = []
            for b in range(c // (2 * h)):
                lo0, m, hi0 = b * 2 * h, b * 2 * h + h, (b + 1) * 2 * h
                g_mid = g_s[m - 1:m, :]
                parts += [g_mid - gcum[lo0:m, :], gcum[m:hi0, :] - g_mid]
            return jnp.concatenate(parts, axis=0)
        if h == 1:
            g_mid = jnp.where((row & 1) == 1, pltpu.roll(gcum, 1, axis=0), gcum)
        else:
            g_mid = group_row(h - 1)
            for b in range(1, grp // (2 * h)):
                g_mid = jnp.where((row & (grp - 1)) >= b * 2 * h, group_row(b * 2 * h + h - 1), g_mid)
        return -jnp.abs(gcum - g_mid)

    bd = mask_ref.shape[1]

    def pair_scores(l, qw, kw):
        h = 0 if l == 0 else 2 ** (l - 1)
        keep = mask_ref[l] > 0.0 if 2 * h <= bd else None
        for hd in range(HG_HEADS):
            sl = slice(hd * HG_D, (hd + 1) * HG_D)
            if 2 * h <= bd:
                for b in range(c // bd):
                    rs = slice(b * bd, (b + 1) * bd)
                    p = lax.dot_general(qw[rs, sl], kw[rs, sl], NT_DIMS, preferred_element_type=F32)
                    acc_s[hd, rs, rs] = jnp.where(keep, p, 0.0 if l == 0 else acc_s[hd, rs, rs])
            else:
                for b in range(c // (2 * h)):
                    rows = slice(b * 2 * h + h, (b + 1) * 2 * h)
                    cols = slice(b * 2 * h, b * 2 * h + h)
                    acc_s[hd, rows, cols] = lax.dot_general(qw[rows, sl], kw[cols, sl], NT_DIMS,
                                                            preferred_element_type=F32)

    if c > bd:
        acc_s[...] = jnp.zeros(acc_s.shape, F32)
    qb = q.astype(BF16)
    kb = k.astype(BF16)
    pair_scores(0, qb, kb)
    for l in range(1, nlev + 1):
        wgt = jnp.exp2(level_exponent(2 ** (l - 1))).astype(BF16)
        pair_scores(l, qb * wgt, kb * wgt)

    glast = gcum[c - 1:, :]
    qg = (q * jnp.exp2(gcum)).astype(BF16)
    kdec = (k * jnp.exp2(glast - gcum)).astype(BF16)
    dlast = jnp.exp2(glast)
    vb = v_ref[rs, :].astype(BF16)
    for hd in range(HG_HEADS):
        sl = slice(hd * HG_D, (hd + 1) * HG_D)
        s_old = s_s[hd]
        o = (jnp.dot(acc_s[hd].astype(BF16), vb[:, sl], preferred_element_type=F32)
             + jnp.dot(qg[:, sl], s_old.astype(BF16), preferred_element_type=F32))
        kv = lax.dot_general(kdec[:, sl], vb[:, sl], TN_DIMS, preferred_element_type=F32)
        dcol = jnp.broadcast_to(dlast[:, sl], (HG_D, HG_D)).T
        s_s[hd] = dcol * s_old + kv
        gate = g_ref[rs, sl]
        o_ref[rs, sl] = (_rms(o, ng_ref[...]) * (gate * jax.nn.sigmoid(gate))).astype(BF16)


def _hgrn(proj, lb, ng, s0, bsz, t):
    n = proj.shape[0]
    w = HG_HEADS * HG_D
    c = min(t, HGRN_CHUNK)
    nlev = int(math.log2(c))
    assert 2 ** nlev == c and t % c == 0
    tri, masks = _hgrn_tables(c)
    tt = _seq_step_rows(t, c)
    nt = t // tt
    col = lambda k: pl.BlockSpec((tt, w), lambda b, j: (b * nt + j, k))
    full = lambda shape: pl.BlockSpec(shape, lambda b, j: (0,) * len(shape))
    return pl.pallas_call(
        functools.partial(_hgrn_kernel, c=c, nlev=nlev),
        grid=(bsz, nt),
        in_specs=[
            col(2), col(3), col(4), col(5),
            full((1, w)), full((1, HG_D)),
            pl.BlockSpec((1, HG_HEADS, HG_D, HG_D), lambda b, j: (b, 0, 0, 0)),
            full(tri.shape), full(masks.shape),
        ],
        out_specs=[
            pl.BlockSpec((tt, w), lambda b, j: (b * nt + j, 0)),
            pl.BlockSpec((1, HG_HEADS, HG_D, HG_D), lambda b, j: (b, 0, 0, 0)),
        ],
        out_shape=[
            jax.ShapeDtypeStruct((n, w), BF16),
            jax.ShapeDtypeStruct((bsz, HG_HEADS, HG_D, HG_D), F32),
        ],
        scratch_shapes=[pltpu.VMEM((HG_HEADS, HG_D, HG_D), F32),
                        pltpu.VMEM((HG_HEADS, c, c), F32),
                        pltpu.VMEM((c, w), F32)],
        compiler_params=_params(("arbitrary", "arbitrary")),
        name="hgrn",
    )(proj, proj, proj, proj, lb, ng, s0, tri, masks)


def _outproj_kernel(org_ref, ohg_ref, x_ref, mod_ref, wo_ref, g2_ref, wr_ref, brt_ref, tri_ref,
                    low_ref, cin_ref, x1_ref, hb_ref, pos_ref, prob_ref, segb_ref, segl_ref, cnt_ref,
                    carry_s, *, tm, parts):
    @pl.when(pl.program_id(0) == 0)
    def _():
        carry_s[...] = cin_ref[...]

    carry = carry_s[...]
    for part in range(parts):
        carry = _outproj_part(part, parts, tm, carry, org_ref, ohg_ref, x_ref, mod_ref, wo_ref,
                              g2_ref, wr_ref, brt_ref, tri_ref, low_ref, x1_ref, hb_ref, pos_ref,
                              prob_ref, segb_ref, segl_ref)
    carry_s[...] = carry
    cnt_ref[...] = carry


def _outproj_part(part, parts, tm, carry, org_ref, ohg_ref, x_ref, mod_ref, wo_ref, g2_ref, wr_ref,
                  brt_ref, tri_ref, low_ref, x1_ref, hb_ref, pos_ref, prob_ref, segb_ref, segl_ref):
    rs = slice(part * tm, (part + 1) * tm)
    mrow = lambda k: _mod_row(mod_ref, k, tm, part, parts)
    half = org_ref.shape[1]
    mix = (jnp.dot(org_ref[rs, :], wo_ref[:half, :], preferred_element_type=F32)
           + jnp.dot(ohg_ref[rs, :], wo_ref[half:, :], preferred_element_type=F32))
    x1 = x_ref[rs, :] + mrow(2) * mix
    x1_ref[rs, :] = x1
    hn = _rms(x1, g2_ref[...]) * (1.0 + mrow(4)) + mrow(3)
    hn_hi = hn.astype(BF16)
    hb_ref[rs, :] = hn_hi

    ne = brt_ref.shape[0]
    hn_lo = (hn - hn_hi.astype(F32)).astype(BF16)
    both = jnp.dot(hn_hi, wr_ref[...], preferred_element_type=F32)
    logits_tm = (both[:, :V7X_LANES] + both[:, V7X_LANES:]
                 + jnp.dot(hn_lo, wr_ref[:, :V7X_LANES], preferred_element_type=F32))
    logits = logits_tm.T[:ne, :] + brt_ref[...]
    eio = lax.broadcasted_iota(I32, (ne, tm), 0)
    cur = logits
    vals, sels = [], []
    for _ in range(TOP_K):
        mx = jnp.max(cur, axis=0, keepdims=True)
        ix = jnp.min(jnp.where(cur == mx, eio, ne), axis=0, keepdims=True)
        sel = eio == ix
        vals.append(mx)
        sels.append(sel)
        cur = jnp.where(sel, -jnp.inf, cur)
    ex = [jnp.exp(v - vals[0]) for v in vals]
    den = ex[0] + ex[1] + ex[2] + ex[3]
    probs = [e / den for e in ex]

    onehot = sels[0].astype(F32)
    for sel in sels[1:]:
        onehot = onehot + sel.astype(F32)
    incl = jnp.dot(onehot.astype(BF16), tri_ref[...], preferred_element_type=F32)
    run_len = jnp.ceil(incl[:, tm - 1:] * (1.0 / RUN_ALIGN)) * RUN_ALIGN
    run0 = jnp.dot(low_ref[...], jnp.broadcast_to(run_len, (ne, tm)).astype(BF16),
                   preferred_element_type=F32)
    local = run0 + incl - onehot
    slots = [jnp.sum(jnp.where(sel, local, 0.0), axis=0, keepdims=True) for sel in sels]
    segb_ref[part] = carry
    segl_ref[part] = run_len

    pos_ref[:, rs] = jnp.concatenate(slots, axis=0).astype(I32)
    sub = lax.broadcasted_iota(I32, (V7X_LANES, tm), 0)
    pt = jnp.zeros((V7X_LANES, tm), F32)
    for kk in range(TOP_K):
        pt = jnp.where(sub == kk, probs[kk], pt)
        pt = jnp.where(sub == TOP_K + kk, slots[kk], pt)
    prob_ref[rs, :] = pt.T
    return carry + run_len


def _router_weights(w_router):
    d, ne = w_router.shape
    assert ne <= V7X_LANES
    hi = w_router.astype(BF16)
    lo = (w_router - hi.astype(F32)).astype(BF16)
    pad = jnp.zeros((d, V7X_LANES - ne), BF16)
    return jnp.concatenate([hi, pad, lo, pad], axis=1)


def _outproj(o_rg, o_hg, x, mod, boff, w_out, g2, wr, brt, cnt_in, t):
    n, d = x.shape
    half = o_rg.shape[1]
    ne = brt.shape[0]
    tm = ROW_TILE
    parts = OUTPROJ_PARTS
    tile = tm * parts
    assert n % tile == 0 and tm <= 256
    nt = n // tm
    tri = (jnp.arange(tm)[:, None] <= jnp.arange(tm)[None, :]).astype(BF16)
    low = (jnp.arange(ne)[:, None] > jnp.arange(ne)[None, :]).astype(BF16)
    rowblk = lambda cols: pl.BlockSpec((tile, cols), lambda i: (i, 0))
    full = lambda shape: pl.BlockSpec(shape, lambda i: (0,) * len(shape))
    pertile = lambda: pl.BlockSpec((parts, ne, 1), lambda i: (i, 0, 0))
    return pl.pallas_call(
        functools.partial(_outproj_kernel, tm=tm, parts=parts),
        grid=(n // tile,),
        in_specs=[
            rowblk(half), rowblk(half), rowblk(d),
            _mod_spec(t, boff, d, tile),
            full((d, d)), full((1, d)), full(wr.shape), full((ne, 1)), full((tm, tm)), full((ne, ne)),
            full((ne, 1)),
        ],
        out_specs=[
            rowblk(d), rowblk(d), pl.BlockSpec((TOP_K, tile), lambda i: (0, i)), rowblk(V7X_LANES),
            pertile(), pertile(), full((ne, 1)),
        ],
        out_shape=[
            jax.ShapeDtypeStruct((n, d), F32),
            jax.ShapeDtypeStruct((n, d), BF16),
            jax.ShapeDtypeStruct((TOP_K, n), I32),
            jax.ShapeDtypeStruct((n, V7X_LANES), F32),
            jax.ShapeDtypeStruct((nt, ne, 1), F32),
            jax.ShapeDtypeStruct((nt, ne, 1), F32),
            jax.ShapeDtypeStruct((ne, 1), F32),
        ],
        scratch_shapes=[pltpu.VMEM((ne, 1), F32)],
        compiler_params=_params(("arbitrary",)),
        name="outproj",
    )(o_rg, o_hg, x, mod, w_out, g2, wr, brt, tri, low, cnt_in)


def _pow2_pieces(hi, lo=1):
    return tuple(1 << b for b in range(int(math.log2(hi)), int(math.log2(lo)) - 1, -1))


def _tile_slots(ne):
    worst = TOP_K * ROW_TILE + ne * (RUN_ALIGN - 1)
    return -(-worst // V7X_LANES) * V7X_LANES


def _stored_rows(ref, first_row, rows):
    start = first_row * ROW_PARTS
    if not isinstance(start, int):
        start = pl.multiple_of(start, V7X_SUBLANES)
    return ref.at[pl.ds(start, rows * ROW_PARTS), :]


def _store_packed(ref, words):
    rows = words.shape[0]
    for c in range(ROW_PARTS):
        ref[pl.ds(c, rows, stride=ROW_PARTS), :] = words[:, c * V7X_LANES:(c + 1) * V7X_LANES]


def _load_packed(ref):
    rows = ref.shape[0] // ROW_PARTS
    return jnp.concatenate([ref[pl.ds(c, rows, stride=ROW_PARTS), :] for c in range(ROW_PARTS)], axis=1)


def _copy_run(n, make_copy, wait, pieces):
    for rows in pieces:
        done = n & (-2 * rows)

        @pl.when((n & rows) != 0)
        def _():
            cp = make_copy(done, rows)
            cp.wait() if wait else cp.start()


RUN_PIECES = _pow2_pieces(ROW_TILE, RUN_ALIGN)
BUF_ROW_BITS = 11
LIST_HEAD = V7X_SUBLANES


def _run_lists(seg_start, seg_len, ne):
    assert _tile_slots(ne) <= 1 << BUF_ROW_BITS
    tiles = seg_len.shape[0]
    buf0 = jnp.cumsum(seg_len, axis=1) - seg_len
    j = jnp.arange(ne, dtype=I32)
    counts, entries = [], []
    for rows in RUN_PIECES:
        has = (seg_len & rows) != 0
        done = seg_len & (-2 * rows)
        packed = ((seg_start + done) << BUF_ROW_BITS) | (buf0 + done)
        rank = jnp.cumsum(has.astype(I32), axis=1) - 1
        place = has[:, :, None] & (rank[:, :, None] == j[None, None, :])
        entries.append(jnp.sum(jnp.where(place, packed[:, :, None], 0), axis=1))
        counts.append(jnp.sum(has.astype(I32), axis=1))
    head = jnp.stack(counts + [jnp.zeros((tiles,), I32)] * (LIST_HEAD - len(RUN_PIECES)), axis=1)
    flat = jnp.concatenate([head] + entries, axis=1)
    width = -(-flat.shape[1] // V7X_LANES) * V7X_LANES
    flat = jnp.pad(flat, ((0, 0), (0, width - flat.shape[1])))
    return flat.reshape(tiles, 1, width)


def _list_spec(ne, index_map):
    width = -(-(LIST_HEAD + len(RUN_PIECES) * ne) // V7X_LANES) * V7X_LANES
    return pl.BlockSpec((1, 1, width), index_map, memory_space=pltpu.SMEM)


def _tile_runs(list_ref, ne, make_copy, wait):
    for pi, rows in enumerate(RUN_PIECES):
        def one(jj, pi=pi, rows=rows):
            v = list_ref[0, 0, LIST_HEAD + pi * ne + jj]
            cp = make_copy(v >> BUF_ROW_BITS, v & ((1 << BUF_ROW_BITS) - 1), rows)
            cp.wait() if wait else cp.start()

        def pair(jj, carry, one=one):
            one(2 * jj)
            one(2 * jj + 1)
            return carry

        cnt = list_ref[0, 0, pi]
        lax.fori_loop(0, cnt >> 1, pair, 0)
        pl.when((cnt & 1) != 0)(functools.partial(one, cnt - 1))


def _pack_pairs(lo_cols, hi_cols):
    a = lax.bitcast_convert_type(lo_cols, jnp.uint32)
    b = lax.bitcast_convert_type(hi_cols, jnp.uint32)
    return (a & jnp.uint32(0xFFFF0000)) | (b >> 16)


def _unpack_pairs(words):
    a = lax.bitcast_convert_type(words & jnp.uint32(0xFFFF0000), F32).astype(BF16)
    b = lax.bitcast_convert_type(words << 16, F32).astype(BF16)
    return a, b


def _dispatch_kernel(zoff_ref, zcnt_ref, lst_ref, lst1_ref, lst2_ref, pos_ref, hba_ref, hbb_ref,
                     out_ref, buf_s, zero_s, sem, zsem, *, ne, tiles_a):
    i = pl.program_id(0)
    last = pl.num_programs(0) - 1
    slot = i % 2
    tt, d = hba_ref.shape
    dh = d // 2

    def runs(list_ref, s, wait):
        _tile_runs(list_ref, ne,
                   lambda g, b, rows: pltpu.make_async_copy(
                       _stored_rows(buf_s.at[s], b, rows), _stored_rows(out_ref, g, rows), sem.at[s]),
                   wait)

    @pl.when(i >= 2)
    def _():
        runs(lst2_ref, slot, True)

    pos = pos_ref[...]
    slot_row = lax.broadcasted_iota(I32, (buf_s.shape[1] // ROW_PARTS, tt), 0)
    hit = slot_row == pos[0:1, :]
    for kk in range(1, TOP_K):
        hit = hit | (slot_row == pos[kk:kk + 1, :])
    perm = jnp.where(hit, 1.0, 0.0).astype(BF16)
    hb = jnp.where(i < tiles_a, hba_ref[...], hbb_ref[...])
    _store_packed(buf_s.at[slot], _pack_pairs(jnp.dot(perm, hb[:, :dh], preferred_element_type=F32),
                                              jnp.dot(perm, hb[:, dh:], preferred_element_type=F32)))
    runs(lst_ref, slot, False)

    @pl.when(i == last)
    def _():
        @pl.when(i >= 1)
        def _():
            runs(lst1_ref, 1 - slot, True)
        runs(lst_ref, slot, True)

        zero_s[...] = jnp.zeros(zero_s.shape, zero_s.dtype)
        zrows = zero_s.shape[0] // ROW_PARTS
        zcopy = lambda g, rows: pltpu.make_async_copy(
            _stored_rows(zero_s, 0, rows), _stored_rows(out_ref, g, rows), zsem)

        def zero_run(e, carry):
            lo = zoff_ref[e] + zcnt_ref[e]
            n = zoff_ref[e + 1] - lo
            nfull = n // zrows
            rem = n - nfull * zrows
            for wait in (False, True):
                def full(c, c2):
                    cp = zcopy(lo + c * zrows, zrows)
                    cp.wait() if wait else cp.start()
                    return c2
                lax.fori_loop(0, nfull, full, 0)
                _copy_run(rem, lambda o, rows: zcopy(lo + nfull * zrows + o, rows), wait,
                          _pow2_pieces(zrows // 2, RUN_ALIGN))
            return carry

        lax.fori_loop(0, zcnt_ref.shape[0], zero_run, 0)


def _dispatch(lists, zoff, zcnt, pos_t, hb_a, hb_b, n_rows, ne):
    d = hb_a.shape[1]
    tt = ROW_TILE
    ta, tb = hb_a.shape[0] // tt, hb_b.shape[0] // tt
    assert hb_a.shape[0] % tt == 0 and hb_b.shape[0] % tt == 0
    back = lambda k: _list_spec(ne, lambda i, *_: (jnp.maximum(i - k, 0), 0, 0))
    return pl.pallas_call(
        functools.partial(_dispatch_kernel, ne=ne, tiles_a=ta),
        grid_spec=pltpu.PrefetchScalarGridSpec(
            num_scalar_prefetch=2,
            grid=(ta + tb,),
            in_specs=[back(0), back(1), back(2),
                      pl.BlockSpec((TOP_K, tt), lambda i, *_: (0, i)),
                      pl.BlockSpec((tt, d), lambda i, *_: (jnp.minimum(i, ta - 1), 0)),
                      pl.BlockSpec((tt, d), lambda i, *_: (jnp.maximum(i - ta, 0), 0))],
            out_specs=pl.BlockSpec(memory_space=pl.ANY),
            scratch_shapes=[pltpu.VMEM((2, _tile_slots(ne) * ROW_PARTS, V7X_LANES), jnp.uint32),
                            pltpu.VMEM((tt * ROW_PARTS, V7X_LANES), jnp.uint32),
                            pltpu.SemaphoreType.DMA((2,)), pltpu.SemaphoreType.DMA(())],
        ),
        out_shape=jax.ShapeDtypeStruct((n_rows * ROW_PARTS, V7X_LANES), jnp.uint32),
        compiler_params=_params(("arbitrary",)),
        name="dispatch",
    )(zoff, zcnt, lists, lists, lists, pos_t, hb_a, hb_b)


def _ffn_kernel(te_ref, tr_ref, nx_ref, xs_ref, wg_ref, bg_ref, wu_ref, bu_ref, wd_ref, bd_ref, y_ref,
                wg_s, wu_s, wd_s, stage_s, sem, *, layer):
    i = pl.program_id(0)
    rows = tr_ref[i]
    valid = rows > 0
    changed = (i == 0) | (te_ref[i] != te_ref[jnp.maximum(i - 1, 0)])
    weights = (wg_ref, wu_ref, wd_ref)

    def fetch(e):
        return [pltpu.make_async_copy(w_ref.at[layer, e], stage_s.at[k], sem.at[k])
                for k, w_ref in enumerate(weights)]

    @pl.when(valid & changed)
    def _():
        @pl.when(i == 0)
        def _():
            for cp in fetch(te_ref[i]):
                cp.start()
        for cp in fetch(te_ref[i]):
            cp.wait()
        wg_s[...] = stage_s[0].astype(BF16)
        wu_s[...] = stage_s[1].astype(BF16)
        wd_s[...] = stage_s[2].astype(BF16)

        @pl.when(nx_ref[i] >= 0)
        def _():
            for cp in fetch(nx_ref[i]):
                cp.start()

    full = xs_ref.shape[0] // ROW_PARTS
    half = full // 2
    dh = ROW_PARTS * xs_ref.shape[1]

    def run(n):
        stored = pl.ds(0, n * ROW_PARTS)
        xa, xb = _unpack_pairs(_load_packed(xs_ref.at[stored, :]))
        gate = (jnp.dot(xa, wg_s[:dh, :], preferred_element_type=F32)
                + jnp.dot(xb, wg_s[dh:, :], preferred_element_type=F32) + bg_ref[...])
        up = (jnp.dot(xa, wu_s[:dh, :], preferred_element_type=F32)
              + jnp.dot(xb, wu_s[dh:, :], preferred_element_type=F32) + bu_ref[...])
        gate = jnp.minimum(gate, SWIGLU_LIMIT)
        up = jnp.clip(up, -SWIGLU_LIMIT, SWIGLU_LIMIT)
        glu = gate * jax.nn.sigmoid(SWIGLU_ALPHA * gate)
        act = ((up + 1.0) * glu).astype(BF16)
        y = (jnp.dot(act, wd_s[...], preferred_element_type=F32) + bd_ref[...]).astype(BF16).astype(F32)
        yh = y.shape[1] // 2
        _store_packed(y_ref.at[stored, :], _pack_pairs(y[:, :yh], y[:, yh:]))

    pl.when(rows > half)(lambda: run(full))

    @pl.when(valid & (rows <= half))
    def _():
        run(half)
        y_ref[half * ROW_PARTS:, :] = jnp.zeros((half * ROW_PARTS, y_ref.shape[1]), y_ref.dtype)

    @pl.when(jnp.logical_not(valid))
    def _():
        y_ref[...] = jnp.zeros(y_ref.shape, y_ref.dtype)


def _ffn(layer, tile_expert, tile_rows, next_expert, xs, w_gate, b_gate, w_up, b_up, w_down, b_down):
    d, dff = w_gate.shape[2], w_gate.shape[3]
    assert d == dff
    tm = FFN_TILE * ROW_PARTS
    assert xs.shape[1] == V7X_LANES and xs.shape[0] % tm == 0
    bspec = lambda c: pl.BlockSpec((None, None, 1, c), lambda i, te, *_: (layer, te[i], 0, 0))
    hbm = pl.BlockSpec(memory_space=pl.ANY)
    return pl.pallas_call(
        functools.partial(_ffn_kernel, layer=layer),
        grid_spec=pltpu.PrefetchScalarGridSpec(
            num_scalar_prefetch=3,
            grid=(xs.shape[0] // tm,),
            in_specs=[
                pl.BlockSpec((tm, V7X_LANES), lambda i, *_: (i, 0)),
                hbm, bspec(dff), hbm, bspec(dff), hbm, bspec(d),
            ],
            out_specs=pl.BlockSpec((tm, V7X_LANES), lambda i, *_: (i, 0)),
            scratch_shapes=[pltpu.VMEM((d, dff), BF16), pltpu.VMEM((d, dff), BF16),
                            pltpu.VMEM((dff, d), BF16), pltpu.VMEM((3, d, dff), F32),
                            pltpu.SemaphoreType.DMA((3,))],
        ),
        out_shape=jax.ShapeDtypeStruct(xs.shape, jnp.uint32),
        compiler_params=_params(("arbitrary",)),
        name="ffn",
    )(tile_expert, tile_rows, next_expert, xs, w_gate, b_gate, w_up, b_up, w_down, b_down)


def _combine_kernel(lst_ref, nxt_ref, y_ref, prob_ref, x1_ref, mod_ref, gf_ref, o_ref,
                    buf_s, sem, *, ne, final):
    i = pl.program_id(0)
    slot = i % 2
    tt = x1_ref.shape[0]

    def runs(list_ref, s, wait):
        _tile_runs(list_ref, ne,
                   lambda g, b, rows: pltpu.make_async_copy(
                       _stored_rows(y_ref, g, rows), _stored_rows(buf_s.at[s], b, rows), sem.at[s]),
                   wait)

    @pl.when(i == 0)
    def _():
        buf_s[...] = jnp.zeros(buf_s.shape, buf_s.dtype)
        runs(lst_ref, slot, False)

    @pl.when(i + 1 < pl.num_programs(0))
    def _():
        runs(nxt_ref, 1 - slot, False)

    runs(lst_ref, slot, True)

    ya, yb = _unpack_pairs(_load_packed(buf_s.at[slot]))
    pp = prob_ref[...]
    slots = buf_s.shape[1] // ROW_PARTS
    slot_col = lax.broadcasted_iota(I32, (tt, slots), 1)
    wgt = jnp.zeros((tt, slots), F32)
    for kk in range(TOP_K):
        pos = pp[:, TOP_K + kk:TOP_K + kk + 1].astype(I32)
        wgt = jnp.where(slot_col == pos, pp[:, kk:kk + 1], wgt)
    wgt = wgt.astype(BF16)
    y = jnp.concatenate([jnp.dot(wgt, ya, preferred_element_type=F32),
                         jnp.dot(wgt, yb, preferred_element_type=F32)], axis=1)
    x2 = x1_ref[...] + _mod_row(mod_ref, 5, tt) * y
    if final:
        x2 = _rms(x2, gf_ref[...])
    o_ref[...] = x2


def _combine(lists, y_rows, prob, x1, mod, boff, g_final, roff, t, ne, final):
    n, d = x1.shape
    tt = ROW_TILE
    r0 = roff // tt
    assert roff % tt == 0 and n % tt == 0
    last = r0 + n // tt - 1
    return pl.pallas_call(
        functools.partial(_combine_kernel, ne=ne, final=final),
        grid=(n // tt,),
        in_specs=[
            _list_spec(ne, lambda i: (r0 + i, 0, 0)),
            _list_spec(ne, lambda i: (jnp.minimum(r0 + i + 1, last), 0, 0)),
            pl.BlockSpec(memory_space=pl.ANY),
            pl.BlockSpec((tt, V7X_LANES), lambda i: (i, 0)),
            pl.BlockSpec((tt, d), lambda i: (i, 0)),
            _mod_spec(t, boff, d),
            pl.BlockSpec((1, d), lambda i: (0, 0)),
        ],
        out_specs=pl.BlockSpec((tt, d), lambda i: (i, 0)),
        scratch_shapes=[pltpu.VMEM((2, _tile_slots(ne) * ROW_PARTS, V7X_LANES), jnp.uint32),
                        pltpu.SemaphoreType.DMA((2,))],
        out_shape=jax.ShapeDtypeStruct((n, d), F32),
        compiler_params=_params(("arbitrary",)),
        name="combine",
    )(lists, lists, y_rows, prob, x1, mod, g_final)


def _block_diag(w):
    depth, nb, c, _ = w.shape
    eye = jnp.eye(nb, dtype=w.dtype)
    return (eye[None, :, None, :, None] * w[:, :, :, None, :]).reshape(depth, nb * c, nb * c)


def kernel(x_prompt, x_sample, state_conv, state_rglru, state_hgrn, c_prompt, c_sample, w_ada, b_ada, g_norm1, g_norm2, w_in, conv_w, conv_b, w_rgate, b_rgate, w_igate, b_igate, lru_lambda, hgrn_lb, hgrn_norm_g, w_out, w_router, b_router, w_gate, b_gate, w_up, b_up, w_down, b_down, g_final):
    depth = w_ada.shape[0]
    bp, tp, d = x_prompt.shape
    bs, ts, _ = x_sample.shape
    w_rg = conv_w.shape[2]
    ne = w_router.shape[2]
    n_p, n_s = bp * tp, bs * ts
    n_tot = n_p + n_s
    span = max(max(ROW_TILE * OUTPROJ_PARTS, INPROJ_TILE) // ts, 1)
    boff_s = -(-bp // span) * span
    groups = ((bp, tp, 0, 0), (bs, ts, boff_s, n_p))
    n_c_pad = -(-(boff_s + bs) // V7X_SUBLANES) * V7X_SUBLANES
    c_all = jnp.concatenate([c_prompt, jnp.zeros((boff_s - bp, d), F32), c_sample,
                             jnp.zeros((n_c_pad - boff_s - bs, d), F32)], axis=0)
    mod_all = _ada(c_all, w_ada, b_ada).reshape(depth, n_c_pad, 6, d)

    p_lb = jax.nn.softmax(hgrn_lb.astype(F32), axis=0)
    lb_all = jnp.cumsum(p_lb, axis=0) - p_lb[0]

    w_in_b = w_in.astype(BF16)
    w_out_b = w_out.astype(BF16)
    wr_all = _block_diag(w_rgate).astype(BF16)
    wi_all = _block_diag(w_igate).astype(BF16)

    assert d // 2 == ROW_PARTS * V7X_LANES
    run_pad = (n_tot // ROW_TILE) * ne * (RUN_ALIGN - 1)
    n_tiles = -(-(TOP_K * n_tot + run_pad + ne * (FFN_TILE - 1)) // FFN_TILE)
    n_rows = n_tiles * FFN_TILE

    xs = [x_prompt.reshape(n_p, d), x_sample.reshape(n_s, d)]
    conv_in = [jnp.zeros((depth, bp, CONV_W - 1, w_rg), F32), state_conv]
    h_in = [jnp.zeros((depth, bp, w_rg), F32), state_rglru]
    s_in = [jnp.zeros((depth, bp, HG_HEADS, HG_D, HG_D), F32), state_hgrn]
    conv_out, h_out, s_out = ([], []), ([], []), ([], [])

    for l in range(depth):
        mod = mod_all[l]
        wr_d, wi_d = wr_all[l], wi_all[l]
        cnt = jnp.zeros((ne, 1), F32)
        x1s, hbs, poss, probs, segbs, segls = [], [], [], [], [], []
        for gi, (bsz, t, boff, _) in enumerate(groups):
            proj = _inproj(xs[gi], mod, boff, g_norm1[l][None], w_in_b[l], t)
            o_rg, c_new, h_last = _rglru(
                proj, conv_in[gi][l], h_in[gi][l], conv_w[l], conv_b[l][None], wr_d, b_rgate[l][None],
                wi_d, b_igate[l][None], lru_lambda[l][None], bsz, t)
            o_hg, s_last = _hgrn(proj, lb_all[l][None], hgrn_norm_g[l][None], s_in[gi][l], bsz, t)
            x1, hb, pos_t, prob, segb, segl, cnt = _outproj(
                o_rg, o_hg, xs[gi], mod, boff, w_out_b[l], g_norm2[l][None], _router_weights(w_router[l]),
                b_router[l][:, None], cnt, t)
            conv_out[gi].append(c_new)
            h_out[gi].append(h_last[:, 0])
            s_out[gi].append(s_last)
            x1s.append(x1)
            hbs.append(hb)
            poss.append(pos_t)
            probs.append(prob)
            segbs.append(segb[:, :, 0])
            segls.append(segl[:, :, 0])

        counts = cnt[:, 0].astype(I32)
        padded = (counts + FFN_TILE - 1) // FFN_TILE * FFN_TILE
        pad_end = jnp.cumsum(padded)
        off = jnp.concatenate([jnp.zeros((1,), I32), pad_end]).astype(I32)
        n_used = (pad_end[-1] // FFN_TILE).astype(I32)
        tile_start = jnp.minimum(jnp.arange(n_tiles, dtype=I32), n_used - 1) * FFN_TILE
        tile_expert = jnp.minimum(jnp.sum(tile_start[:, None] >= pad_end[None, :], axis=1), ne - 1).astype(I32)
        eidx = jnp.arange(ne, dtype=I32)
        of_tile = lambda table: jnp.sum(jnp.where(tile_expert[:, None] == eidx[None, :], table[None, :], 0), axis=1)
        tile_idx = jnp.arange(n_tiles, dtype=I32)
        tile_rows = jnp.where(
            tile_idx < n_used,
            jnp.clip(of_tile(counts) - (tile_idx * FFN_TILE - of_tile(off[:ne])), 0, FFN_TILE), 0).astype(I32)
        later_used = (counts[None, :] > 0) & (eidx[None, :] > eidx[:, None])
        next_used = jnp.min(jnp.where(later_used, eidx[None, :], ne), axis=1)
        next_expert = of_tile(jnp.where(next_used < ne, next_used, -1)).astype(I32)
        lists = _run_lists(off[None, :ne] + jnp.concatenate(segbs, axis=0).astype(I32),
                           jnp.concatenate(segls, axis=0).astype(I32), ne)
        zoff = jnp.concatenate([off, jnp.full((1,), n_rows, I32)])
        zcnt = jnp.concatenate([counts, jnp.zeros((1,), I32)])

        x_sorted = _dispatch(lists, zoff, zcnt, jnp.concatenate(poss, axis=1), hbs[0], hbs[1],
                             n_rows, ne)
        y_rows = _ffn(l, tile_expert, tile_rows, next_expert, x_sorted, w_gate, b_gate[:, :, None, :],
                      w_up, b_up[:, :, None, :], w_down, b_down[:, :, None, :])
        final = l == depth - 1
        xs = [_combine(lists, y_rows, probs[gi], x1s[gi], mod, boff, g_final[None],
                       roff, t, ne, final)
              for gi, (bsz, t, boff, roff) in enumerate(groups)]

    return (xs[0].reshape(bp, tp, d), xs[1].reshape(bs, ts, d),
            jnp.stack(conv_out[0]), jnp.stack(h_out[0]), jnp.stack(s_out[0]),
            jnp.stack(conv_out[1]), jnp.stack(h_out[1]), jnp.stack(s_out[1]))
```

```python
import functools
import math

import jax
import jax.numpy as jnp
from jax import lax
from jax.experimental import pallas as pl
from jax.experimental.pallas import tpu as pltpu

F32 = jnp.float32
BF16 = jnp.bfloat16
I32 = jnp.int32
HIGHEST = lax.Precision.HIGHEST

EPS = 1e-6
RG_C = 8.0
CONV_W = 4
HG_HEADS = 4
HG_D = 128
TOP_K = 4
SWIGLU_LIMIT = 7.0
SWIGLU_ALPHA = 1.702
LOG2_E = 1.4426950408889634

V7X_LANES = 128
V7X_SUBLANES = 8
V7X_VMEM_LIMIT_BYTES = 56 * 1024 * 1024

ROW_TILE = 256
INPROJ_TILE = 512
HGRN_CHUNK = 256
SEQ_TILES_PER_STEP = 2
OUTPROJ_PARTS = 2
FFN_TILE = 1024
ROW_PARTS = 4
RUN_ALIGN = V7X_SUBLANES // ROW_PARTS
ADA_COLS = 1536

NT_DIMS = (((1,), (1,)), ((), ()))
TN_DIMS = (((0,), (0,)), ((), ()))


def _params(sem, vmem=V7X_VMEM_LIMIT_BYTES):
    return pltpu.CompilerParams(dimension_semantics=sem, vmem_limit_bytes=vmem)


def _rms(x, g):
    ms = jnp.mean(x * x, axis=-1, keepdims=True)
    return x * lax.rsqrt(ms + EPS) * g


def _mod_row(mod_ref, k, rows, part=0, parts=1):
    nb, _, d = mod_ref.shape
    if nb == 1:
        return mod_ref[0, k:k + 1, :]
    nbp = nb // parts
    per = rows // nbp
    return jnp.concatenate([jnp.broadcast_to(mod_ref[b, k:k + 1, :], (per, d))
                            for b in range(part * nbp, (part + 1) * nbp)], axis=0)


def _mod_spec(t, boff, d, tile=None):
    tile = tile or ROW_TILE
    if t >= tile:
        assert t % tile == 0
        per = t // tile
        return pl.BlockSpec((1, 6, d), lambda i, *_: (boff + i // per, 0, 0))
    nb = tile // t
    assert tile % t == 0 and boff % nb == 0
    return pl.BlockSpec((nb, 6, d), lambda i, *_: (boff // nb + i, 0, 0))


def _ada_kernel(c_ref, w_ref, b_ref, o_ref):
    c = c_ref[...]
    s = c * jax.nn.sigmoid(c)
    o_ref[0] = jnp.dot(s, w_ref[0], precision=HIGHEST, preferred_element_type=F32) + b_ref[0]


def _ada(c_all, w_ada, b_ada):
    depth, d, n6 = w_ada.shape
    rows = c_all.shape[0]
    return pl.pallas_call(
        _ada_kernel,
        grid=(depth, n6 // ADA_COLS),
        in_specs=[
            pl.BlockSpec((rows, d), lambda l, j: (0, 0)),
            pl.BlockSpec((1, d, ADA_COLS), lambda l, j: (l, 0, j)),
            pl.BlockSpec((1, 1, ADA_COLS), lambda l, j: (l, 0, j)),
        ],
        out_specs=pl.BlockSpec((1, rows, ADA_COLS), lambda l, j: (l, 0, j)),
        out_shape=jax.ShapeDtypeStruct((depth, rows, n6), F32),
        compiler_params=_params(("arbitrary", "arbitrary")),
        name="ada",
    )(c_all, w_ada, b_ada.reshape(depth, 1, n6))


def _inproj_kernel(x_ref, mod_ref, g_ref, w_ref, o_ref):
    rows = x_ref.shape[0]
    hn = _rms(x_ref[...], g_ref[...]) * (1.0 + _mod_row(mod_ref, 1, rows)) + _mod_row(mod_ref, 0, rows)
    o_ref[...] = jnp.dot(hn.astype(BF16), w_ref[...], preferred_element_type=F32)


def _inproj(x, mod, boff, g1, w_in, t):
    n, d = x.shape
    cols = w_in.shape[1]
    tm = INPROJ_TILE
    assert n % tm == 0
    return pl.pallas_call(
        _inproj_kernel,
        grid=(n // tm,),
        in_specs=[
            pl.BlockSpec((tm, d), lambda i: (i, 0)),
            _mod_spec(t, boff, d, tm),
            pl.BlockSpec((1, d), lambda i: (0, 0)),
            pl.BlockSpec((d, cols), lambda i: (0, 0)),
        ],
        out_specs=pl.BlockSpec((tm, cols), lambda i: (i, 0)),
        out_shape=jax.ShapeDtypeStruct((n, cols), F32),
        compiler_params=_params(("arbitrary",)),
        name="inproj",
    )(x, mod, g1, w_in)


def _rglru_kernel(xr_ref, yr_ref, cst_ref, h0_ref, cw_ref, cb_ref, wr_ref, br_ref, wi_ref, bi_ref,
                  lam_ref, o_ref, cnew_ref, hlast_ref, tail_s, h_s, *, tt):
    j = pl.program_id(1)
    w = xr_ref.shape[1]

    @pl.when(j == 0)
    def _():
        tail_s[...] = jnp.zeros((V7X_SUBLANES, w), F32)
        tail_s[V7X_SUBLANES - (CONV_W - 1):, :] = cst_ref[0]
        h_s[...] = h0_ref[0]

    for r0 in range(0, xr_ref.shape[0], tt):
        _rglru_tile(slice(r0, r0 + tt), xr_ref, yr_ref, cw_ref, cb_ref, wr_ref, br_ref, wi_ref, bi_ref,
                    lam_ref, o_ref, cnew_ref, hlast_ref, tail_s, h_s)


def _rglru_tile(rs, xr_ref, yr_ref, cw_ref, cb_ref, wr_ref, br_ref, wi_ref, bi_ref, lam_ref,
                o_ref, cnew_ref, hlast_ref, tail_s, h_s):
    tt = rs.stop - rs.start
    w = xr_ref.shape[1]
    x = xr_ref[rs, :]
    ext = jnp.concatenate([tail_s[...], x], axis=0)
    cw = cw_ref[...]
    xc = cb_ref[...] + x * cw[CONV_W - 1:CONV_W, :]
    for d in range(1, CONV_W):
        xs = pltpu.roll(ext, d, axis=0)[V7X_SUBLANES:, :]
        xc = xc + xs * cw[CONV_W - 1 - d:CONV_W - d, :]
    tail_s[...] = x[tt - V7X_SUBLANES:, :]
    cnew_ref[0] = x[tt - (CONV_W - 1):, :]

    xb = xc.astype(BF16)
    gr = jax.nn.sigmoid(jnp.dot(xb, wr_ref[...], preferred_element_type=F32) + br_ref[...])
    gi = jax.nn.sigmoid(jnp.dot(xb, wi_ref[...], preferred_element_type=F32) + bi_ref[...])
    nl = -lam_ref[...]
    sp = jnp.maximum(nl, 0.0) + jnp.log1p(jnp.exp(-jnp.abs(nl)))
    a = jnp.exp((-RG_C) * gr * sp)
    u = jnp.sqrt(1.0 - a * a) * gi * xc

    grp = V7X_SUBLANES
    ngrp = tt // grp
    a3 = a.reshape(ngrp, grp, w)
    u3 = u.reshape(ngrp, grp, w)
    sub = lax.broadcasted_iota(I32, (ngrp, grp, w), 1)
    d = 1
    while d < grp:
        keep = sub >= d
        u3 = jnp.where(keep, u3 + a3 * pltpu.roll(u3, d, axis=1), u3)
        a3 = jnp.where(keep, a3 * pltpu.roll(a3, d, axis=1), a3)
        d *= 2
    last = lambda x3: jnp.broadcast_to(x3[:, grp - 1:grp, :], x3.shape)
    a_end, u_end = last(a3), last(u3)
    carry = jnp.broadcast_to(h_s[...], (grp, w))
    h_groups = []
    for gi_ in range(ngrp):
        h_groups.append(u3[gi_] + a3[gi_] * carry)
        carry = u_end[gi_] + a_end[gi_] * carry
    h = jnp.concatenate(h_groups, axis=0)
    h_s[...] = carry[:1, :]
    hlast_ref[0] = carry[:1, :]
    o_ref[rs, :] = (jax.nn.gelu(yr_ref[rs, :], approximate=True) * h).astype(BF16)


def _seq_step_rows(t, tile):
    per = min(SEQ_TILES_PER_STEP, t // tile)
    while t % (tile * per):
        per -= 1
    return tile * per


def _rglru(proj, conv_st, h0, cw, cb, wr, br, wi, bi, lam, bsz, t):
    n = proj.shape[0]
    w = cw.shape[1]
    tile = min(t, ROW_TILE)
    assert tile >= V7X_SUBLANES and t % tile == 0
    tt = _seq_step_rows(t, tile)
    nt = t // tt
    full = lambda shape: pl.BlockSpec(shape, lambda b, j: (0,) * len(shape))
    return pl.pallas_call(
        functools.partial(_rglru_kernel, tt=tile),
        grid=(bsz, nt),
        in_specs=[
            pl.BlockSpec((tt, w), lambda b, j: (b * nt + j, 0)),
            pl.BlockSpec((tt, w), lambda b, j: (b * nt + j, 1)),
            pl.BlockSpec((1, CONV_W - 1, w), lambda b, j: (b, 0, 0)),
            pl.BlockSpec((1, 1, w), lambda b, j: (b, 0, 0)),
            full((CONV_W, w)), full((1, w)), full((w, w)), full((1, w)), full((w, w)), full((1, w)),
            full((1, w)),
        ],
        out_specs=[
            pl.BlockSpec((tt, w), lambda b, j: (b * nt + j, 0)),
            pl.BlockSpec((1, CONV_W - 1, w), lambda b, j: (b, 0, 0)),
            pl.BlockSpec((1, 1, w), lambda b, j: (b, 0, 0)),
        ],
        out_shape=[
            jax.ShapeDtypeStruct((n, w), BF16),
            jax.ShapeDtypeStruct((bsz, CONV_W - 1, w), F32),
            jax.ShapeDtypeStruct((bsz, 1, w), F32),
        ],
        scratch_shapes=[pltpu.VMEM((V7X_SUBLANES, w), F32), pltpu.VMEM((1, w), F32)],
        compiler_params=_params(("arbitrary", "arbitrary")),
        name="rglru",
    )(proj, proj, conv_st, h0.reshape(bsz, 1, w), cw, cb, wr, br, wi, bi, lam)


def _hgrn_tables(c):
    t = jnp.arange(c)[:, None]
    r = jnp.arange(c)[None, :]
    tri = (r <= t).astype(BF16)
    bd = min(c, V7X_LANES)
    t, r = t[:bd, :bd], r[:bd, :bd]
    masks = [(t == r)]
    h = 1
    while 2 * h <= bd:
        m_t = (t // (2 * h)) * (2 * h) + h
        same = (t // (2 * h)) == (r // (2 * h))
        masks.append(same & (t >= m_t) & (r < m_t))
        h *= 2
    return tri, jnp.stack(masks).astype(F32)


def _hgrn_kernel(q_ref, f_ref, v_ref, g_ref, lb_ref, ng_ref, s0_ref, tri_ref, mask_ref,
                 o_ref, slast_ref, s_s, acc_s, g_s, *, c, nlev):
    @pl.when(pl.program_id(1) == 0)
    def _():
        s_s[...] = s0_ref[0]

    for r0 in range(0, f_ref.shape[0], c):
        _hgrn_chunk(slice(r0, r0 + c), q_ref, f_ref, v_ref, g_ref, lb_ref, ng_ref, tri_ref, mask_ref,
                    o_ref, s_s, acc_s, g_s, nlev)
    slast_ref[0] = s_s[...]


def _hgrn_chunk(rs, q_ref, f_ref, v_ref, g_ref, lb_ref, ng_ref, tri_ref, mask_ref, o_ref,
                s_s, acc_s, g_s, nlev):
    c = rs.stop - rs.start
    w = f_ref.shape[1]
    ff = f_ref[rs, :]
    lb = lb_ref[...]
    log_sig = -(jnp.maximum(-ff, 0.0) + jnp.log1p(jnp.exp(-jnp.abs(ff))))
    log_f = log_sig + jnp.log1p(lb * jnp.exp(-ff))
    k = (1.0 - lb) * jax.nn.sigmoid(-ff)
    q = q_ref[rs, :]

    hi = log_f.astype(BF16)
    r1 = log_f - hi.astype(F32)
    mid = r1.astype(BF16)
    lo = (r1 - mid.astype(F32)).astype(BF16)
    tri = tri_ref[...]
    gcum = (jnp.dot(tri, hi, preferred_element_type=F32) + jnp.dot(tri, mid, preferred_element_type=F32)
            + jnp.dot(tri, lo, preferred_element_type=F32))
    gcum = gcum * LOG2_E
    g_s[...] = gcum

    row = lax.broadcasted_iota(I32, (c, w), 0)
    grp = V7X_SUBLANES
    groups = gcum.reshape(c // grp, grp, w)

    def group_row(r):
        return jnp.broadcast_to(groups[:, r:r + 1, :], groups.shape).reshape(c, w)

    def level_exponent(h):
        if h >= grp:
            parts = []
            for b in range(c // (2 * h)):
                lo0, m, hi0 = b * 2 * h, b * 2 * h + h, (b + 1) * 2 * h
                g_mid = g_s[m - 1:m, :]
                parts += [g_mid - gcum[lo0:m, :], gcum[m:hi0, :] - g_mid]
            return jnp.concatenate(parts, axis=0)
        if h == 1:
            g_mid = jnp.where((row & 1) == 1, pltpu.roll(gcum, 1, axis=0), gcum)
        else:
            g_mid = group_row(h - 1)
            for b in range(1, grp // (2 * h)):
                g_mid = jnp.where((row & (grp - 1)) >= b * 2 * h, group_row(b * 2 * h + h - 1), g_mid)
        return -jnp.abs(gcum - g_mid)

    bd = mask_ref.shape[1]

    def pair_scores(l, qw, kw):
        h = 0 if l == 0 else 2 ** (l - 1)
        keep = mask_ref[l] > 0.0 if 2 * h <= bd else None
        for hd in range(HG_HEADS):
            sl = slice(hd * HG_D, (hd + 1) * HG_D)
            if 2 * h <= bd:
                for b in range(c // bd):
                    rs = slice(b * bd, (b + 1) * bd)
                    p = lax.dot_general(qw[rs, sl], kw[rs, sl], NT_DIMS, preferred_element_type=F32)
                    acc_s[hd, rs, rs] = jnp.where(keep, p, 0.0 if l == 0 else acc_s[hd, rs, rs])
            else:
                for b in range(c // (2 * h)):
                    rows = slice(b * 2 * h + h, (b + 1) * 2 * h)
                    cols = slice(b * 2 * h, b * 2 * h + h)
                    acc_s[hd, rows, cols] = lax.dot_general(qw[rows, sl], kw[cols, sl], NT_DIMS,
                                                            preferred_element_type=F32)

    if c > bd:
        acc_s[...] = jnp.zeros(acc_s.shape, F32)
    qb = q.astype(BF16)
    kb = k.astype(BF16)
    pair_scores(0, qb, kb)
    for l in range(1, nlev + 1):
        wgt = jnp.exp2(level_exponent(2 ** (l - 1))).astype(BF16)
        pair_scores(l, qb * wgt, kb * wgt)

    glast = gcum[c - 1:, :]
    qg = (q * jnp.exp2(gcum)).astype(BF16)
    kdec = (k * jnp.exp2(glast - gcum)).astype(BF16)
    dlast = jnp.exp2(glast)
    vb = v_ref[rs, :].astype(BF16)
    for hd in range(HG_HEADS):
        sl = slice(hd * HG_D, (hd + 1) * HG_D)
        s_old = s_s[hd]
        o = (jnp.dot(acc_s[hd].astype(BF16), vb[:, sl], preferred_element_type=F32)
             + jnp.dot(qg[:, sl], s_old.astype(BF16), preferred_element_type=F32))
        kv = lax.dot_general(kdec[:, sl], vb[:, sl], TN_DIMS, preferred_element_type=F32)
        dcol = jnp.broadcast_to(dlast[:, sl], (HG_D, HG_D)).T
        s_s[hd] = dcol * s_old + kv
        gate = g_ref[rs, sl]
        o_ref[rs, sl] = (_rms(o, ng_ref[...]) * (gate * jax.nn.sigmoid(gate))).astype(BF16)


def _hgrn(proj, lb, ng, s0, bsz, t):
    n = proj.shape[0]
    w = HG_HEADS * HG_D
    c = min(t, HGRN_CHUNK)
    nlev = int(math.log2(c))
    assert 2 ** nlev == c and t % c == 0
    tri, masks = _hgrn_tables(c)
    tt = _seq_step_rows(t, c)
    nt = t // tt
    col = lambda k: pl.BlockSpec((tt, w), lambda b, j: (b * nt + j, k))
    full = lambda shape: pl.BlockSpec(shape, lambda b, j: (0,) * len(shape))
    return pl.pallas_call(
        functools.partial(_hgrn_kernel, c=c, nlev=nlev),
        grid=(bsz, nt),
        in_specs=[
            col(2), col(3), col(4), col(5),
            full((1, w)), full((1, HG_D)),
            pl.BlockSpec((1, HG_HEADS, HG_D, HG_D), lambda b, j: (b, 0, 0, 0)),
            full(tri.shape), full(masks.shape),
        ],
        out_specs=[
            pl.BlockSpec((tt, w), lambda b, j: (b * nt + j, 0)),
            pl.BlockSpec((1, HG_HEADS, HG_D, HG_D), lambda b, j: (b, 0, 0, 0)),
        ],
        out_shape=[
            jax.ShapeDtypeStruct((n, w), BF16),
            jax.ShapeDtypeStruct((bsz, HG_HEADS, HG_D, HG_D), F32),
        ],
        scratch_shapes=[pltpu.VMEM((HG_HEADS, HG_D, HG_D), F32),
                        pltpu.VMEM((HG_HEADS, c, c), F32),
                        pltpu.VMEM((c, w), F32)],
        compiler_params=_params(("arbitrary", "arbitrary")),
        name="hgrn",
    )(proj, proj, proj, proj, lb, ng, s0, tri, masks)


def _outproj_kernel(org_ref, ohg_ref, x_ref, mod_ref, wo_ref, g2_ref, wr_ref, brt_ref, tri_ref,
                    low_ref, cin_ref, x1_ref, hb_ref, pos_ref, prob_ref, segb_ref, segl_ref, cnt_ref,
                    carry_s, *, tm, parts):
    @pl.when(pl.program_id(0) == 0)
    def _():
        carry_s[...] = cin_ref[...]

    carry = carry_s[...]
    for part in range(parts):
        carry = _outproj_part(part, parts, tm, carry, org_ref, ohg_ref, x_ref, mod_ref, wo_ref,
                              g2_ref, wr_ref, brt_ref, tri_ref, low_ref, x1_ref, hb_ref, pos_ref,
                              prob_ref, segb_ref, segl_ref)
    carry_s[...] = carry
    cnt_ref[...] = carry


def _outproj_part(part, parts, tm, carry, org_ref, ohg_ref, x_ref, mod_ref, wo_ref, g2_ref, wr_ref,
                  brt_ref, tri_ref, low_ref, x1_ref, hb_ref, pos_ref, prob_ref, segb_ref, segl_ref):
    rs = slice(part * tm, (part + 1) * tm)
    mrow = lambda k: _mod_row(mod_ref, k, tm, part, parts)
    half = org_ref.shape[1]
    mix = (jnp.dot(org_ref[rs, :], wo_ref[:half, :], preferred_element_type=F32)
           + jnp.dot(ohg_ref[rs, :], wo_ref[half:, :], preferred_element_type=F32))
    x1 = x_ref[rs, :] + mrow(2) * mix
    x1_ref[rs, :] = x1
    hn = _rms(x1, g2_ref[...]) * (1.0 + mrow(4)) + mrow(3)
    hn_hi = hn.astype(BF16)
    hb_ref[rs, :] = hn_hi

    ne = brt_ref.shape[0]
    hn_lo = (hn - hn_hi.astype(F32)).astype(BF16)
    both = jnp.dot(hn_hi, wr_ref[...], preferred_element_type=F32)
    logits_tm = (both[:, :V7X_LANES] + both[:, V7X_LANES:]
                 + jnp.dot(hn_lo, wr_ref[:, :V7X_LANES], preferred_element_type=F32))
    logits = logits_tm.T[:ne, :] + brt_ref[...]
    eio = lax.broadcasted_iota(I32, (ne, tm), 0)
    cur = logits
    vals, sels = [], []
    for _ in range(TOP_K):
        mx = jnp.max(cur, axis=0, keepdims=True)
        ix = jnp.min(jnp.where(cur == mx, eio, ne), axis=0, keepdims=True)
        sel = eio == ix
        vals.append(mx)
        sels.append(sel)
        cur = jnp.where(sel, -jnp.inf, cur)
    ex = [jnp.exp(v - vals[0]) for v in vals]
    den = ex[0] + ex[1] + ex[2] + ex[3]
    probs = [e / den for e in ex]

    onehot = sels[0].astype(F32)
    for sel in sels[1:]:
        onehot = onehot + sel.astype(F32)
    incl = jnp.dot(onehot.astype(BF16), tri_ref[...], preferred_element_type=F32)
    run_len = jnp.ceil(incl[:, tm - 1:] * (1.0 / RUN_ALIGN)) * RUN_ALIGN
    run0 = jnp.dot(low_ref[...], jnp.broadcast_to(run_len, (ne, tm)).astype(BF16),
                   preferred_element_type=F32)
    local = run0 + incl - onehot
    slots = [jnp.sum(jnp.where(sel, local, 0.0), axis=0, keepdims=True) for sel in sels]
    segb_ref[part] = carry
    segl_ref[part] = run_len

    pos_ref[:, rs] = jnp.concatenate(slots, axis=0).astype(I32)
    sub = lax.broadcasted_iota(I32, (V7X_LANES, tm), 0)
    pt = jnp.zeros((V7X_LANES, tm), F32)
    for kk in range(TOP_K):
        pt = jnp.where(sub == kk, probs[kk], pt)
        pt = jnp.where(sub == TOP_K + kk, slots[kk], pt)
    prob_ref[rs, :] = pt.T
    return carry + run_len


def _router_weights(w_router):
    d, ne = w_router.shape
    assert ne <= V7X_LANES
    hi = w_router.astype(BF16)
    lo = (w_router - hi.astype(F32)).astype(BF16)
    pad = jnp.zeros((d, V7X_LANES - ne), BF16)
    return jnp.concatenate([hi, pad, lo, pad], axis=1)


def _outproj(o_rg, o_hg, x, mod, boff, w_out, g2, wr, brt, cnt_in, t):
    n, d = x.shape
    half = o_rg.shape[1]
    ne = brt.shape[0]
    tm = ROW_TILE
    parts = OUTPROJ_PARTS
    tile = tm * parts
    assert n % tile == 0 and tm <= 256
    nt = n // tm
    tri = (jnp.arange(tm)[:, None] <= jnp.arange(tm)[None, :]).astype(BF16)
    low = (jnp.arange(ne)[:, None] > jnp.arange(ne)[None, :]).astype(BF16)
    rowblk = lambda cols: pl.BlockSpec((tile, cols), lambda i: (i, 0))
    full = lambda shape: pl.BlockSpec(shape, lambda i: (0,) * len(shape))
    pertile = lambda: pl.BlockSpec((parts, ne, 1), lambda i: (i, 0, 0))
    return pl.pallas_call(
        functools.partial(_outproj_kernel, tm=tm, parts=parts),
        grid=(n // tile,),
        in_specs=[
            rowblk(half), rowblk(half), rowblk(d),
            _mod_spec(t, boff, d, tile),
            full((d, d)), full((1, d)), full(wr.shape), full((ne, 1)), full((tm, tm)), full((ne, ne)),
            full((ne, 1)),
        ],
        out_specs=[
            rowblk(d), rowblk(d), pl.BlockSpec((TOP_K, tile), lambda i: (0, i)), rowblk(V7X_LANES),
            pertile(), pertile(), full((ne, 1)),
        ],
        out_shape=[
            jax.ShapeDtypeStruct((n, d), F32),
            jax.ShapeDtypeStruct((n, d), BF16),
            jax.ShapeDtypeStruct((TOP_K, n), I32),
            jax.ShapeDtypeStruct((n, V7X_LANES), F32),
            jax.ShapeDtypeStruct((nt, ne, 1), F32),
            jax.ShapeDtypeStruct((nt, ne, 1), F32),
            jax.ShapeDtypeStruct((ne, 1), F32),
        ],
        scratch_shapes=[pltpu.VMEM((ne, 1), F32)],
        compiler_params=_params(("arbitrary",)),
        name="outproj",
    )(o_rg, o_hg, x, mod, w_out, g2, wr, brt, tri, low, cnt_in)


def _pow2_pieces(hi, lo=1):
    return tuple(1 << b for b in range(int(math.log2(hi)), int(math.log2(lo)) - 1, -1))


def _tile_slots(ne):
    worst = TOP_K * ROW_TILE + ne * (RUN_ALIGN - 1)
    return -(-worst // V7X_LANES) * V7X_LANES


def _stored_rows(ref, first_row, rows):
    start = first_row * ROW_PARTS
    if not isinstance(start, int):
        start = pl.multiple_of(start, V7X_SUBLANES)
    return ref.at[pl.ds(start, rows * ROW_PARTS), :]


def _store_packed(ref, words):
    rows = words.shape[0]
    for c in range(ROW_PARTS):
        ref[pl.ds(c, rows, stride=ROW_PARTS), :] = words[:, c * V7X_LANES:(c + 1) * V7X_LANES]


def _load_packed(ref):
    rows = ref.shape[0] // ROW_PARTS
    return jnp.concatenate([ref[pl.ds(c, rows, stride=ROW_PARTS), :] for c in range(ROW_PARTS)], axis=1)


def _copy_run(n, make_copy, wait, pieces):
    for rows in pieces:
        done = n & (-2 * rows)

        @pl.when((n & rows) != 0)
        def _():
            cp = make_copy(done, rows)
            cp.wait() if wait else cp.start()


RUN_PIECES = _pow2_pieces(ROW_TILE, RUN_ALIGN)
BUF_ROW_BITS = 11
LIST_HEAD = V7X_SUBLANES


def _run_lists(seg_start, seg_len, ne):
    assert _tile_slots(ne) <= 1 << BUF_ROW_BITS
    tiles = seg_len.shape[0]
    buf0 = jnp.cumsum(seg_len, axis=1) - seg_len
    j = jnp.arange(ne, dtype=I32)
    counts, entries = [], []
    for rows in RUN_PIECES:
        has = (seg_len & rows) != 0
        done = seg_len & (-2 * rows)
        packed = ((seg_start + done) << BUF_ROW_BITS) | (buf0 + done)
        rank = jnp.cumsum(has.astype(I32), axis=1) - 1
        place = has[:, :, None] & (rank[:, :, None] == j[None, None, :])
        entries.append(jnp.sum(jnp.where(place, packed[:, :, None], 0), axis=1))
        counts.append(jnp.sum(has.astype(I32), axis=1))
    head = jnp.stack(counts + [jnp.zeros((tiles,), I32)] * (LIST_HEAD - len(RUN_PIECES)), axis=1)
    flat = jnp.concatenate([head] + entries, axis=1)
    width = -(-flat.shape[1] // V7X_LANES) * V7X_LANES
    flat = jnp.pad(flat, ((0, 0), (0, width - flat.shape[1])))
    return flat.reshape(tiles, 1, width)


def _list_spec(ne, index_map):
    width = -(-(LIST_HEAD + len(RUN_PIECES) * ne) // V7X_LANES) * V7X_LANES
    return pl.BlockSpec((1, 1, width), index_map, memory_space=pltpu.SMEM)


def _tile_runs(list_ref, ne, make_copy, wait):
    for pi, rows in enumerate(RUN_PIECES):
        def one(jj, pi=pi, rows=rows):
            v = list_ref[0, 0, LIST_HEAD + pi * ne + jj]
            cp = make_copy(v >> BUF_ROW_BITS, v & ((1 << BUF_ROW_BITS) - 1), rows)
            cp.wait() if wait else cp.start()

        def pair(jj, carry, one=one):
            one(2 * jj)
            one(2 * jj + 1)
            return carry

        cnt = list_ref[0, 0, pi]
        lax.fori_loop(0, cnt >> 1, pair, 0)
        pl.when((cnt & 1) != 0)(functools.partial(one, cnt - 1))


def _pack_pairs(lo_cols, hi_cols):
    a = lax.bitcast_convert_type(lo_cols, jnp.uint32)
    b = lax.bitcast_convert_type(hi_cols, jnp.uint32)
    return (a & jnp.uint32(0xFFFF0000)) | (b >> 16)


def _unpack_pairs(words):
    a = lax.bitcast_convert_type(words & jnp.uint32(0xFFFF0000), F32).astype(BF16)
    b = lax.bitcast_convert_type(words << 16, F32).astype(BF16)
    return a, b


def _dispatch_kernel(zoff_ref, zcnt_ref, lst_ref, lst1_ref, lst2_ref, pos_ref, hba_ref, hbb_ref,
                     out_ref, buf_s, zero_s, sem, zsem, *, ne, tiles_a):
    i = pl.program_id(0)
    last = pl.num_programs(0) - 1
    slot = i % 2
    tt, d = hba_ref.shape
    dh = d // 2

    def runs(list_ref, s, wait):
        _tile_runs(list_ref, ne,
                   lambda g, b, rows: pltpu.make_async_copy(
                       _stored_rows(buf_s.at[s], b, rows), _stored_rows(out_ref, g, rows), sem.at[s]),
                   wait)

    @pl.when(i >= 2)
    def _():
        runs(lst2_ref, slot, True)

    pos = pos_ref[...]
    slot_row = lax.broadcasted_iota(I32, (buf_s.shape[1] // ROW_PARTS, tt), 0)
    hit = slot_row == pos[0:1, :]
    for kk in range(1, TOP_K):
        hit = hit | (slot_row == pos[kk:kk + 1, :])
    perm = jnp.where(hit, 1.0, 0.0).astype(BF16)
    hb = jnp.where(i < tiles_a, hba_ref[...], hbb_ref[...])
    _store_packed(buf_s.at[slot], _pack_pairs(jnp.dot(perm, hb[:, :dh], preferred_element_type=F32),
                                              jnp.dot(perm, hb[:, dh:], preferred_element_type=F32)))
    runs(lst_ref, slot, False)

    @pl.when(i == last)
    def _():
        @pl.when(i >= 1)
        def _():
            runs(lst1_ref, 1 - slot, True)
        runs(lst_ref, slot, True)

        zero_s[...] = jnp.zeros(zero_s.shape, zero_s.dtype)
        zrows = zero_s.shape[0] // ROW_PARTS
        zcopy = lambda g, rows: pltpu.make_async_copy(
            _stored_rows(zero_s, 0, rows), _stored_rows(out_ref, g, rows), zsem)

        def zero_run(e, carry):
            lo = zoff_ref[e] + zcnt_ref[e]
            n = zoff_ref[e + 1] - lo
            nfull = n // zrows
            rem = n - nfull * zrows
            for wait in (False, True):
                def full(c, c2):
                    cp = zcopy(lo + c * zrows, zrows)
                    cp.wait() if wait else cp.start()
                    return c2
                lax.fori_loop(0, nfull, full, 0)
                _copy_run(rem, lambda o, rows: zcopy(lo + nfull * zrows + o, rows), wait,
                          _pow2_pieces(zrows // 2, RUN_ALIGN))
            return carry

        lax.fori_loop(0, zcnt_ref.shape[0], zero_run, 0)


def _dispatch(lists, zoff, zcnt, pos_t, hb_a, hb_b, n_rows, ne):
    d = hb_a.shape[1]
    tt = ROW_TILE
    ta, tb = hb_a.shape[0] // tt, hb_b.shape[0] // tt
    assert hb_a.shape[0] % tt == 0 and hb_b.shape[0] % tt == 0
    back = lambda k: _list_spec(ne, lambda i, *_: (jnp.maximum(i - k, 0), 0, 0))
    return pl.pallas_call(
        functools.partial(_dispatch_kernel, ne=ne, tiles_a=ta),
        grid_spec=pltpu.PrefetchScalarGridSpec(
            num_scalar_prefetch=2,
            grid=(ta + tb,),
            in_specs=[back(0), back(1), back(2),
                      pl.BlockSpec((TOP_K, tt), lambda i, *_: (0, i)),
                      pl.BlockSpec((tt, d), lambda i, *_: (jnp.minimum(i, ta - 1), 0)),
                      pl.BlockSpec((tt, d), lambda i, *_: (jnp.maximum(i - ta, 0), 0))],
            out_specs=pl.BlockSpec(memory_space=pl.ANY),
            scratch_shapes=[pltpu.VMEM((2, _tile_slots(ne) * ROW_PARTS, V7X_LANES), jnp.uint32),
                            pltpu.VMEM((tt * ROW_PARTS, V7X_LANES), jnp.uint32),
                            pltpu.SemaphoreType.DMA((2,)), pltpu.SemaphoreType.DMA(())],
        ),
        out_shape=jax.ShapeDtypeStruct((n_rows * ROW_PARTS, V7X_LANES), jnp.uint32),
        compiler_params=_params(("arbitrary",)),
        name="dispatch",
    )(zoff, zcnt, lists, lists, lists, pos_t, hb_a, hb_b)


def _ffn_kernel(te_ref, tr_ref, nx_ref, xs_ref, wg_ref, bg_ref, wu_ref, bu_ref, wd_ref, bd_ref, y_ref,
                wg_s, wu_s, wd_s, stage_s, sem, *, layer):
    i = pl.program_id(0)
    rows = tr_ref[i]
    valid = rows > 0
    changed = (i == 0) | (te_ref[i] != te_ref[jnp.maximum(i - 1, 0)])
    weights = (wg_ref, wu_ref, wd_ref)

    def fetch(e):
        return [pltpu.make_async_copy(w_ref.at[layer, e], stage_s.at[k], sem.at[k])
                for k, w_ref in enumerate(weights)]

    @pl.when(valid & changed)
    def _():
        @pl.when(i == 0)
        def _():
            for cp in fetch(te_ref[i]):
                cp.start()
        for cp in fetch(te_ref[i]):
            cp.wait()
        wg_s[...] = stage_s[0].astype(BF16)
        wu_s[...] = stage_s[1].astype(BF16)
        wd_s[...] = stage_s[2].astype(BF16)

        @pl.when(nx_ref[i] >= 0)
        def _():
            for cp in fetch(nx_ref[i]):
                cp.start()

    full = xs_ref.shape[0] // ROW_PARTS
    half = full // 2
    dh = ROW_PARTS * xs_ref.shape[1]

    def run(n):
        stored = pl.ds(0, n * ROW_PARTS)
        xa, xb = _unpack_pairs(_load_packed(xs_ref.at[stored, :]))
        gate = (jnp.dot(xa, wg_s[:dh, :], preferred_element_type=F32)
                + jnp.dot(xb, wg_s[dh:, :], preferred_element_type=F32) + bg_ref[...])
        up = (jnp.dot(xa, wu_s[:dh, :], preferred_element_type=F32)
              + jnp.dot(xb, wu_s[dh:, :], preferred_element_type=F32) + bu_ref[...])
        gate = jnp.minimum(gate, SWIGLU_LIMIT)
        up = jnp.clip(up, -SWIGLU_LIMIT, SWIGLU_LIMIT)
        glu = gate * jax.nn.sigmoid(SWIGLU_ALPHA * gate)
        act = ((up + 1.0) * glu).astype(BF16)
        y = (jnp.dot(act, wd_s[...], preferred_element_type=F32) + bd_ref[...]).astype(BF16).astype(F32)
        yh = y.shape[1] // 2
        _store_packed(y_ref.at[stored, :], _pack_pairs(y[:, :yh], y[:, yh:]))

    pl.when(rows > half)(lambda: run(full))

    @pl.when(valid & (rows <= half))
    def _():
        run(half)
        y_ref[half * ROW_PARTS:, :] = jnp.zeros((half * ROW_PARTS, y_ref.shape[1]), y_ref.dtype)

    @pl.when(jnp.logical_not(valid))
    def _():
        y_ref[...] = jnp.zeros(y_ref.shape, y_ref.dtype)


def _ffn(layer, tile_expert, tile_rows, next_expert, xs, w_gate, b_gate, w_up, b_up, w_down, b_down):
    d, dff = w_gate.shape[2], w_gate.shape[3]
    assert d == dff
    tm = FFN_TILE * ROW_PARTS
    assert xs.shape[1] == V7X_LANES and xs.shape[0] % tm == 0
    bspec = lambda c: pl.BlockSpec((None, None, 1, c), lambda i, te, *_: (layer, te[i], 0, 0))
    hbm = pl.BlockSpec(memory_space=pl.ANY)
    return pl.pallas_call(
        functools.partial(_ffn_kernel, layer=layer),
        grid_spec=pltpu.PrefetchScalarGridSpec(
            num_scalar_prefetch=3,
            grid=(xs.shape[0] // tm,),
            in_specs=[
                pl.BlockSpec((tm, V7X_LANES), lambda i, *_: (i, 0)),
                hbm, bspec(dff), hbm, bspec(dff), hbm, bspec(d),
            ],
            out_specs=pl.BlockSpec((tm, V7X_LANES), lambda i, *_: (i, 0)),
            scratch_shapes=[pltpu.VMEM((d, dff), BF16), pltpu.VMEM((d, dff), BF16),
                            pltpu.VMEM((dff, d), BF16), pltpu.VMEM((3, d, dff), F32),
                            pltpu.SemaphoreType.DMA((3,))],
        ),
        out_shape=jax.ShapeDtypeStruct(xs.shape, jnp.uint32),
        compiler_params=_params(("arbitrary",)),
        name="ffn",
    )(tile_expert, tile_rows, next_expert, xs, w_gate, b_gate, w_up, b_up, w_down, b_down)


def _combine_kernel(lst_ref, nxt_ref, y_ref, prob_ref, x1_ref, mod_ref, gf_ref, o_ref,
                    buf_s, sem, *, ne, final):
    i = pl.program_id(0)
    slot = i % 2
    tt = x1_ref.shape[0]

    def runs(list_ref, s, wait):
        _tile_runs(list_ref, ne,
                   lambda g, b, rows: pltpu.make_async_copy(
                       _stored_rows(y_ref, g, rows), _stored_rows(buf_s.at[s], b, rows), sem.at[s]),
                   wait)

    @pl.when(i == 0)
    def _():
        buf_s[...] = jnp.zeros(buf_s.shape, buf_s.dtype)
        runs(lst_ref, slot, False)

    @pl.when(i + 1 < pl.num_programs(0))
    def _():
        runs(nxt_ref, 1 - slot, False)

    runs(lst_ref, slot, True)

    ya, yb = _unpack_pairs(_load_packed(buf_s.at[slot]))
    pp = prob_ref[...]
    slots = buf_s.shape[1] // ROW_PARTS
    slot_col = lax.broadcasted_iota(I32, (tt, slots), 1)
    wgt = jnp.zeros((tt, slots), F32)
    for kk in range(TOP_K):
        pos = pp[:, TOP_K + kk:TOP_K + kk + 1].astype(I32)
        wgt = jnp.where(slot_col == pos, pp[:, kk:kk + 1], wgt)
    wgt = wgt.astype(BF16)
    y = jnp.concatenate([jnp.dot(wgt, ya, preferred_element_type=F32),
                         jnp.dot(wgt, yb, preferred_element_type=F32)], axis=1)
    x2 = x1_ref[...] + _mod_row(mod_ref, 5, tt) * y
    if final:
        x2 = _rms(x2, gf_ref[...])
    o_ref[...] = x2


def _combine(lists, y_rows, prob, x1, mod, boff, g_final, roff, t, ne, final):
    n, d = x1.shape
    tt = ROW_TILE
    r0 = roff // tt
    assert roff % tt == 0 and n % tt == 0
    last = r0 + n // tt - 1
    return pl.pallas_call(
        functools.partial(_combine_kernel, ne=ne, final=final),
        grid=(n // tt,),
        in_specs=[
            _list_spec(ne, lambda i: (r0 + i, 0, 0)),
            _list_spec(ne, lambda i: (jnp.minimum(r0 + i + 1, last), 0, 0)),
            pl.BlockSpec(memory_space=pl.ANY),
            pl.BlockSpec((tt, V7X_LANES), lambda i: (i, 0)),
            pl.BlockSpec((tt, d), lambda i: (i, 0)),
            _mod_spec(t, boff, d),
            pl.BlockSpec((1, d), lambda i: (0, 0)),
        ],
        out_specs=pl.BlockSpec((tt, d), lambda i: (i, 0)),
        scratch_shapes=[pltpu.VMEM((2, _tile_slots(ne) * ROW_PARTS, V7X_LANES), jnp.uint32),
                        pltpu.SemaphoreType.DMA((2,))],
        out_shape=jax.ShapeDtypeStruct((n, d), F32),
        compiler_params=_params(("arbitrary",)),
        name="combine",
    )(lists, lists, y_rows, prob, x1, mod, g_final)


def _block_diag(w):
    depth, nb, c, _ = w.shape
    eye = jnp.eye(nb, dtype=w.dtype)
    return (eye[None, :, None, :, None] * w[:, :, :, None, :]).reshape(depth, nb * c, nb * c)


def kernel(x_prompt, x_sample, state_conv, state_rglru, state_hgrn, c_prompt, c_sample, w_ada, b_ada, g_norm1, g_norm2, w_in, conv_w, conv_b, w_rgate, b_rgate, w_igate, b_igate, lru_lambda, hgrn_lb, hgrn_norm_g, w_out, w_router, b_router, w_gate, b_gate, w_up, b_up, w_down, b_down, g_final):
    depth = w_ada.shape[0]
    bp, tp, d = x_prompt.shape
    bs, ts, _ = x_sample.shape
    w_rg = conv_w.shape[2]
    ne = w_router.shape[2]
    n_p, n_s = bp * tp, bs * ts
    n_tot = n_p + n_s
    span = max(max(ROW_TILE * OUTPROJ_PARTS, INPROJ_TILE) // ts, 1)
    boff_s = -(-bp // span) * span
    groups = ((bp, tp, 0, 0), (bs, ts, boff_s, n_p))
    n_c_pad = -(-(boff_s + bs) // V7X_SUBLANES) * V7X_SUBLANES
    c_all = jnp.concatenate([c_prompt, jnp.zeros((boff_s - bp, d), F32), c_sample,
                             jnp.zeros((n_c_pad - boff_s - bs, d), F32)], axis=0)
    mod_all = _ada(c_all, w_ada, b_ada).reshape(depth, n_c_pad, 6, d)

    p_lb = jax.nn.softmax(hgrn_lb.astype(F32), axis=0)
    lb_all = jnp.cumsum(p_lb, axis=0) - p_lb[0]

    w_in_b = w_in.astype(BF16)
    w_out_b = w_out.astype(BF16)
    wr_all = _block_diag(w_rgate).astype(BF16)
    wi_all = _block_diag(w_igate).astype(BF16)

    assert d // 2 == ROW_PARTS * V7X_LANES
    run_pad = (n_tot // ROW_TILE) * ne * (RUN_ALIGN - 1)
    n_tiles = -(-(TOP_K * n_tot + run_pad + ne * (FFN_TILE - 1)) // FFN_TILE)
    n_rows = n_tiles * FFN_TILE

    xs = [x_prompt.reshape(n_p, d), x_sample.reshape(n_s, d)]
    conv_in = [jnp.zeros((depth, bp, CONV_W - 1, w_rg), F32), state_conv]
    h_in = [jnp.zeros((depth, bp, w_rg), F32), state_rglru]
    s_in = [jnp.zeros((depth, bp, HG_HEADS, HG_D, HG_D), F32), state_hgrn]
    conv_out, h_out, s_out = ([], []), ([], []), ([], [])

    for l in range(depth):
        mod = mod_all[l]
        wr_d, wi_d = wr_all[l], wi_all[l]
        cnt = jnp.zeros((ne, 1), F32)
        x1s, hbs, poss, probs, segbs, segls = [], [], [], [], [], []
        for gi, (bsz, t, boff, _) in enumerate(groups):
            proj = _inproj(xs[gi], mod, boff, g_norm1[l][None], w_in_b[l], t)
            o_rg, c_new, h_last = _rglru(
                proj, conv_in[gi][l], h_in[gi][l], conv_w[l], conv_b[l][None], wr_d, b_rgate[l][None],
                wi_d, b_igate[l][None], lru_lambda[l][None], bsz, t)
            o_hg, s_last = _hgrn(proj, lb_all[l][None], hgrn_norm_g[l][None], s_in[gi][l], bsz, t)
            x1, hb, pos_t, prob, segb, segl, cnt = _outproj(
                o_rg, o_hg, xs[gi], mod, boff, w_out_b[l], g_norm2[l][None], _router_weights(w_router[l]),
                b_router[l][:, None], cnt, t)
            conv_out[gi].append(c_new)
            h_out[gi].append(h_last[:, 0])
            s_out[gi].append(s_last)
            x1s.append(x1)
            hbs.append(hb)
            poss.append(pos_t)
            probs.append(prob)
            segbs.append(segb[:, :, 0])
            segls.append(segl[:, :, 0])

        counts = cnt[:, 0].astype(I32)
        padded = (counts + FFN_TILE - 1) // FFN_TILE * FFN_TILE
        pad_end = jnp.cumsum(padded)
        off = jnp.concatenate([jnp.zeros((1,), I32), pad_end]).astype(I32)
        n_used = (pad_end[-1] // FFN_TILE).astype(I32)
        tile_start = jnp.minimum(jnp.arange(n_tiles, dtype=I32), n_used - 1) * FFN_TILE
        tile_expert = jnp.minimum(jnp.sum(tile_start[:, None] >= pad_end[None, :], axis=1), ne - 1).astype(I32)
        eidx = jnp.arange(ne, dtype=I32)
        of_tile = lambda table: jnp.sum(jnp.where(tile_expert[:, None] == eidx[None, :], table[None, :], 0), axis=1)
        tile_idx = jnp.arange(n_tiles, dtype=I32)
        tile_rows = jnp.where(
            tile_idx < n_used,
            jnp.clip(of_tile(counts) - (tile_idx * FFN_TILE - of_tile(off[:ne])), 0, FFN_TILE), 0).astype(I32)
        later_used = (counts[None, :] > 0) & (eidx[None, :] > eidx[:, None])
        next_used = jnp.min(jnp.where(later_used, eidx[None, :], ne), axis=1)
        next_expert = of_tile(jnp.where(next_used < ne, next_used, -1)).astype(I32)
        lists = _run_lists(off[None, :ne] + jnp.concatenate(segbs, axis=0).astype(I32),
                           jnp.concatenate(segls, axis=0).astype(I32), ne)
        zoff = jnp.concatenate([off, jnp.full((1,), n_rows, I32)])
        zcnt = jnp.concatenate([counts, jnp.zeros((1,), I32)])

        x_sorted = _dispatch(lists, zoff, zcnt, jnp.concatenate(poss, axis=1), hbs[0], hbs[1],
                             n_rows, ne)
        y_rows = _ffn(l, tile_expert, tile_rows, next_expert, x_sorted, w_gate, b_gate[:, :, None, :],
                      w_up, b_up[:, :, None, :], w_down, b_down[:, :, None, :])
        final = l == depth - 1
        xs = [_combine(lists, y_rows, probs[gi], x1s[gi], mod, boff, g_final[None],
                       roff, t, ne, final)
              for gi, (bsz, t, boff, roff) in enumerate(groups)]

    return (xs[0].reshape(bp, tp, d), xs[1].reshape(bs, ts, d),
            jnp.stack(conv_out[0]), jnp.stack(h_out[0]), jnp.stack(s_out[0]),
            jnp.stack(conv_out[1]), jnp.stack(h_out[1]), jnp.stack(s_out[1]))
```

```python
import functools
import math

import jax
import jax.numpy as jnp
from jax import lax
from jax.experimental import pallas as pl
from jax.experimental.pallas import tpu as pltpu

F32 = jnp.float32
BF16 = jnp.bfloat16
I32 = jnp.int32
HIGHEST = lax.Precision.HIGHEST

EPS = 1e-6
RG_C = 8.0
CONV_W = 4
HG_HEADS = 4
HG_D = 128
TOP_K = 4
SWIGLU_LIMIT = 7.0
SWIGLU_ALPHA = 1.702
LOG2_E = 1.4426950408889634

V7X_LANES = 128
V7X_SUBLANES = 8
V7X_VMEM_LIMIT_BYTES = 56 * 1024 * 1024

ROW_TILE = 256
INPROJ_TILE = 512
HGRN_CHUNK = 256
SEQ_TILES_PER_STEP = 4
OUTPROJ_PARTS = 4
FFN_TILE = 512
ROW_PARTS = 4
RUN_ALIGN = V7X_SUBLANES // ROW_PARTS
ADA_COLS = 1536

NT_DIMS = (((1,), (1,)), ((), ()))
TN_DIMS = (((0,), (0,)), ((), ()))


def _params(sem, vmem=V7X_VMEM_LIMIT_BYTES):
    return pltpu.CompilerParams(dimension_semantics=sem, vmem_limit_bytes=vmem)


def _rms(x, g):
    ms = jnp.mean(x * x, axis=-1, keepdims=True)
    return x * lax.rsqrt(ms + EPS) * g


def _mod_row(mod_ref, k, rows, part=0, parts=1):
    nb, _, d = mod_ref.shape
    if nb == 1:
        return mod_ref[0, k:k + 1, :]
    nbp = nb // parts
    per = rows // nbp
    return jnp.concatenate([jnp.broadcast_to(mod_ref[b, k:k + 1, :], (per, d))
                            for b in range(part * nbp, (part + 1) * nbp)], axis=0)


def _mod_spec(t, boff, d, tile=None):
    tile = tile or ROW_TILE
    if t >= tile:
        assert t % tile == 0
        per = t // tile
        return pl.BlockSpec((1, 6, d), lambda i, *_: (boff + i // per, 0, 0))
    nb = tile // t
    assert tile % t == 0 and boff % nb == 0
    return pl.BlockSpec((nb, 6, d), lambda i, *_: (boff // nb + i, 0, 0))


def _ada_kernel(c_ref, w_ref, b_ref, o_ref):
    c = c_ref[...]
    s = c * jax.nn.sigmoid(c)
    o_ref[0] = jnp.dot(s, w_ref[0], precision=HIGHEST, preferred_element_type=F32) + b_ref[0]


def _ada(c_all, w_ada, b_ada):
    depth, d, n6 = w_ada.shape
    rows = c_all.shape[0]
    return pl.pallas_call(
        _ada_kernel,
        grid=(depth, n6 // ADA_COLS),
        in_specs=[
            pl.BlockSpec((rows, d), lambda l, j: (0, 0)),
            pl.BlockSpec((1, d, ADA_COLS), lambda l, j: (l, 0, j)),
            pl.BlockSpec((1, 1, ADA_COLS), lambda l, j: (l, 0, j)),
        ],
        out_specs=pl.BlockSpec((1, rows, ADA_COLS), lambda l, j: (l, 0, j)),
        out_shape=jax.ShapeDtypeStruct((depth, rows, n6), F32),
        compiler_params=_params(("arbitrary", "arbitrary")),
        name="ada",
    )(c_all, w_ada, b_ada.reshape(depth, 1, n6))


def _inproj_kernel(x_ref, mod_ref, g_ref, w_ref, o_ref):
    rows = x_ref.shape[0]
    hn = _rms(x_ref[...], g_ref[...]) * (1.0 + _mod_row(mod_ref, 1, rows)) + _mod_row(mod_ref, 0, rows)
    o_ref[...] = jnp.dot(hn.astype(BF16), w_ref[...], preferred_element_type=F32)


def _inproj(x, mod, boff, g1, w_in, t):
    n, d = x.shape
    cols = w_in.shape[1]
    tm = INPROJ_TILE
    assert n % tm == 0
    return pl.pallas_call(
        _inproj_kernel,
        grid=(n // tm,),
        in_specs=[
            pl.BlockSpec((tm, d), lambda i: (i, 0)),
            _mod_spec(t, boff, d, tm),
            pl.BlockSpec((1, d), lambda i: (0, 0)),
            pl.BlockSpec((d, cols), lambda i: (0, 0)),
        ],
        out_specs=pl.BlockSpec((tm, cols), lambda i: (i, 0)),
        out_shape=jax.ShapeDtypeStruct((n, cols), F32),
        compiler_params=_params(("arbitrary",)),
        name="inproj",
    )(x, mod, g1, w_in)


def _rglru_kernel(xr_ref, yr_ref, cst_ref, h0_ref, cw_ref, cb_ref, wr_ref, br_ref, wi_ref, bi_ref,
                  lam_ref, o_ref, cnew_ref, hlast_ref, tail_s, h_s, *, tt):
    j = pl.program_id(1)
    w = xr_ref.shape[1]

    @pl.when(j == 0)
    def _():
        tail_s[...] = jnp.zeros((V7X_SUBLANES, w), F32)
        tail_s[V7X_SUBLANES - (CONV_W - 1):, :] = cst_ref[0]
        h_s[...] = h0_ref[0]

    for r0 in range(0, xr_ref.shape[0], tt):
        _rglru_tile(slice(r0, r0 + tt), xr_ref, yr_ref, cw_ref, cb_ref, wr_ref, br_ref, wi_ref, bi_ref,
                    lam_ref, o_ref, cnew_ref, hlast_ref, tail_s, h_s)


def _rglru_tile(rs, xr_ref, yr_ref, cw_ref, cb_ref, wr_ref, br_ref, wi_ref, bi_ref, lam_ref,
                o_ref, cnew_ref, hlast_ref, tail_s, h_s):
    tt = rs.stop - rs.start
    w = xr_ref.shape[1]
    x = xr_ref[rs, :]
    ext = jnp.concatenate([tail_s[...], x], axis=0)
    cw = cw_ref[...]
    xc = cb_ref[...] + x * cw[CONV_W - 1:CONV_W, :]
    for d in range(1, CONV_W):
        xs = pltpu.roll(ext, d, axis=0)[V7X_SUBLANES:, :]
        xc = xc + xs * cw[CONV_W - 1 - d:CONV_W - d, :]
    tail_s[...] = x[tt - V7X_SUBLANES:, :]
    cnew_ref[0] = x[tt - (CONV_W - 1):, :]

    xb = xc.astype(BF16)
    gr = jax.nn.sigmoid(jnp.dot(xb, wr_ref[...], preferred_element_type=F32) + br_ref[...])
    gi = jax.nn.sigmoid(jnp.dot(xb, wi_ref[...], preferred_element_type=F32) + bi_ref[...])
    nl = -lam_ref[...]
    sp = jnp.maximum(nl, 0.0) + jnp.log1p(jnp.exp(-jnp.abs(nl)))
    a = jnp.exp((-RG_C) * gr * sp)
    u = jnp.sqrt(1.0 - a * a) * gi * xc

    grp = V7X_SUBLANES
    ngrp = tt // grp
    a3 = a.reshape(ngrp, grp, w)
    u3 = u.reshape(ngrp, grp, w)
    sub = lax.broadcasted_iota(I32, (ngrp, grp, w), 1)
    d = 1
    while d < grp:
        keep = sub >= d
        u3 = jnp.where(keep, u3 + a3 * pltpu.roll(u3, d, axis=1), u3)
        a3 = jnp.where(keep, a3 * pltpu.roll(a3, d, axis=1), a3)
        d *= 2
    last = lambda x3: jnp.broadcast_to(x3[:, grp - 1:grp, :], x3.shape)
    a_end, u_end = last(a3), last(u3)
    carry = jnp.broadcast_to(h_s[...], (grp, w))
    h_groups = []
    for gi_ in range(ngrp):
        h_groups.append(u3[gi_] + a3[gi_] * carry)
        carry = u_end[gi_] + a_end[gi_] * carry
    h = jnp.concatenate(h_groups, axis=0)
    h_s[...] = carry[:1, :]
    hlast_ref[0] = carry[:1, :]
    o_ref[rs, :] = (jax.nn.gelu(yr_ref[rs, :], approximate=True) * h).astype(BF16)


def _seq_step_rows(t, tile):
    per = min(SEQ_TILES_PER_STEP, t // tile)
    while t % (tile * per):
        per -= 1
    return tile * per


def _rglru(proj, conv_st, h0, cw, cb, wr, br, wi, bi, lam, bsz, t):
    n = proj.shape[0]
    w = cw.shape[1]
    tile = min(t, ROW_TILE)
    assert tile >= V7X_SUBLANES and t % tile == 0
    tt = _seq_step_rows(t, tile)
    nt = t // tt
    full = lambda shape: pl.BlockSpec(shape, lambda b, j: (0,) * len(shape))
    return pl.pallas_call(
        functools.partial(_rglru_kernel, tt=tile),
        grid=(bsz, nt),
        in_specs=[
            pl.BlockSpec((tt, w), lambda b, j: (b * nt + j, 0)),
            pl.BlockSpec((tt, w), lambda b, j: (b * nt + j, 1)),
            pl.BlockSpec((1, CONV_W - 1, w), lambda b, j: (b, 0, 0)),
            pl.BlockSpec((1, 1, w), lambda b, j: (b, 0, 0)),
            full((CONV_W, w)), full((1, w)), full((w, w)), full((1, w)), full((w, w)), full((1, w)),
            full((1, w)),
        ],
        out_specs=[
            pl.BlockSpec((tt, w), lambda b, j: (b * nt + j, 0)),
            pl.BlockSpec((1, CONV_W - 1, w), lambda b, j: (b, 0, 0)),
            pl.BlockSpec((1, 1, w), lambda b, j: (b, 0, 0)),
        ],
        out_shape=[
            jax.ShapeDtypeStruct((n, w), BF16),
            jax.ShapeDtypeStruct((bsz, CONV_W - 1, w), F32),
            jax.ShapeDtypeStruct((bsz, 1, w), F32),
        ],
        scratch_shapes=[pltpu.VMEM((V7X_SUBLANES, w), F32), pltpu.VMEM((1, w), F32)],
        compiler_params=_params(("arbitrary", "arbitrary")),
        name="rglru",
    )(proj, proj, conv_st, h0.reshape(bsz, 1, w), cw, cb, wr, br, wi, bi, lam)


def _hgrn_tables(c):
    t = jnp.arange(c)[:, None]
    r = jnp.arange(c)[None, :]
    tri = (r <= t).astype(BF16)
    bd = min(c, V7X_LANES)
    t, r = t[:bd, :bd], r[:bd, :bd]
    masks = [(t == r)]
    h = 1
    while 2 * h <= bd:
        m_t = (t // (2 * h)) * (2 * h) + h
        same = (t // (2 * h)) == (r // (2 * h))
        masks.append(same & (t >= m_t) & (r < m_t))
        h *= 2
    return tri, jnp.stack(masks).astype(F32)


def _hgrn_kernel(q_ref, f_ref, v_ref, g_ref, lb_ref, ng_ref, s0_ref, tri_ref, mask_ref,
                 o_ref, slast_ref, s_s, acc_s, g_s, *, c, nlev):
    @pl.when(pl.program_id(1) == 0)
    def _():
        s_s[...] = s0_ref[0]

    for r0 in range(0, f_ref.shape[0], c):
        _hgrn_chunk(slice(r0, r0 + c), q_ref, f_ref, v_ref, g_ref, lb_ref, ng_ref, tri_ref, mask_ref,
                    o_ref, s_s, acc_s, g_s, nlev)
    slast_ref[0] = s_s[...]


def _hgrn_chunk(rs, q_ref, f_ref, v_ref, g_ref, lb_ref, ng_ref, tri_ref, mask_ref, o_ref,
                s_s, acc_s, g_s, nlev):
    c = rs.stop - rs.start
    w = f_ref.shape[1]
    ff = f_ref[rs, :]
    lb = lb_ref[...]
    log_sig = -(jnp.maximum(-ff, 0.0) + jnp.log1p(jnp.exp(-jnp.abs(ff))))
    log_f = log_sig + jnp.log1p(lb * jnp.exp(-ff))
    k = (1.0 - lb) * jax.nn.sigmoid(-ff)
    q = q_ref[rs, :]

    hi = log_f.astype(BF16)
    r1 = log_f - hi.astype(F32)
    mid = r1.astype(BF16)
    lo = (r1 - mid.astype(F32)).astype(BF16)
    tri = tri_ref[...]
    gcum = (jnp.dot(tri, hi, preferred_element_type=F32) + jnp.dot(tri, mid, preferred_element_type=F32)
            + jnp.dot(tri, lo, preferred_element_type=F32))
    gcum = gcum * LOG2_E
    g_s[...] = gcum

    row = lax.broadcasted_iota(I32, (c, w), 0)
    grp = V7X_SUBLANES
    groups = gcum.reshape(c // grp, grp, w)

    def group_row(r):
        return jnp.broadcast_to(groups[:, r:r + 1, :], groups.shape).reshape(c, w)

    def level_exponent(h):
        if h >= grp:
            parts = []
            for b in range(c // (2 * h)):
                lo0, m, hi0 = b * 2 * h, b * 2 * h + h, (b + 1) * 2 * h
                g_mid = g_s[m - 1:m, :]
                parts += [g_mid - gcum[lo0:m, :], gcum[m:hi0, :] - g_mid]
            return jnp.concatenate(parts, axis=0)
        if h == 1:
            g_mid = jnp.where((row & 1) == 1, pltpu.roll(gcum, 1, axis=0), gcum)
        else:
            g_mid = group_row(h - 1)
            for b in range(1, grp // (2 * h)):
                g_mid = jnp.where((row & (grp - 1)) >= b * 2 * h, group_row(b * 2 * h + h - 1), g_mid)
        return -jnp.abs(gcum - g_mid)

    bd = mask_ref.shape[1]

    def pair_scores(l, qw, kw):
        h = 0 if l == 0 else 2 ** (l - 1)
        keep = mask_ref[l] > 0.0 if 2 * h <= bd else None
        for hd in range(HG_HEADS):
            sl = slice(hd * HG_D, (hd + 1) * HG_D)
            if 2 * h <= bd:
                for b in range(c // bd):
                    rs = slice(b * bd, (b + 1) * bd)
                    p = lax.dot_general(qw[rs, sl], kw[rs, sl], NT_DIMS, preferred_element_type=F32)
                    acc_s[hd, rs, rs] = jnp.where(keep, p, 0.0 if l == 0 else acc_s[hd, rs, rs])
            else:
                for b in range(c // (2 * h)):
                    rows = slice(b * 2 * h + h, (b + 1) * 2 * h)
                    cols = slice(b * 2 * h, b * 2 * h + h)
                    acc_s[hd, rows, cols] = lax.dot_general(qw[rows, sl], kw[cols, sl], NT_DIMS,
                                                            preferred_element_type=F32)

    if c > bd:
        acc_s[...] = jnp.zeros(acc_s.shape, F32)
    qb = q.astype(BF16)
    kb = k.astype(BF16)
    pair_scores(0, qb, kb)
    for l in range(1, nlev + 1):
        wgt = jnp.exp2(level_exponent(2 ** (l - 1))).astype(BF16)
        pair_scores(l, qb * wgt, kb * wgt)

    glast = gcum[c - 1:, :]
    qg = (q * jnp.exp2(gcum)).astype(BF16)
    kdec = (k * jnp.exp2(glast - gcum)).astype(BF16)
    dlast = jnp.exp2(glast)
    vb = v_ref[rs, :].astype(BF16)
    for hd in range(HG_HEADS):
        sl = slice(hd * HG_D, (hd + 1) * HG_D)
        s_old = s_s[hd]
        o = (jnp.dot(acc_s[hd].astype(BF16), vb[:, sl], preferred_element_type=F32)
             + jnp.dot(qg[:, sl], s_old.astype(BF16), preferred_element_type=F32))
        kv = lax.dot_general(kdec[:, sl], vb[:, sl], TN_DIMS, preferred_element_type=F32)
        dcol = jnp.broadcast_to(dlast[:, sl], (HG_D, HG_D)).T
        s_s[hd] = dcol * s_old + kv
        gate = g_ref[rs, sl]
        o_ref[rs, sl] = (_rms(o, ng_ref[...]) * (gate * jax.nn.sigmoid(gate))).astype(BF16)


def _hgrn(proj, lb, ng, s0, bsz, t):
    n = proj.shape[0]
    w = HG_HEADS * HG_D
    c = min(t, HGRN_CHUNK)
    nlev = int(math.log2(c))
    assert 2 ** nlev == c and t % c == 0
    tri, masks = _hgrn_tables(c)
    tt = _seq_step_rows(t, c)
    nt = t // tt
    col = lambda k: pl.BlockSpec((tt, w), lambda b, j: (b * nt + j, k))
    full = lambda shape: pl.BlockSpec(shape, lambda b, j: (0,) * len(shape))
    return pl.pallas_call(
        functools.partial(_hgrn_kernel, c=c, nlev=nlev),
        grid=(bsz, nt),
        in_specs=[
            col(2), col(3), col(4), col(5),
            full((1, w)), full((1, HG_D)),
            pl.BlockSpec((1, HG_HEADS, HG_D, HG_D), lambda b, j: (b, 0, 0, 0)),
            full(tri.shape), full(masks.shape),
        ],
        out_specs=[
            pl.BlockSpec((tt, w), lambda b, j: (b * nt + j, 0)),
            pl.BlockSpec((1, HG_HEADS, HG_D, HG_D), lambda b, j: (b, 0, 0, 0)),
        ],
        out_shape=[
            jax.ShapeDtypeStruct((n, w), BF16),
            jax.ShapeDtypeStruct((bsz, HG_HEADS, HG_D, HG_D), F32),
        ],
        scratch_shapes=[pltpu.VMEM((HG_HEADS, HG_D, HG_D), F32),
                        pltpu.VMEM((HG_HEADS, c, c), F32),
                        pltpu.VMEM((c, w), F32)],
        compiler_params=_params(("arbitrary", "arbitrary")),
        name="hgrn",
    )(proj, proj, proj, proj, lb, ng, s0, tri, masks)


def _outproj_kernel(org_ref, ohg_ref, x_ref, mod_ref, wo_ref, g2_ref, wr_ref, brt_ref, tri_ref,
                    low_ref, cin_ref, x1_ref, hb_ref, pos_ref, prob_ref, segb_ref, segl_ref, cnt_ref,
                    carry_s, *, tm, parts):
    @pl.when(pl.program_id(0) == 0)
    def _():
        carry_s[...] = cin_ref[...]

    carry = carry_s[...]
    for part in range(parts):
        carry = _outproj_part(part, parts, tm, carry, org_ref, ohg_ref, x_ref, mod_ref, wo_ref,
                              g2_ref, wr_ref, brt_ref, tri_ref, low_ref, x1_ref, hb_ref, pos_ref,
                              prob_ref, segb_ref, segl_ref)
    carry_s[...] = carry
    cnt_ref[...] = carry


def _outproj_part(part, parts, tm, carry, org_ref, ohg_ref, x_ref, mod_ref, wo_ref, g2_ref, wr_ref,
                  brt_ref, tri_ref, low_ref, x1_ref, hb_ref, pos_ref, prob_ref, segb_ref, segl_ref):
    rs = slice(part * tm, (part + 1) * tm)
    mrow = lambda k: _mod_row(mod_ref, k, tm, part, parts)
    half = org_ref.shape[1]
    mix = (jnp.dot(org_ref[rs, :], wo_ref[:half, :], preferred_element_type=F32)
           + jnp.dot(ohg_ref[rs, :], wo_ref[half:, :], preferred_element_type=F32))
    x1 = x_ref[rs, :] + mrow(2) * mix
    x1_ref[rs, :] = x1
    hn = _rms(x1, g2_ref[...]) * (1.0 + mrow(4)) + mrow(3)
    hn_hi = hn.astype(BF16)
    hb_ref[rs, :] = hn_hi

    ne = brt_ref.shape[0]
    hn_lo = (hn - hn_hi.astype(F32)).astype(BF16)
    both = jnp.dot(hn_hi, wr_ref[...], preferred_element_type=F32)
    logits_tm = (both[:, :V7X_LANES] + both[:, V7X_LANES:]
                 + jnp.dot(hn_lo, wr_ref[:, :V7X_LANES], preferred_element_type=F32))
    logits = logits_tm.T[:ne, :] + brt_ref[...]
    eio = lax.broadcasted_iota(I32, (ne, tm), 0)
    cur = logits
    vals, sels = [], []
    for _ in range(TOP_K):
        mx = jnp.max(cur, axis=0, keepdims=True)
        ix = jnp.min(jnp.where(cur == mx, eio, ne), axis=0, keepdims=True)
        sel = eio == ix
        vals.append(mx)
        sels.append(sel)
        cur = jnp.where(sel, -jnp.inf, cur)
    ex = [jnp.exp(v - vals[0]) for v in vals]
    den = ex[0] + ex[1] + ex[2] + ex[3]
    probs = [e / den for e in ex]

    onehot = sels[0].astype(F32)
    for sel in sels[1:]:
        onehot = onehot + sel.astype(F32)
    incl = jnp.dot(onehot.astype(BF16), tri_ref[...], preferred_element_type=F32)
    run_len = jnp.ceil(incl[:, tm - 1:] * (1.0 / RUN_ALIGN)) * RUN_ALIGN
    run0 = jnp.dot(low_ref[...], jnp.broadcast_to(run_len, (ne, tm)).astype(BF16),
                   preferred_element_type=F32)
    local = run0 + incl - onehot
    slots = [jnp.sum(jnp.where(sel, local, 0.0), axis=0, keepdims=True) for sel in sels]
    segb_ref[part] = carry
    segl_ref[part] = run_len

    pos_ref[:, rs] = jnp.concatenate(slots, axis=0).astype(I32)
    sub = lax.broadcasted_iota(I32, (V7X_LANES, tm), 0)
    pt = jnp.zeros((V7X_LANES, tm), F32)
    for kk in range(TOP_K):
        pt = jnp.where(sub == kk, probs[kk], pt)
        pt = jnp.where(sub == TOP_K + kk, slots[kk], pt)
    prob_ref[rs, :] = pt.T
    return carry + run_len


def _router_weights(w_router):
    d, ne = w_router.shape
    assert ne <= V7X_LANES
    hi = w_router.astype(BF16)
    lo = (w_router - hi.astype(F32)).astype(BF16)
    pad = jnp.zeros((d, V7X_LANES - ne), BF16)
    return jnp.concatenate([hi, pad, lo, pad], axis=1)


def _outproj(o_rg, o_hg, x, mod, boff, w_out, g2, wr, brt, cnt_in, t):
    n, d = x.shape
    half = o_rg.shape[1]
    ne = brt.shape[0]
    tm = ROW_TILE
    parts = min(OUTPROJ_PARTS, n // ROW_TILE)
    tile = tm * parts
    assert n % tile == 0 and tm <= 256
    nt = n // tm
    tri = (jnp.arange(tm)[:, None] <= jnp.arange(tm)[None, :]).astype(BF16)
    low = (jnp.arange(ne)[:, None] > jnp.arange(ne)[None, :]).astype(BF16)
    rowblk = lambda cols: pl.BlockSpec((tile, cols), lambda i: (i, 0))
    full = lambda shape: pl.BlockSpec(shape, lambda i: (0,) * len(shape))
    pertile = lambda: pl.BlockSpec((parts, ne, 1), lambda i: (i, 0, 0))
    return pl.pallas_call(
        functools.partial(_outproj_kernel, tm=tm, parts=parts),
        grid=(n // tile,),
        in_specs=[
            rowblk(half), rowblk(half), rowblk(d),
            _mod_spec(t, boff, d, tile),
            full((d, d)), full((1, d)), full(wr.shape), full((ne, 1)), full((tm, tm)), full((ne, ne)),
            full((ne, 1)),
        ],
        out_specs=[
            rowblk(d), rowblk(d), pl.BlockSpec((TOP_K, tile), lambda i: (0, i)), rowblk(V7X_LANES),
            pertile(), pertile(), full((ne, 1)),
        ],
        out_shape=[
            jax.ShapeDtypeStruct((n, d), F32),
            jax.ShapeDtypeStruct((n, d), BF16),
            jax.ShapeDtypeStruct((TOP_K, n), I32),
            jax.ShapeDtypeStruct((n, V7X_LANES), F32),
            jax.ShapeDtypeStruct((nt, ne, 1), F32),
            jax.ShapeDtypeStruct((nt, ne, 1), F32),
            jax.ShapeDtypeStruct((ne, 1), F32),
        ],
        scratch_shapes=[pltpu.VMEM((ne, 1), F32)],
        compiler_params=_params(("arbitrary",)),
        name="outproj",
    )(o_rg, o_hg, x, mod, w_out, g2, wr, brt, tri, low, cnt_in)


def _pow2_pieces(hi, lo=1):
    return tuple(1 << b for b in range(int(math.log2(hi)), int(math.log2(lo)) - 1, -1))


def _tile_slots(ne):
    worst = TOP_K * ROW_TILE + ne * (RUN_ALIGN - 1)
    return -(-worst // V7X_LANES) * V7X_LANES


def _stored_rows(ref, first_row, rows):
    start = first_row * ROW_PARTS
    if not isinstance(start, int):
        start = pl.multiple_of(start, V7X_SUBLANES)
    return ref.at[pl.ds(start, rows * ROW_PARTS), :]


def _store_packed(ref, words):
    rows = words.shape[0]
    for c in range(ROW_PARTS):
        ref[pl.ds(c, rows, stride=ROW_PARTS), :] = words[:, c * V7X_LANES:(c + 1) * V7X_LANES]


def _load_packed(ref):
    rows = ref.shape[0] // ROW_PARTS
    return jnp.concatenate([ref[pl.ds(c, rows, stride=ROW_PARTS), :] for c in range(ROW_PARTS)], axis=1)


def _copy_run(n, make_copy, wait, pieces):
    for rows in pieces:
        done = n & (-2 * rows)

        @pl.when((n & rows) != 0)
        def _():
            cp = make_copy(done, rows)
            cp.wait() if wait else cp.start()


RUN_PIECES = _pow2_pieces(ROW_TILE, RUN_ALIGN)
BUF_ROW_BITS = 11
LIST_HEAD = V7X_SUBLANES


def _run_lists(seg_start, seg_len, ne):
    assert _tile_slots(ne) <= 1 << BUF_ROW_BITS
    tiles = seg_len.shape[0]
    buf0 = jnp.cumsum(seg_len, axis=1) - seg_len
    j = jnp.arange(ne, dtype=I32)
    counts, entries = [], []
    for rows in RUN_PIECES:
        has = (seg_len & rows) != 0
        done = seg_len & (-2 * rows)
        packed = ((seg_start + done) << BUF_ROW_BITS) | (buf0 + done)
        rank = jnp.cumsum(has.astype(I32), axis=1) - 1
        place = has[:, :, None] & (rank[:, :, None] == j[None, None, :])
        entries.append(jnp.sum(jnp.where(place, packed[:, :, None], 0), axis=1))
        counts.append(jnp.sum(has.astype(I32), axis=1))
    head = jnp.stack(counts + [jnp.zeros((tiles,), I32)] * (LIST_HEAD - len(RUN_PIECES)), axis=1)
    flat = jnp.concatenate([head] + entries, axis=1)
    width = -(-flat.shape[1] // V7X_LANES) * V7X_LANES
    flat = jnp.pad(flat, ((0, 0), (0, width - flat.shape[1])))
    return flat.reshape(tiles, 1, width)


def _list_spec(ne, index_map):
    width = -(-(LIST_HEAD + len(RUN_PIECES) * ne) // V7X_LANES) * V7X_LANES
    return pl.BlockSpec((1, 1, width), index_map, memory_space=pltpu.SMEM)


def _tile_runs(list_ref, ne, make_copy, wait):
    for pi, rows in enumerate(RUN_PIECES):
        def one(jj, pi=pi, rows=rows):
            v = list_ref[0, 0, LIST_HEAD + pi * ne + jj]
            cp = make_copy(v >> BUF_ROW_BITS, v & ((1 << BUF_ROW_BITS) - 1), rows)
            cp.wait() if wait else cp.start()

        def pair(jj, carry, one=one):
            one(2 * jj)
            one(2 * jj + 1)
            return carry

        cnt = list_ref[0, 0, pi]
        lax.fori_loop(0, cnt >> 1, pair, 0)
        pl.when((cnt & 1) != 0)(functools.partial(one, cnt - 1))


def _pack_pairs(lo_cols, hi_cols):
    a = lax.bitcast_convert_type(lo_cols, jnp.uint32)
    b = lax.bitcast_convert_type(hi_cols, jnp.uint32)
    return (a & jnp.uint32(0xFFFF0000)) | (b >> 16)


def _unpack_pairs(words):
    a = lax.bitcast_convert_type(words & jnp.uint32(0xFFFF0000), F32).astype(BF16)
    b = lax.bitcast_convert_type(words << 16, F32).astype(BF16)
    return a, b


def _dispatch_kernel(zoff_ref, zcnt_ref, lst_ref, lst1_ref, lst2_ref, pos_ref, hba_ref, hbb_ref,
                     out_ref, buf_s, zero_s, sem, zsem, *, ne, tiles_a):
    i = pl.program_id(0)
    last = pl.num_programs(0) - 1
    slot = i % 2
    tt, d = hba_ref.shape
    dh = d // 2

    def runs(list_ref, s, wait):
        _tile_runs(list_ref, ne,
                   lambda g, b, rows: pltpu.make_async_copy(
                       _stored_rows(buf_s.at[s], b, rows), _stored_rows(out_ref, g, rows), sem.at[s]),
                   wait)

    @pl.when(i >= 2)
    def _():
        runs(lst2_ref, slot, True)

    pos = pos_ref[...]
    slot_row = lax.broadcasted_iota(I32, (buf_s.shape[1] // ROW_PARTS, tt), 0)
    hit = slot_row == pos[0:1, :]
    for kk in range(1, TOP_K):
        hit = hit | (slot_row == pos[kk:kk + 1, :])
    perm = jnp.where(hit, 1.0, 0.0).astype(BF16)
    hb = jnp.where(i < tiles_a, hba_ref[...], hbb_ref[...])
    _store_packed(buf_s.at[slot], _pack_pairs(jnp.dot(perm, hb[:, :dh], preferred_element_type=F32),
                                              jnp.dot(perm, hb[:, dh:], preferred_element_type=F32)))
    runs(lst_ref, slot, False)

    @pl.when(i == last)
    def _():
        @pl.when(i >= 1)
        def _():
            runs(lst1_ref, 1 - slot, True)
        runs(lst_ref, slot, True)

        zero_s[...] = jnp.zeros(zero_s.shape, zero_s.dtype)
        zrows = zero_s.shape[0] // ROW_PARTS
        zcopy = lambda g, rows: pltpu.make_async_copy(
            _stored_rows(zero_s, 0, rows), _stored_rows(out_ref, g, rows), zsem)

        def zero_run(e, carry):
            lo = zoff_ref[e] + zcnt_ref[e]
            n = zoff_ref[e + 1] - lo
            nfull = n // zrows
            rem = n - nfull * zrows
            for wait in (False, True):
                def full(c, c2):
                    cp = zcopy(lo + c * zrows, zrows)
                    cp.wait() if wait else cp.start()
                    return c2
                lax.fori_loop(0, nfull, full, 0)
                _copy_run(rem, lambda o, rows: zcopy(lo + nfull * zrows + o, rows), wait,
                          _pow2_pieces(zrows // 2, RUN_ALIGN))
            return carry

        lax.fori_loop(0, zcnt_ref.shape[0], zero_run, 0)


def _dispatch(lists, zoff, zcnt, pos_t, hb_a, hb_b, n_rows, ne):
    d = hb_a.shape[1]
    tt = ROW_TILE
    ta, tb = hb_a.shape[0] // tt, hb_b.shape[0] // tt
    assert hb_a.shape[0] % tt == 0 and hb_b.shape[0] % tt == 0
    back = lambda k: _list_spec(ne, lambda i, *_: (jnp.maximum(i - k, 0), 0, 0))
    return pl.pallas_call(
        functools.partial(_dispatch_kernel, ne=ne, tiles_a=ta),
        grid_spec=pltpu.PrefetchScalarGridSpec(
            num_scalar_prefetch=2,
            grid=(ta + tb,),
            in_specs=[back(0), back(1), back(2),
                      pl.BlockSpec((TOP_K, tt), lambda i, *_: (0, i)),
                      pl.BlockSpec((tt, d), lambda i, *_: (jnp.minimum(i, ta - 1), 0)),
                      pl.BlockSpec((tt, d), lambda i, *_: (jnp.maximum(i - ta, 0), 0))],
            out_specs=pl.BlockSpec(memory_space=pl.ANY),
            scratch_shapes=[pltpu.VMEM((2, _tile_slots(ne) * ROW_PARTS, V7X_LANES), jnp.uint32),
                            pltpu.VMEM((tt * ROW_PARTS, V7X_LANES), jnp.uint32),
                            pltpu.SemaphoreType.DMA((2,)), pltpu.SemaphoreType.DMA(())],
        ),
        out_shape=jax.ShapeDtypeStruct((n_rows * ROW_PARTS, V7X_LANES), jnp.uint32),
        compiler_params=_params(("arbitrary",)),
        name="dispatch",
    )(zoff, zcnt, lists, lists, lists, pos_t, hb_a, hb_b)


def _ffn_kernel(te_ref, tr_ref, nx_ref, xs_ref, wg_ref, bg_ref, wu_ref, bu_ref, wd_ref, bd_ref, y_ref,
                wg_s, wu_s, wd_s, stage_s, sem, *, layer):
    i = pl.program_id(0)
    rows = tr_ref[i]
    valid = rows > 0
    changed = (i == 0) | (te_ref[i] != te_ref[jnp.maximum(i - 1, 0)])
    weights = (wg_ref, wu_ref, wd_ref)

    def fetch(e):
        return [pltpu.make_async_copy(w_ref.at[layer, e], stage_s.at[k], sem.at[k])
                for k, w_ref in enumerate(weights)]

    @pl.when(valid & changed)
    def _():
        @pl.when(i == 0)
        def _():
            for cp in fetch(te_ref[i]):
                cp.start()
        for cp in fetch(te_ref[i]):
            cp.wait()
        wg_s[...] = stage_s[0].astype(BF16)
        wu_s[...] = stage_s[1].astype(BF16)
        wd_s[...] = stage_s[2].astype(BF16)

        @pl.when(nx_ref[i] >= 0)
        def _():
            for cp in fetch(nx_ref[i]):
                cp.start()

    full = xs_ref.shape[0] // ROW_PARTS
    half = full // 2
    dh = ROW_PARTS * xs_ref.shape[1]

    def run(n):
        stored = pl.ds(0, n * ROW_PARTS)
        xa, xb = _unpack_pairs(_load_packed(xs_ref.at[stored, :]))
        gate = (jnp.dot(xa, wg_s[:dh, :], preferred_element_type=F32)
                + jnp.dot(xb, wg_s[dh:, :], preferred_element_type=F32) + bg_ref[...])
        up = (jnp.dot(xa, wu_s[:dh, :], preferred_element_type=F32)
              + jnp.dot(xb, wu_s[dh:, :], preferred_element_type=F32) + bu_ref[...])
        gate = jnp.minimum(gate, SWIGLU_LIMIT)
        up = jnp.clip(up, -SWIGLU_LIMIT, SWIGLU_LIMIT)
        glu = gate * jax.nn.sigmoid(SWIGLU_ALPHA * gate)
        act = ((up + 1.0) * glu).astype(BF16)
        y = (jnp.dot(act, wd_s[...], preferred_element_type=F32) + bd_ref[...]).astype(BF16).astype(F32)
        yh = y.shape[1] // 2
        _store_packed(y_ref.at[stored, :], _pack_pairs(y[:, :yh], y[:, yh:]))

    pl.when(rows > half)(lambda: run(full))

    @pl.when(valid & (rows <= half))
    def _():
        run(half)
        y_ref[half * ROW_PARTS:, :] = jnp.zeros((half * ROW_PARTS, y_ref.shape[1]), y_ref.dtype)

    @pl.when(jnp.logical_not(valid))
    def _():
        y_ref[...] = jnp.zeros(y_ref.shape, y_ref.dtype)


def _ffn(layer, tile_expert, tile_rows, next_expert, xs, w_gate, b_gate, w_up, b_up, w_down, b_down):
    d, dff = w_gate.shape[2], w_gate.shape[3]
    assert d == dff
    tm = FFN_TILE * ROW_PARTS
    assert xs.shape[1] == V7X_LANES and xs.shape[0] % tm == 0
    bspec = lambda c: pl.BlockSpec((None, None, 1, c), lambda i, te, *_: (layer, te[i], 0, 0))
    hbm = pl.BlockSpec(memory_space=pl.ANY)
    return pl.pallas_call(
        functools.partial(_ffn_kernel, layer=layer),
        grid_spec=pltpu.PrefetchScalarGridSpec(
            num_scalar_prefetch=3,
            grid=(xs.shape[0] // tm,),
            in_specs=[
                pl.BlockSpec((tm, V7X_LANES), lambda i, *_: (i, 0)),
                hbm, bspec(dff), hbm, bspec(dff), hbm, bspec(d),
            ],
            out_specs=pl.BlockSpec((tm, V7X_LANES), lambda i, *_: (i, 0)),
            scratch_shapes=[pltpu.VMEM((d, dff), BF16), pltpu.VMEM((d, dff), BF16),
                            pltpu.VMEM((dff, d), BF16), pltpu.VMEM((3, d, dff), F32),
                            pltpu.SemaphoreType.DMA((3,))],
        ),
        out_shape=jax.ShapeDtypeStruct(xs.shape, jnp.uint32),
        compiler_params=_params(("arbitrary",)),
        name="ffn",
    )(tile_expert, tile_rows, next_expert, xs, w_gate, b_gate, w_up, b_up, w_down, b_down)


def _combine_kernel(lst_ref, nxt_ref, y_ref, prob_ref, x1_ref, mod_ref, gf_ref, o_ref,
                    buf_s, sem, *, ne, final):
    i = pl.program_id(0)
    slot = i % 2
    tt = x1_ref.shape[0]

    def runs(list_ref, s, wait):
        _tile_runs(list_ref, ne,
                   lambda g, b, rows: pltpu.make_async_copy(
                       _stored_rows(y_ref, g, rows), _stored_rows(buf_s.at[s], b, rows), sem.at[s]),
                   wait)

    @pl.when(i == 0)
    def _():
        buf_s[...] = jnp.zeros(buf_s.shape, buf_s.dtype)
        runs(lst_ref, slot, False)

    @pl.when(i + 1 < pl.num_programs(0))
    def _():
        runs(nxt_ref, 1 - slot, False)

    runs(lst_ref, slot, True)

    ya, yb = _unpack_pairs(_load_packed(buf_s.at[slot]))
    pp = prob_ref[...]
    slots = buf_s.shape[1] // ROW_PARTS
    slot_col = lax.broadcasted_iota(I32, (tt, slots), 1)
    wgt = jnp.zeros((tt, slots), F32)
    for kk in range(TOP_K):
        pos = pp[:, TOP_K + kk:TOP_K + kk + 1].astype(I32)
        wgt = jnp.where(slot_col == pos, pp[:, kk:kk + 1], wgt)
    wgt = wgt.astype(BF16)
    y = jnp.concatenate([jnp.dot(wgt, ya, preferred_element_type=F32),
                         jnp.dot(wgt, yb, preferred_element_type=F32)], axis=1)
    x2 = x1_ref[...] + _mod_row(mod_ref, 5, tt) * y
    if final:
        x2 = _rms(x2, gf_ref[...])
    o_ref[...] = x2


def _combine(lists, y_rows, prob, x1, mod, boff, g_final, roff, t, ne, final):
    n, d = x1.shape
    tt = ROW_TILE
    r0 = roff // tt
    assert roff % tt == 0 and n % tt == 0
    last = r0 + n // tt - 1
    return pl.pallas_call(
        functools.partial(_combine_kernel, ne=ne, final=final),
        grid=(n // tt,),
        in_specs=[
            _list_spec(ne, lambda i: (r0 + i, 0, 0)),
            _list_spec(ne, lambda i: (jnp.minimum(r0 + i + 1, last), 0, 0)),
            pl.BlockSpec(memory_space=pl.ANY),
            pl.BlockSpec((tt, V7X_LANES), lambda i: (i, 0)),
            pl.BlockSpec((tt, d), lambda i: (i, 0)),
            _mod_spec(t, boff, d),
            pl.BlockSpec((1, d), lambda i: (0, 0)),
        ],
        out_specs=pl.BlockSpec((tt, d), lambda i: (i, 0)),
        scratch_shapes=[pltpu.VMEM((2, _tile_slots(ne) * ROW_PARTS, V7X_LANES), jnp.uint32),
                        pltpu.SemaphoreType.DMA((2,))],
        out_shape=jax.ShapeDtypeStruct((n, d), F32),
        compiler_params=_params(("arbitrary",)),
        name="combine",
    )(lists, lists, y_rows, prob, x1, mod, g_final)


def _block_diag(w):
    depth, nb, c, _ = w.shape
    eye = jnp.eye(nb, dtype=w.dtype)
    return (eye[None, :, None, :, None] * w[:, :, :, None, :]).reshape(depth, nb * c, nb * c)


def kernel(x_prompt, x_sample, state_conv, state_rglru, state_hgrn, c_prompt, c_sample, w_ada, b_ada, g_norm1, g_norm2, w_in, conv_w, conv_b, w_rgate, b_rgate, w_igate, b_igate, lru_lambda, hgrn_lb, hgrn_norm_g, w_out, w_router, b_router, w_gate, b_gate, w_up, b_up, w_down, b_down, g_final):
    depth = w_ada.shape[0]
    bp, tp, d = x_prompt.shape
    bs, ts, _ = x_sample.shape
    w_rg = conv_w.shape[2]
    ne = w_router.shape[2]
    n_p, n_s = bp * tp, bs * ts
    n_tot = n_p + n_s
    span = max(max(ROW_TILE * OUTPROJ_PARTS, INPROJ_TILE) // ts, 1)
    boff_s = -(-bp // span) * span
    groups = ((bp, tp, 0, 0), (bs, ts, boff_s, n_p))
    n_c_pad = -(-(boff_s + bs) // V7X_SUBLANES) * V7X_SUBLANES
    c_all = jnp.concatenate([c_prompt, jnp.zeros((boff_s - bp, d), F32), c_sample,
                             jnp.zeros((n_c_pad - boff_s - bs, d), F32)], axis=0)
    mod_all = _ada(c_all, w_ada, b_ada).reshape(depth, n_c_pad, 6, d)

    p_lb = jax.nn.softmax(hgrn_lb.astype(F32), axis=0)
    lb_all = jnp.cumsum(p_lb, axis=0) - p_lb[0]

    w_in_b = w_in.astype(BF16)
    w_out_b = w_out.astype(BF16)
    wr_all = _block_diag(w_rgate).astype(BF16)
    wi_all = _block_diag(w_igate).astype(BF16)

    assert d // 2 == ROW_PARTS * V7X_LANES
    run_pad = (n_tot // ROW_TILE) * ne * (RUN_ALIGN - 1)
    n_tiles = -(-(TOP_K * n_tot + run_pad + ne * (FFN_TILE - 1)) // FFN_TILE)
    n_rows = n_tiles * FFN_TILE

    xs = [x_prompt.reshape(n_p, d), x_sample.reshape(n_s, d)]
    conv_in = [jnp.zeros((depth, bp, CONV_W - 1, w_rg), F32), state_conv]
    h_in = [jnp.zeros((depth, bp, w_rg), F32), state_rglru]
    s_in = [jnp.zeros((depth, bp, HG_HEADS, HG_D, HG_D), F32), state_hgrn]
    conv_out, h_out, s_out = ([], []), ([], []), ([], [])

    for l in range(depth):
        mod = mod_all[l]
        wr_d, wi_d = wr_all[l], wi_all[l]
        cnt = jnp.zeros((ne, 1), F32)
        x1s, hbs, poss, probs, segbs, segls = [], [], [], [], [], []
        for gi, (bsz, t, boff, _) in enumerate(groups):
            proj = _inproj(xs[gi], mod, boff, g_norm1[l][None], w_in_b[l], t)
            o_rg, c_new, h_last = _rglru(
                proj, conv_in[gi][l], h_in[gi][l], conv_w[l], conv_b[l][None], wr_d, b_rgate[l][None],
                wi_d, b_igate[l][None], lru_lambda[l][None], bsz, t)
            o_hg, s_last = _hgrn(proj, lb_all[l][None], hgrn_norm_g[l][None], s_in[gi][l], bsz, t)
            x1, hb, pos_t, prob, segb, segl, cnt = _outproj(
                o_rg, o_hg, xs[gi], mod, boff, w_out_b[l], g_norm2[l][None], _router_weights(w_router[l]),
                b_router[l][:, None], cnt, t)
            conv_out[gi].append(c_new)
            h_out[gi].append(h_last[:, 0])
            s_out[gi].append(s_last)
            x1s.append(x1)
            hbs.append(hb)
            poss.append(pos_t)
            probs.append(prob)
            segbs.append(segb[:, :, 0])
            segls.append(segl[:, :, 0])

        counts = cnt[:, 0].astype(I32)
        padded = (counts + FFN_TILE - 1) // FFN_TILE * FFN_TILE
        pad_end = jnp.cumsum(padded)
        off = jnp.concatenate([jnp.zeros((1,), I32), pad_end]).astype(I32)
        n_used = (pad_end[-1] // FFN_TILE).astype(I32)
        tile_start = jnp.minimum(jnp.arange(n_tiles, dtype=I32), n_used - 1) * FFN_TILE
        tile_expert = jnp.minimum(jnp.sum(tile_start[:, None] >= pad_end[None, :], axis=1), ne - 1).astype(I32)
        eidx = jnp.arange(ne, dtype=I32)
        of_tile = lambda table: jnp.sum(jnp.where(tile_expert[:, None] == eidx[None, :], table[None, :], 0), axis=1)
        tile_idx = jnp.arange(n_tiles, dtype=I32)
        tile_rows = jnp.where(
            tile_idx < n_used,
            jnp.clip(of_tile(counts) - (tile_idx * FFN_TILE - of_tile(off[:ne])), 0, FFN_TILE), 0).astype(I32)
        later_used = (counts[None, :] > 0) & (eidx[None, :] > eidx[:, None])
        next_used = jnp.min(jnp.where(later_used, eidx[None, :], ne), axis=1)
        next_expert = of_tile(jnp.where(next_used < ne, next_used, -1)).astype(I32)
        lists = _run_lists(off[None, :ne] + jnp.concatenate(segbs, axis=0).astype(I32),
                           jnp.concatenate(segls, axis=0).astype(I32), ne)
        zoff = jnp.concatenate([off, jnp.full((1,), n_rows, I32)])
        zcnt = jnp.concatenate([counts, jnp.zeros((1,), I32)])

        x_sorted = _dispatch(lists, zoff, zcnt, jnp.concatenate(poss, axis=1), hbs[0], hbs[1],
                             n_rows, ne)
        y_rows = _ffn(l, tile_expert, tile_rows, next_expert, x_sorted, w_gate, b_gate[:, :, None, :],
                      w_up, b_up[:, :, None, :], w_down, b_down[:, :, None, :])
        final = l == depth - 1
        xs = [_combine(lists, y_rows, probs[gi], x1s[gi], mod, boff, g_final[None],
                       roff, t, ne, final)
              for gi, (bsz, t, boff, roff) in enumerate(groups)]

    return (xs[0].reshape(bp, tp, d), xs[1].reshape(bs, ts, d),
            jnp.stack(conv_out[0]), jnp.stack(h_out[0]), jnp.stack(s_out[0]),
            jnp.stack(conv_out[1]), jnp.stack(h_out[1]), jnp.stack(s_out[1]))
```
